```python
import math
import jax
import jax.numpy as jnp
from jax import lax
import numpy as np


D_MODEL = 1024
BATCH = 8
SEQ = 4096
DEPTH = 1

GRID_W = 64
CTX_LEN = 256
D_SSD = D_MODEL
SSD_HEADDIM = 64
SSD_HEADS = D_SSD // SSD_HEADDIM
SSD_GROUPS = 4
SSD_HEADS_PER_GROUP = SSD_HEADS // SSD_GROUPS
SSD_STATE = 128
CONV_W = 5
CHUNK = 128
CONV_DIM = D_SSD + 2 * SSD_GROUPS * SSD_STATE
D_FOURIER = D_MODEL // 2
FOURIER_GROUPS = 4
FOURIER_CH = D_FOURIER // FOURIER_GROUPS
D_MIX = D_SSD + D_FOURIER
D_PROJ = D_SSD + CONV_DIM + 2 * SSD_HEADS + D_FOURIER
N_EXPERT_GROUPS = 4
EXPERTS_PER_GROUP = 8
N_EXPERTS = N_EXPERT_GROUPS * EXPERTS_PER_GROUP
TOP_K_INNER = 2
D_EXPERT = D_MODEL // 2
EXPERT_BLOCK = 128
EPS = 1e-6

kernel_name = 'hybrid_ssd_fourier_hmoe_dit_block'


def rmsnorm(u, w):
    uf = u.astype(jnp.float32)
    uf = uf * lax.rsqrt(jnp.mean(uf * uf, axis=-1, keepdims=True) + EPS)
    return uf.astype(u.dtype) * w


def modulate(u, shift, scale):
    return u * (1 + scale) + shift


def depthwise_conv(u, w):
    return lax.conv_general_dilated(u, w[:, None, :], window_strides=(1,), padding='SAME',
                                    dimension_numbers=('NWC', 'WIO', 'NWC'),
                                    feature_group_count=u.shape[-1])


def ssd_scan(xd, da, bm, cm, h0):
    out_dtype = xd.dtype
    b, l = xd.shape[0], xd.shape[1]
    nc = l // CHUNK
    g, r, p, n = SSD_GROUPS, SSD_HEADS_PER_GROUP, SSD_HEADDIM, SSD_STATE
    f32 = jnp.float32
    xd = xd.astype(f32).reshape(b, nc, CHUNK, g, r, p)
    da = da.astype(f32).reshape(b, nc, CHUNK, g, r)
    bm = bm.astype(f32).reshape(b, nc, CHUNK, g, n)
    cm = cm.astype(f32).reshape(b, nc, CHUNK, g, n)
    a_cs = jnp.cumsum(da, axis=2)
    a_lr = jnp.moveaxis(a_cs, 2, -1)
    lower = jnp.tril(jnp.ones((CHUNK, CHUNK), bool))
    seg = jnp.where(lower, a_lr[..., :, None] - a_lr[..., None, :], -jnp.inf)
    cb = jnp.einsum('bclgn,bcsgn->bcgls', cm, bm)
    y_diag = jnp.einsum('bcgrls,bcsgrp->bclgrp', cb[:, :, :, None] * jnp.exp(seg), xd)
    decay_to_end = jnp.exp(a_cs[:, :, -1:] - a_cs)
    chunk_states = jnp.einsum('bcqgn,bcqgrp->bcgrpn', bm, xd * decay_to_end[..., None])
    chunk_decay = jnp.exp(a_cs[:, :, -1])

    def carry_state(h, inp):
        dec, st = inp
        return h * dec[..., None, None] + st, h

    h_final, h_in = lax.scan(carry_state, h0.astype(f32),
                             (jnp.moveaxis(chunk_decay, 1, 0), jnp.moveaxis(chunk_states, 1, 0)))
    h_in = jnp.moveaxis(h_in, 0, 1)
    y_off = jnp.einsum('bcqgn,bcgrpn->bcqgrp', cm, h_in) * jnp.exp(a_cs)[..., None]
    y = (y_diag + y_off).reshape(b, l, SSD_HEADS, p)
    return y.astype(out_dtype), h_final


def ssd_mixer(z, xbc, dt_raw, h0_f, h0_b, conv_w, conv_b, dt_bias, a_log, d_skip, norm_w):
    b, l = xbc.shape[0], xbc.shape[1]
    xbc = jax.nn.silu(depthwise_conv(xbc, conv_w) + conv_b)
    xs, bm, cm = jnp.split(xbc, [D_SSD, D_SSD + SSD_GROUPS * SSD_STATE], axis=-1)
    xs = xs.reshape(b, l, SSD_HEADS, SSD_HEADDIM)
    bm = bm.reshape(b, l, SSD_GROUPS, SSD_STATE)
    cm = cm.reshape(b, l, SSD_GROUPS, SSD_STATE)
    dt = jax.nn.softplus(dt_raw.reshape(b, l, 2, SSD_HEADS) + dt_bias)
    da = dt * -jnp.exp(a_log)
    flip = lambda u: jnp.flip(u, axis=1)
    y_f, h_f = ssd_scan(xs * dt[:, :, 0, :, None], da[:, :, 0], bm, cm, h0_f)
    y_b, h_b = ssd_scan(flip(xs * dt[:, :, 1, :, None]), flip(da[:, :, 1]), flip(bm), flip(cm), h0_b)
    y = y_f + flip(y_b) + d_skip[:, None] * xs
    y = y.reshape(b, l, D_SSD) * jax.nn.silu(z)
    return rmsnorm(y, norm_w), h_f, h_b


def fourier_mixer(f, w_four):
    b, l = f.shape[0], f.shape[1]
    fg = f.reshape(b, l, FOURIER_GROUPS, FOURIER_CH).astype(jnp.float32)
    spec = jnp.fft.fft2(fg, axes=(1, 3), norm='ortho').real.astype(f.dtype)
    return jnp.einsum('blgc,gcd->blgd', spec, w_four).reshape(b, l, D_FOURIER)


def split_proj(proj):
    return jnp.split(proj, [D_SSD, D_SSD + CONV_DIM, D_SSD + CONV_DIM + 2 * SSD_HEADS], axis=-1)


def hier_moe(h, w_rg, b_rg, w_re, b_re, w_eg, w_eu, w_ed):
    bsz, l, d = h.shape
    t = h.reshape(bsz * l, d)
    n_tok = t.shape[0]
    f32 = jnp.float32
    p_grp = jax.nn.softmax((t @ w_rg + b_rg).astype(f32), axis=-1)
    p_top_grp, grp = lax.top_k(p_grp, 1)
    logit_exp = jnp.einsum('td,gde->tge', t, w_re) + b_re
    sel = jnp.broadcast_to(grp[:, :, None], (n_tok, 1, EXPERTS_PER_GROUP))
    logit_sel = jnp.take_along_axis(logit_exp, sel, axis=1)[:, 0]
    p_exp = jax.nn.softmax(logit_sel.astype(f32), axis=-1)
    p_top, j_top = lax.top_k(p_exp, TOP_K_INNER)
    gate = p_top_grp * p_top / jnp.sum(p_top, axis=-1, keepdims=True)
    eid = grp * EXPERTS_PER_GROUP + j_top
    n_assign = n_tok * TOP_K_INNER
    e_flat = eid.reshape(-1)
    g_flat = gate.reshape(-1)
    tok_flat = jnp.repeat(jnp.arange(n_tok, dtype=jnp.int32), TOP_K_INNER)
    order = jnp.argsort(e_flat)
    e_s, tok_s, g_s = e_flat[order], tok_flat[order], g_flat[order]
    counts = jnp.bincount(e_flat, length=N_EXPERTS)
    padded = (counts + EXPERT_BLOCK - 1) // EXPERT_BLOCK * EXPERT_BLOCK
    start = jnp.cumsum(counts) - counts
    end_pad = jnp.cumsum(padded)
    start_pad = end_pad - padded
    dest = start_pad[e_s] + jnp.arange(n_assign, dtype=jnp.int32) - start[e_s]
    n_blocks = -(-(n_assign + N_EXPERTS * (EXPERT_BLOCK - 1)) // EXPERT_BLOCK)
    n_rows = n_blocks * EXPERT_BLOCK
    row_tok = jnp.full((n_rows,), n_tok, jnp.int32).at[dest].set(tok_s)
    row_gate = jnp.zeros((n_rows,), f32).at[dest].set(g_s)
    blk_exp = jnp.minimum(jnp.searchsorted(end_pad, jnp.arange(n_blocks) * EXPERT_BLOCK, side='right'),
                          N_EXPERTS - 1)
    t_pad = jnp.concatenate([t, jnp.zeros((1, d), t.dtype)], axis=0)
    rows = t_pad[row_tok].reshape(n_blocks, EXPERT_BLOCK, d)

    def expert_block(args):
        xb, e = args
        return (jax.nn.silu(xb @ w_eg[e]) * (xb @ w_eu[e])) @ w_ed[e]

    y_rows = lax.map(expert_block, (rows, blk_exp)).reshape(n_rows, d)
    y = jax.ops.segment_sum(y_rows.astype(f32) * row_gate[:, None], row_tok,
                            num_segments=n_tok + 1)[:n_tok]
    return y.astype(h.dtype).reshape(bsz, l, d)


def setup_inputs(seed: int = 0) -> dict:
    key = jax.random.key(seed)
    ks = jax.random.split(key, 26)
    f32 = jnp.float32

    def nrm(k, shape, scale):
        return jax.random.normal(k, shape, f32) * scale

    def gain(k, shape):
        return 1.0 + 0.05 * jax.random.normal(k, shape, f32)

    dt0 = jnp.exp(jax.random.uniform(ks[10], (DEPTH, 2, SSD_HEADS), f32,
                                     math.log(1e-3), math.log(1e-1)))
    dt_bias = dt0 + jnp.log(-jnp.expm1(-dt0))
    a_log = jnp.log(jax.random.uniform(ks[11], (DEPTH, 2, SSD_HEADS), f32, 1.0, 16.0))
    return {
        'x': nrm(ks[0], (BATCH, SEQ, D_MODEL), 1.0),
        'c': nrm(ks[1], (BATCH, D_MODEL), 1.0),
        'ctx': nrm(ks[2], (BATCH, CTX_LEN, D_MODEL), 1.0),
        'c_ctx': nrm(ks[3], (D_MODEL,), 1.0),
        'w_mod': nrm(ks[4], (DEPTH, D_MODEL, 6 * D_MODEL), D_MODEL ** -0.5),
        'b_mod': nrm(ks[5], (DEPTH, 6 * D_MODEL), 0.02),
        'norm1': gain(ks[6], (DEPTH, D_MODEL)),
        'w_in': nrm(ks[7], (DEPTH, D_MODEL, D_PROJ), D_MODEL ** -0.5),
        'conv_w': nrm(ks[8], (DEPTH, CONV_W, CONV_DIM), CONV_W ** -0.5),
        'conv_b': nrm(ks[9], (DEPTH, CONV_DIM), 0.02),
        'dt_bias': dt_bias,
        'a_log': a_log,
        'd_skip': gain(ks[12], (DEPTH, SSD_HEADS)),
        'ssd_norm': gain(ks[13], (DEPTH, D_SSD)),
        'w_four': nrm(ks[14], (DEPTH, FOURIER_GROUPS, FOURIER_CH, FOURIER_CH), FOURIER_CH ** -0.5),
        'w_out': nrm(ks[15], (DEPTH, D_MIX, D_MODEL), D_MIX ** -0.5),
        'norm2': gain(ks[16], (DEPTH, D_MODEL)),
        'w_rg': nrm(ks[17], (DEPTH, D_MODEL, N_EXPERT_GROUPS), D_MODEL ** -0.5),
        'b_rg': nrm(ks[18], (DEPTH, N_EXPERT_GROUPS), 0.01),
        'w_re': nrm(ks[19], (DEPTH, N_EXPERT_GROUPS, D_MODEL, EXPERTS_PER_GROUP), D_MODEL ** -0.5),
        'b_re': nrm(ks[20], (DEPTH, N_EXPERT_GROUPS, EXPERTS_PER_GROUP), 0.01),
        'w_eg': nrm(ks[21], (DEPTH, N_EXPERTS, D_MODEL, D_EXPERT), D_MODEL ** -0.5),
        'w_eu': nrm(ks[22], (DEPTH, N_EXPERTS, D_MODEL, D_EXPERT), D_MODEL ** -0.5),
        'w_ed': nrm(ks[23], (DEPTH, N_EXPERTS, D_EXPERT, D_MODEL), D_EXPERT ** -0.5),
        'final_norm': gain(ks[24], (D_MODEL,)),
    }


def reference(x, c, ctx, c_ctx, w_mod, b_mod, norm1, w_in, conv_w, conv_b, dt_bias, a_log,
              d_skip, ssd_norm, w_four, w_out, norm2, w_rg, b_rg, w_re, b_re, w_eg, w_eu, w_ed,
              final_norm):
    n_lat = x.shape[1]
    rows = n_lat // GRID_W
    assert rows * GRID_W == n_lat
    bsz = x.shape[0]
    h_zero = jnp.zeros((bsz, SSD_GROUPS, SSD_HEADS_PER_GROUP, SSD_HEADDIM, SSD_STATE), jnp.float32)
    for layer in range(DEPTH):
        last = layer == DEPTH - 1
        mod_lat = jax.nn.silu(c) @ w_mod[layer] + b_mod[layer]
        mod_ctx = jax.nn.silu(c_ctx) @ w_mod[layer] + b_mod[layer]
        sh1, sc1, g1, sh2, sc2, g2 = jnp.split(mod_lat[:, None, :], 6, axis=-1)
        sh1c, sc1c, g1c, sh2c, sc2c, g2c = jnp.split(mod_ctx, 6, axis=-1)
        prj_c = modulate(rmsnorm(ctx, norm1[layer]), sh1c, sc1c) @ w_in[layer]
        prj_l = modulate(rmsnorm(x, norm1[layer]), sh1, sc1) @ w_in[layer]
        z_c, xbc_c, dt_c, f_c = split_proj(prj_c)
        z_l, xbc_l, dt_l, f_l = split_proj(prj_l)
        y_c, hf_c, hb_c = ssd_mixer(z_c, xbc_c, dt_c, h_zero, h_zero, conv_w[layer], conv_b[layer],
                                    dt_bias[layer], a_log[layer], d_skip[layer], ssd_norm[layer])
        y_l, _, _ = ssd_mixer(z_l, xbc_l, dt_l, hf_c, hb_c, conv_w[layer], conv_b[layer],
                              dt_bias[layer], a_log[layer], d_skip[layer], ssd_norm[layer])
        mix_l = jnp.concatenate([y_l, fourier_mixer(f_l, w_four[layer])], axis=-1) @ w_out[layer]
        x = x + g1 * mix_l
        if not last:
            mix_c = jnp.concatenate([y_c, fourier_mixer(f_c, w_four[layer])], axis=-1) @ w_out[layer]
            ctx = ctx + g1c * mix_c
            ctx = ctx + g2c * hier_moe(modulate(rmsnorm(ctx, norm2[layer]), sh2c, sc2c), w_rg[layer],
                                       b_rg[layer], w_re[layer], b_re[layer], w_eg[layer],
                                       w_eu[layer], w_ed[layer])
        x = x + g2 * hier_moe(modulate(rmsnorm(x, norm2[layer]), sh2, sc2), w_rg[layer], b_rg[layer],
                              w_re[layer], b_re[layer], w_eg[layer], w_eu[layer], w_ed[layer])
    return rmsnorm(x, final_norm)
```

```python
import functools
import math

import jax
import jax.numpy as jnp
from jax import lax
from jax.experimental import pallas as pl
from jax.experimental.pallas import tpu as pltpu

F32 = jnp.float32
BF16 = jnp.bfloat16
HIGHEST = lax.Precision.HIGHEST

EPS = 1e-6
CHUNK = 128
N_GROUPS = 4
HEADS_PER_GROUP = 4
HEADDIM = 64
D_STATE = 128
CONV_W = 5
CONV_HALO = 16
N_FOURIER_GROUPS = 4
FOURIER_CH = 128
N_EXPERT_GROUPS = 4
EXPERTS_PER_GROUP = 8
N_EXPERTS = N_EXPERT_GROUPS * EXPERTS_PER_GROUP
ROUTE_LANES = 128
MOE_ROWS = 256
VMEM_LIMIT_BYTES = 56 * 1024 * 1024


def _params(*sem):
    return pltpu.CompilerParams(dimension_semantics=sem, vmem_limit_bytes=VMEM_LIMIT_BYTES)


def _silu(v):
    return v * jax.nn.sigmoid(v)


def _mod_kernel(c_ref, w_ref, b_ref, o_ref):
    s = _silu(c_ref[...])
    o_ref[...] = jnp.dot(s, w_ref[...], preferred_element_type=F32, precision=HIGHEST) + b_ref[...]


def _modulation(c_rows, w_mod, b_mod):
    rows, d = c_rows.shape
    n = w_mod.shape[1]
    tn = 512
    return pl.pallas_call(
        _mod_kernel,
        grid=(n // tn,),
        in_specs=[pl.BlockSpec((rows, d), lambda j: (0, 0)),
                  pl.BlockSpec((d, tn), lambda j: (0, j)),
                  pl.BlockSpec((1, tn), lambda j: (0, j))],
        out_specs=pl.BlockSpec((rows, tn), lambda j: (0, j)),
        out_shape=jax.ShapeDtypeStruct((rows, n), F32),
        compiler_params=_params("arbitrary"),
        name="modulation",
    )(c_rows, w_mod, b_mod.reshape(1, n))


def _inproj_kernel(x_ref, sh_ref, sc_ref, nw_ref, wx_ref, wdt_ref, *rest, with_zf):
    if with_zf:
        wz_ref, wf_ref, xbc_ref, dt_ref, z_ref, f_ref = rest
    else:
        xbc_ref, dt_ref = rest
    x = x_ref[0]
    ms = jnp.mean(x * x, axis=-1, keepdims=True)
    xn = x * lax.rsqrt(ms + EPS) * nw_ref[...]
    xm = (xn * (1.0 + sc_ref[0]) + sh_ref[0]).astype(BF16)
    xbc_ref[0] = jnp.dot(xm, wx_ref[...], preferred_element_type=F32).astype(BF16)
    dt = jnp.dot(xm, wdt_ref[...], preferred_element_type=F32)
    for g in range(N_GROUPS):
        dt_ref[0, g] = dt[:, 8 * g:8 * g + 8]
    if with_zf:
        z_ref[0] = jnp.dot(xm, wz_ref[...], preferred_element_type=F32).astype(BF16)
        f_ref[0] = jnp.dot(xm, wf_ref[...], preferred_element_type=F32).astype(BF16)


def _in_projection(x, shift, scale, norm_w, wx, wdt, wz=None, wf=None):
    b, l, d = x.shape
    tm = min(512, l)
    with_zf = wz is not None
    row = lambda bi, i: (bi, i, 0)
    vec = lambda bi, i: (bi, 0, 0)
    const = lambda bi, i: (0, 0)
    in_specs = [pl.BlockSpec((1, tm, d), row),
                pl.BlockSpec((1, 1, d), vec),
                pl.BlockSpec((1, 1, d), vec),
                pl.BlockSpec((1, d), const),
                pl.BlockSpec(wx.shape, const),
                pl.BlockSpec(wdt.shape, const)]
    args = [x, shift, scale, norm_w, wx, wdt]
    out_specs = [pl.BlockSpec((1, tm, wx.shape[1]), row),
                 pl.BlockSpec((1, N_GROUPS, tm, 8), lambda bi, i: (bi, 0, i, 0))]
    out_shape = [jax.ShapeDtypeStruct((b, l, wx.shape[1]), BF16),
                 jax.ShapeDtypeStruct((b, N_GROUPS, l, 8), F32)]
    if with_zf:
        in_specs += [pl.BlockSpec(wz.shape, const), pl.BlockSpec(wf.shape, const)]
        args += [wz, wf]
        out_specs += [pl.BlockSpec((1, tm, wz.shape[1]), row), pl.BlockSpec((1, tm, wf.shape[1]), row)]
        out_shape += [jax.ShapeDtypeStruct((b, l, wz.shape[1]), BF16),
                      jax.ShapeDtypeStruct((b, l, wf.shape[1]), BF16)]
    return pl.pallas_call(
        functools.partial(_inproj_kernel, with_zf=with_zf),
        grid=(b, l // tm),
        in_specs=in_specs,
        out_specs=out_specs,
        out_shape=out_shape,
        compiler_params=_params("parallel", "arbitrary"),
        name="in_projection_zf" if with_zf else "in_projection",
    )(*args)


def _conv_silu(in_ref, w_ref, b_ref, out_ref, seq):
    ch = in_ref.shape[-1]
    rb = min(512, seq)
    w = w_ref[...]
    bias = b_ref[...]
    for i in range(seq // rb):
        lo, hi = i * rb - CONV_HALO, (i + 1) * rb + CONV_HALO
        parts = []
        if lo < 0:
            parts.append(jnp.zeros((CONV_HALO, ch), F32))
        parts.append(in_ref[0, max(lo, 0):min(hi, seq), :].astype(F32))
        if hi > seq:
            parts.append(jnp.zeros((CONV_HALO, ch), F32))
        win = jnp.concatenate(parts, axis=0) if len(parts) > 1 else parts[0]
        acc = jnp.broadcast_to(bias, (rb, ch))
        for k in range(CONV_W):
            off = CONV_HALO - CONV_W // 2 + k
            acc = acc + w[k:k + 1, :] * win[off:off + rb, :]
        out_ref[i * rb:(i + 1) * rb, :] = _silu(acc)


def _ssd_kernel(xs_ref, xb_ref, xc_ref, dt_ref, z_ref,
                cwx_ref, cwb_ref, cwc_ref, cbx_ref, cbb_ref, cbc_ref,
                dtb_ref, alog_ref, dsk_ref, h0f_ref, h0b_ref,
                y_ref, hf_ref, hb_ref,
                xs_s, b_s, c_s, dt_s, da_s, yf_s, hst_s, *, seq, emit_y):
    nc = seq // CHUNK
    _conv_silu(xs_ref, cwx_ref, cbx_ref, xs_s, seq)
    _conv_silu(xb_ref, cwb_ref, cbb_ref, b_s, seq)
    _conv_silu(xc_ref, cwc_ref, cbc_ref, c_s, seq)

    dtr = dt_ref[0, 0] + dtb_ref[0]
    dtv = jnp.maximum(dtr, 0.0) + jnp.log(1.0 + jnp.exp(-jnp.abs(dtr)))
    dt_s[...] = dtv
    da_s[...] = dtv * (-jnp.exp(alog_ref[0]))

    ri = lax.broadcasted_iota(jnp.int32, (CHUNK, CHUNK), 0)
    ci = lax.broadcasted_iota(jnp.int32, (CHUNK, CHUNK), 1)
    lane_pad = jnp.zeros((CHUNK, CHUNK - 8), F32)

    def scan_direction(d, h0_ref, hout_ref):
        for j in range(HEADS_PER_GROUP):
            hst_s[j] = h0_ref[0, 0, j]
        tri = (ci <= ri) if d == 0 else (ci >= ri)
        tri_f = tri.astype(F32)
        tot_row = CHUNK - 1 if d == 0 else 0

        def body(t, carry):
            c = t if d == 0 else nc - 1 - t
            r0 = pl.multiple_of(c * CHUNK, CHUNK)
            rows = pl.ds(r0, CHUNK)
            xs = xs_s[rows, :]
            bc = b_s[rows, :]
            cc = c_s[rows, :]
            da_p = jnp.concatenate([da_s[rows, :], lane_pad], axis=1)
            dt_p = jnp.concatenate([dt_s[rows, :], lane_pad], axis=1)
            cs = jnp.dot(tri_f, da_p, preferred_element_type=F32, precision=HIGHEST)
            cs_t = cs.T
            dt_t = dt_p.T
            bt = bc.T
            cb = jnp.dot(cc.astype(BF16), bt.astype(BF16), preferred_element_type=F32)
            ys = []
            for j in range(HEADS_PER_GROUP):
                lane = HEADS_PER_GROUP * d + j
                a_col1 = cs[:, lane:lane + 1]
                a_col = jnp.broadcast_to(a_col1, (CHUNK, CHUNK))
                a_tot = a_col1[tot_row:tot_row + 1, :]
                a_row = cs_t[lane:lane + 1, :]
                dt_row = dt_t[lane:lane + 1, :]
                decay = jnp.where(tri, jnp.exp(a_col - a_row), 0.0)
                g = (cb * decay * dt_row).astype(BF16)
                ea = jnp.exp(a_col)
                cea = (cc * ea).astype(BF16)
                xh = xs[:, HEADDIM * j:HEADDIM * (j + 1)].astype(BF16)
                h = hst_s[j]
                y_h = (jnp.dot(g, xh, preferred_element_type=F32)
                       + jnp.dot(cea, h.astype(BF16), preferred_element_type=F32))
                w_row = jnp.exp(a_tot - a_row) * dt_row
                s_new = jnp.dot((bt * w_row).astype(BF16), xh, preferred_element_type=F32)
                hst_s[j] = h * jnp.exp(a_tot) + s_new
                ys.append(y_h)
            y_c = jnp.concatenate(ys, axis=1)
            if emit_y:
                if d == 0:
                    yf_s[rows, :] = y_c + dsk_ref[...] * xs
                else:
                    zc = z_ref[0, rows, :].astype(F32)
                    y_ref[0, rows, :] = ((yf_s[rows, :] + y_c) * _silu(zc)).astype(BF16)
            return carry

        lax.fori_loop(0, nc, body, 0)
        for j in range(HEADS_PER_GROUP):
            hout_ref[0, 0, j] = hst_s[j]

    scan_direction(0, h0f_ref, hf_ref)
    scan_direction(1, h0b_ref, hb_ref)
    if not emit_y:
        y_ref[...] = jnp.zeros(y_ref.shape, y_ref.dtype)


def _ssd_mixer(xbc, dt, z, conv_w, conv_b, dtb, alog, dskip, h0f, h0b, emit_y):
    b, l, _ = xbc.shape
    gw = HEADS_PER_GROUP * HEADDIM
    nxb = (N_GROUPS * gw) // D_STATE
    y_rows = l if emit_y else 8
    st_shape = (b, N_GROUPS, HEADS_PER_GROUP, D_STATE, HEADDIM)
    st_spec = pl.BlockSpec((1, 1, HEADS_PER_GROUP, D_STATE, HEADDIM), lambda bi, g: (bi, g, 0, 0, 0))
    in_specs = [
        pl.BlockSpec((1, l, gw), lambda bi, g: (bi, 0, g)),
        pl.BlockSpec((1, l, D_STATE), lambda bi, g: (bi, 0, nxb + g)),
        pl.BlockSpec((1, l, D_STATE), lambda bi, g: (bi, 0, nxb + N_GROUPS + g)),
        pl.BlockSpec((1, 1, l, 8), lambda bi, g: (bi, g, 0, 0)),
        pl.BlockSpec((1, y_rows, gw), lambda bi, g: (bi, 0, g)),
        pl.BlockSpec((CONV_W, gw), lambda bi, g: (0, g)),
        pl.BlockSpec((CONV_W, D_STATE), lambda bi, g: (0, nxb + g)),
        pl.BlockSpec((CONV_W, D_STATE), lambda bi, g: (0, nxb + N_GROUPS + g)),
        pl.BlockSpec((1, gw), lambda bi, g: (0, g)),
        pl.BlockSpec((1, D_STATE), lambda bi, g: (0, nxb + g)),
        pl.BlockSpec((1, D_STATE), lambda bi, g: (0, nxb + N_GROUPS + g)),
        pl.BlockSpec((1, 1, 8), lambda bi, g: (g, 0, 0)),
        pl.BlockSpec((1, 1, 8), lambda bi, g: (g, 0, 0)),
        pl.BlockSpec((1, gw), lambda bi, g: (0, g)),
        st_spec, st_spec,
    ]
    out_specs = [pl.BlockSpec((1, y_rows, gw), lambda bi, g: (bi, 0, g)), st_spec, st_spec]
    out_shape = [jax.ShapeDtypeStruct((b, y_rows, N_GROUPS * gw), BF16),
                 jax.ShapeDtypeStruct(st_shape, F32), jax.ShapeDtypeStruct(st_shape, F32)]
    scratch = [pltpu.VMEM((l, gw), F32), pltpu.VMEM((l, D_STATE), F32), pltpu.VMEM((l, D_STATE), F32),
               pltpu.VMEM((l, 8), F32), pltpu.VMEM((l, 8), F32),
               pltpu.VMEM((l if emit_y else 8, gw), F32),
               pltpu.VMEM((HEADS_PER_GROUP, D_STATE, HEADDIM), F32)]
    return pl.pallas_call(
        functools.partial(_ssd_kernel, seq=l, emit_y=emit_y),
        grid=(b, N_GROUPS),
        in_specs=in_specs, out_specs=out_specs, out_shape=out_shape,
        scratch_shapes=scratch,
        compiler_params=_params("parallel", "arbitrary"),
        name="ssd_mixer" if emit_y else "ssd_mixer_ctx",
    )(xbc, xbc, xbc, dt, z, conv_w, conv_w, conv_w, conv_b, conv_b, conv_b, dtb, alog, dskip, h0f, h0b)


def _fourier_kernel(f_ref, cc_ref, sc_ref, w_ref, cl_ref, sl_ref, o_ref, u_s, v_s, *, seq):
    @pl.when(pl.program_id(1) == 0)
    def _():
        scale = 1.0 / math.sqrt(seq * FOURIER_CH)
        for g in range(N_FOURIER_GROUPS):
            w = w_ref[g]
            a = jnp.dot(cc_ref[...], w, preferred_element_type=F32, precision=HIGHEST) * scale
            bm = jnp.dot(sc_ref[...], w, preferred_element_type=F32, precision=HIGHEST) * scale
            cols = slice(FOURIER_CH * g, FOURIER_CH * (g + 1))
            fg = f_ref[0, :, cols]
            u_s[:, cols] = jnp.dot(fg, a.astype(BF16), preferred_element_type=F32).astype(BF16)
            v_s[:, cols] = jnp.dot(fg, bm.astype(BF16), preferred_element_type=F32).astype(BF16)

    out = (jnp.dot(cl_ref[...], u_s[...], preferred_element_type=F32)
           - jnp.dot(sl_ref[...], v_s[...], preferred_element_type=F32))
    o_ref[0] = out.astype(BF16)


def _dft_tables(n, dtype):
    k = lax.broadcasted_iota(jnp.int32, (n, n), 0)
    l = lax.broadcasted_iota(jnp.int32, (n, n), 1)
    ang = ((k * l) % n).astype(F32) * (2.0 * math.pi / n)
    return jnp.cos(ang).astype(dtype), jnp.sin(ang).astype(dtype)


def _fourier_mixer(f, w_four):
    b, l, df = f.shape
    tr = min(512, l)
    cc, sc = _dft_tables(FOURIER_CH, F32)
    cl, sl = _dft_tables(l, BF16)
    return pl.pallas_call(
        functools.partial(_fourier_kernel, seq=l),
        grid=(b, l // tr),
        in_specs=[pl.BlockSpec((1, l, df), lambda bi, i: (bi, 0, 0)),
                  pl.BlockSpec((FOURIER_CH, FOURIER_CH), lambda bi, i: (0, 0)),
                  pl.BlockSpec((FOURIER_CH, FOURIER_CH), lambda bi, i: (0, 0)),
                  pl.BlockSpec(w_four.shape, lambda bi, i: (0, 0, 0)),
                  pl.BlockSpec((tr, l), lambda bi, i: (i, 0)),
                  pl.BlockSpec((tr, l), lambda bi, i: (i, 0))],
        out_specs=pl.BlockSpec((1, tr, df), lambda bi, i: (bi, i, 0)),
        out_shape=jax.ShapeDtypeStruct((b, l, df), BF16),
        scratch_shapes=[pltpu.VMEM((l, df), BF16), pltpu.VMEM((l, df), BF16)],
        compiler_params=_params("parallel", "arbitrary"),
        name="fourier_mixer",
    )(f, cc, sc, w_four, cl, sl)


def _outproj_router_kernel(y_ref, four_ref, x_ref, g1_ref, sh_ref, sc_ref, nssd_ref, n2_ref,
                           wos_ref, wof_ref, wr_ref, br_ref,
                           x1_ref, h_ref, eid_ref, gate_ref):
    y = y_ref[0].astype(F32)
    ms = jnp.mean(y * y, axis=-1, keepdims=True)
    yn = (y * lax.rsqrt(ms + EPS) * nssd_ref[...]).astype(BF16)
    mix = (jnp.dot(yn, wos_ref[...], preferred_element_type=F32)
           + jnp.dot(four_ref[0], wof_ref[...], preferred_element_type=F32))
    x1 = x_ref[0] + g1_ref[0] * mix
    x1_ref[0] = x1
    ms2 = jnp.mean(x1 * x1, axis=-1, keepdims=True)
    h = (x1 * lax.rsqrt(ms2 + EPS) * n2_ref[...]) * (1.0 + sc_ref[0]) + sh_ref[0]
    h_ref[0] = h

    lg = jnp.dot(h, wr_ref[...], preferred_element_type=F32, precision=HIGHEST) + br_ref[...]
    tm = lg.shape[0]
    lane = lax.broadcasted_iota(jnp.int32, (tm, ROUTE_LANES), 1)
    lane_f = lane.astype(F32)
    neg = jnp.float32(-1e30)
    big = jnp.float32(1e9)
    is_grp = lane < N_EXPERT_GROUPS
    gl = jnp.where(is_grp, lg, neg)
    gmax = jnp.max(gl, axis=-1, keepdims=True)
    gsum = jnp.sum(jnp.where(is_grp, jnp.exp(gl - gmax), 0.0), axis=-1, keepdims=True)
    grp = jnp.min(jnp.where(gl == gmax, lane_f, big), axis=-1, keepdims=True)
    p_grp = 1.0 / gsum
    lo = N_EXPERT_GROUPS + EXPERTS_PER_GROUP * grp
    in_grp = jnp.logical_and(lane_f >= lo, lane_f < lo + EXPERTS_PER_GROUP)
    el = jnp.where(in_grp, lg, neg)
    m1 = jnp.max(el, axis=-1, keepdims=True)
    i1 = jnp.min(jnp.where(el == m1, lane_f, big), axis=-1, keepdims=True)
    el2 = jnp.where(lane_f == i1, neg, el)
    m2 = jnp.max(el2, axis=-1, keepdims=True)
    i2 = jnp.min(jnp.where(el2 == m2, lane_f, big), axis=-1, keepdims=True)
    e2 = jnp.exp(m2 - m1)
    den = 1.0 + e2
    gate1 = p_grp / den
    gate2 = p_grp * e2 / den
    lane8 = lax.broadcasted_iota(jnp.int32, (tm, 8), 1)
    eid = jnp.where(lane8 == 0, i1 - N_EXPERT_GROUPS, jnp.where(lane8 == 1, i2 - N_EXPERT_GROUPS, 0.0))
    eid_ref[0] = eid.astype(jnp.int32)
    gate_ref[0] = jnp.where(lane8 == 0, gate1, jnp.where(lane8 == 1, gate2, 0.0))


def _outproj_router(y, four, x, g1, sh2, sc2, nssd, n2, wos, wof, wr, br):
    b, l, d = x.shape
    tm = min(512, l)
    row = lambda bi, i: (bi, i, 0)
    vec = lambda bi, i: (bi, 0, 0)
    const = lambda bi, i: (0, 0)
    return pl.pallas_call(
        _outproj_router_kernel,
        grid=(b, l // tm),
        in_specs=[pl.BlockSpec((1, tm, y.shape[2]), row),
                  pl.BlockSpec((1, tm, four.shape[2]), row),
                  pl.BlockSpec((1, tm, d), row),
                  pl.BlockSpec((1, 1, d), vec), pl.BlockSpec((1, 1, d), vec), pl.BlockSpec((1, 1, d), vec),
                  pl.BlockSpec((1, y.shape[2]), const), pl.BlockSpec((1, d), const),
                  pl.BlockSpec(wos.shape, const), pl.BlockSpec(wof.shape, const),
                  pl.BlockSpec(wr.shape, const), pl.BlockSpec(br.shape, const)],
        out_specs=[pl.BlockSpec((1, tm, d), row), pl.BlockSpec((1, tm, d), row),
                   pl.BlockSpec((1, tm, 8), row), pl.BlockSpec((1, tm, 8), row)],
        out_shape=[jax.ShapeDtypeStruct((b, l, d), F32), jax.ShapeDtypeStruct((b, l, d), F32),
                   jax.ShapeDtypeStruct((b, l, 8), jnp.int32), jax.ShapeDtypeStruct((b, l, 8), F32)],
        compiler_params=_params("parallel", "arbitrary"),
        name="outproj_router",
    )(y, four, x, g1, sh2, sc2, nssd, n2, wos, wof, wr, br)


def _expert_kernel(bexp_ref, bnv_ref, ra_ref, h_hbm, wg_ref, wu_ref, wd_ref, y_hbm,
                   hbuf, ybuf, wg_s, wu_s, wd_s, sem_in, sem_out, *, n_tok):
    blk = pl.program_id(0)
    nvalid = bnv_ref[blk]
    prev = bexp_ref[jnp.maximum(blk - 1, 0)]
    new_expert = jnp.logical_or(blk == 0, bexp_ref[blk] != prev)

    @pl.when(jnp.logical_and(nvalid > 0, new_expert))
    def _():
        wg_s[...] = wg_ref[0].astype(BF16)
        wu_s[...] = wu_ref[0].astype(BF16)
        wd_s[...] = wd_ref[0].astype(BF16)

    def in_copy(r, tok):
        return pltpu.make_async_copy(h_hbm.at[pl.ds(tok, 1), :], hbuf.at[pl.ds(r, 1), :], sem_in)

    def out_copy(r, a):
        return pltpu.make_async_copy(ybuf.at[pl.ds(r, 1), :], y_hbm.at[pl.ds(a, 1), :], sem_out)

    @pl.when(nvalid > 0)
    def _():
        def start_in(r, carry):
            a = jnp.maximum(ra_ref[0, 0, r], 0)
            tok = jnp.where(a >= n_tok, a - n_tok, a)
            in_copy(r, tok).start()
            return carry
        lax.fori_loop(0, MOE_ROWS, start_in, 0)

        def wait_in(r, carry):
            in_copy(r, 0).wait()
            return carry
        lax.fori_loop(0, MOE_ROWS, wait_in, 0)

        xb = hbuf[...].astype(BF16)
        gact = jnp.dot(xb, wg_s[...], preferred_element_type=F32)
        up = jnp.dot(xb, wu_s[...], preferred_element_type=F32)
        act = (_silu(gact) * up).astype(BF16)
        ybuf[...] = jnp.dot(act, wd_s[...], preferred_element_type=F32)

        def start_out(r, carry):
            out_copy(r, ra_ref[0, 0, r]).start()
            return carry
        lax.fori_loop(0, nvalid, start_out, 0)

        def wait_out(r, carry):
            out_copy(r, 0).wait()
            return carry
        lax.fori_loop(0, nvalid, wait_out, 0)


def _experts(h2d, row_assign, blk_exp, blk_nvalid, w_eg, w_eu, w_ed):
    n_tok, d = h2d.shape
    n_blocks = blk_exp.shape[0]
    de = w_eg.shape[2]
    grid_spec = pltpu.PrefetchScalarGridSpec(
        num_scalar_prefetch=2,
        grid=(n_blocks,),
        in_specs=[pl.BlockSpec((1, 1, MOE_ROWS), lambda i, be, bn: (i, 0, 0), memory_space=pltpu.SMEM),
                  pl.BlockSpec(memory_space=pl.ANY),
                  pl.BlockSpec((1, d, de), lambda i, be, bn: (be[i], 0, 0)),
                  pl.BlockSpec((1, d, de), lambda i, be, bn: (be[i], 0, 0)),
                  pl.BlockSpec((1, de, d), lambda i, be, bn: (be[i], 0, 0))],
        out_specs=pl.BlockSpec(memory_space=pl.ANY),
        scratch_shapes=[pltpu.VMEM((MOE_ROWS, d), F32), pltpu.VMEM((MOE_ROWS, d), F32),
                        pltpu.VMEM((d, de), BF16), pltpu.VMEM((d, de), BF16), pltpu.VMEM((de, d), BF16),
                        pltpu.SemaphoreType.DMA, pltpu.SemaphoreType.DMA],
    )
    return pl.pallas_call(
        functools.partial(_expert_kernel, n_tok=n_tok),
        grid_spec=grid_spec,
        out_shape=jax.ShapeDtypeStruct((2 * n_tok, d), F32),
        compiler_params=_params("arbitrary"),
        name="moe_experts",
    )(blk_exp, blk_nvalid, row_assign.reshape(n_blocks, 1, MOE_ROWS), h2d, w_eg, w_eu, w_ed)


def _route_tables(eid, n_tok):
    n_assign = 2 * n_tok
    e_flat = jnp.concatenate([eid[:, 0], eid[:, 1]])
    order = jnp.argsort(e_flat).astype(jnp.int32)
    e_sorted = e_flat[order]
    experts = jnp.arange(N_EXPERTS, dtype=jnp.int32)
    start = jnp.searchsorted(e_sorted, experts, side='left').astype(jnp.int32)
    counts = jnp.searchsorted(e_sorted, experts, side='right').astype(jnp.int32) - start
    padded = (counts + MOE_ROWS - 1) // MOE_ROWS * MOE_ROWS
    end_pad = jnp.cumsum(padded)
    start_pad = end_pad - padded
    n_blocks = -(-(n_assign + N_EXPERTS * (MOE_ROWS - 1)) // MOE_ROWS)
    blk_exp = jnp.minimum(jnp.searchsorted(end_pad, jnp.arange(n_blocks, dtype=jnp.int32) * MOE_ROWS,
                                           side='right'), N_EXPERTS - 1).astype(jnp.int32)
    e_row = jnp.repeat(blk_exp, MOE_ROWS)
    j = jnp.arange(n_blocks * MOE_ROWS, dtype=jnp.int32) - start_pad[e_row]
    valid = j < counts[e_row]
    src = jnp.clip(start[e_row] + j, 0, n_assign - 1)
    row_assign = jnp.where(valid, order[src], -1).astype(jnp.int32)
    blk_nvalid = jnp.sum(valid.reshape(n_blocks, MOE_ROWS), axis=1).astype(jnp.int32)
    return row_assign, blk_exp, blk_nvalid


def _combine_kernel(x1_ref, y0_ref, y1_ref, gate_ref, g2_ref, nw_ref, o_ref):
    gate = gate_ref[0]
    moe = gate[:, 0:1] * y0_ref[...] + gate[:, 1:2] * y1_ref[...]
    x2 = x1_ref[0] + g2_ref[0] * moe
    ms = jnp.mean(x2 * x2, axis=-1, keepdims=True)
    o_ref[0] = x2 * lax.rsqrt(ms + EPS) * nw_ref[...]


def _combine(x1, y_assign, gate, g2, final_norm):
    b, l, d = x1.shape
    tm = min(512, l)
    nt = l // tm
    row = lambda bi, i: (bi, i, 0)
    return pl.pallas_call(
        _combine_kernel,
        grid=(b, nt),
        in_specs=[pl.BlockSpec((1, tm, d), row),
                  pl.BlockSpec((tm, d), lambda bi, i: (bi * nt + i, 0)),
                  pl.BlockSpec((tm, d), lambda bi, i: (b * nt + bi * nt + i, 0)),
                  pl.BlockSpec((1, tm, 8), row),
                  pl.BlockSpec((1, 1, d), lambda bi, i: (bi, 0, 0)),
                  pl.BlockSpec((1, d), lambda bi, i: (0, 0))],
        out_specs=pl.BlockSpec((1, tm, d), row),
        out_shape=jax.ShapeDtypeStruct((b, l, d), F32),
        compiler_params=_params("parallel", "arbitrary"),
        name="moe_combine",
    )(x1, y_assign, y_assign, gate, g2, final_norm.reshape(1, d))


def _group_major(v):
    return v.reshape(2, N_GROUPS, HEADS_PER_GROUP).transpose(1, 0, 2).reshape(N_GROUPS, 1, 2 * HEADS_PER_GROUP)


def kernel(x, c, ctx, c_ctx, w_mod, b_mod, norm1, w_in, conv_w, conv_b, dt_bias, a_log, d_skip, ssd_norm,
           w_four, w_out, norm2, w_rg, b_rg, w_re, b_re, w_eg, w_eu, w_ed, final_norm):
    bsz, seq, d = x.shape
    n_tok = bsz * seq
    d_ssd = N_GROUPS * HEADS_PER_GROUP * HEADDIM
    conv_dim = d_ssd + 2 * N_GROUPS * D_STATE
    n_heads = N_GROUPS * HEADS_PER_GROUP
    layer = 0

    c_rows = jnp.zeros((16, d), F32).at[:bsz].set(c).at[bsz].set(c_ctx)
    mod = _modulation(c_rows, w_mod[layer], b_mod[layer])
    sh1, sc1, g1, sh2, sc2, g2 = [m[:bsz, None, :] for m in jnp.split(mod, 6, axis=-1)]
    sh1c, sc1c = [jnp.broadcast_to(m[bsz][None, None, :], (bsz, 1, d)) for m in jnp.split(mod, 6, axis=-1)[:2]]

    w = w_in[layer]
    wz = w[:, :d_ssd].astype(BF16)
    wx = w[:, d_ssd:d_ssd + conv_dim].astype(BF16)
    wdt = w[:, d_ssd + conv_dim:d_ssd + conv_dim + 2 * n_heads]
    wdt = wdt.reshape(d, 2, N_GROUPS, HEADS_PER_GROUP).transpose(0, 2, 1, 3).reshape(d, 2 * n_heads)
    wdt = jnp.pad(wdt, ((0, 0), (0, 128 - 2 * n_heads))).astype(BF16)
    wf = w[:, d_ssd + conv_dim + 2 * n_heads:].astype(BF16)
    n1 = norm1[layer].reshape(1, d)

    dtb = _group_major(dt_bias[layer])
    alog = _group_major(a_log[layer])
    dsk = jnp.repeat(d_skip[layer], HEADDIM).reshape(1, d_ssd)
    cw = conv_w[layer]
    cb = conv_b[layer].reshape(1, conv_dim)

    xbc_c, dt_c = _in_projection(ctx, sh1c, sc1c, n1, wx, wdt)
    h_zero = jnp.zeros((bsz, N_GROUPS, HEADS_PER_GROUP, D_STATE, HEADDIM), F32)
    z_dummy = jnp.zeros((bsz, 8, d_ssd), BF16)
    _, hf_c, hb_c = _ssd_mixer(xbc_c, dt_c, z_dummy, cw, cb, dtb, alog, dsk, h_zero, h_zero, emit_y=False)

    xbc_l, dt_l, z_l, f_l = _in_projection(x, sh1, sc1, n1, wx, wdt, wz, wf)
    y_l, _, _ = _ssd_mixer(xbc_l, dt_l, z_l, cw, cb, dtb, alog, dsk, hf_c, hb_c, emit_y=True)
    four = _fourier_mixer(f_l, w_four[layer])

    wo = w_out[layer]
    wr = jnp.concatenate([w_rg[layer], w_re[layer].transpose(1, 0, 2).reshape(d, N_EXPERTS)], axis=1)
    wr = jnp.pad(wr, ((0, 0), (0, ROUTE_LANES - wr.shape[1])))
    br = jnp.pad(jnp.concatenate([b_rg[layer], b_re[layer].reshape(-1)]),
                 (0, ROUTE_LANES - N_EXPERT_GROUPS - N_EXPERTS)).reshape(1, ROUTE_LANES)
    x1, h, eid, gate = _outproj_router(
        y_l, four, x, g1, sh2, sc2, ssd_norm[layer].reshape(1, d_ssd), norm2[layer].reshape(1, d),
        wo[:d_ssd].astype(BF16), wo[d_ssd:].astype(BF16), wr, br)

    row_assign, blk_exp, blk_nvalid = _route_tables(eid.reshape(n_tok, 8), n_tok)
    y_assign = _experts(h.reshape(n_tok, d), row_assign, blk_exp, blk_nvalid,
                        w_eg[layer], w_eu[layer], w_ed[layer])
    return _combine(x1, y_assign, gate, g2, final_norm)
```

```python
import functools
import math

import jax
import jax.numpy as jnp
from jax import lax
from jax.experimental import pallas as pl
from jax.experimental.pallas import tpu as pltpu

F32 = jnp.float32
BF16 = jnp.bfloat16
HIGHEST = lax.Precision.HIGHEST

EPS = 1e-6
LOG2E = 1.4426950408889634
CHUNK = 128
N_GROUPS = 4
HEADS_PER_GROUP = 4
HEADDIM = 64
D_STATE = 128
CONV_W = 5
CONV_HALO = 16
N_FOURIER_GROUPS = 4
FOURIER_CH = 128
N_EXPERT_GROUPS = 4
EXPERTS_PER_GROUP = 8
N_EXPERTS = N_EXPERT_GROUPS * EXPERTS_PER_GROUP
ROUTE_LANES = 128
MOE_ROWS = 256
VMEM_LIMIT_BYTES = 56 * 1024 * 1024


def _params(*sem):
    return pltpu.CompilerParams(dimension_semantics=sem, vmem_limit_bytes=VMEM_LIMIT_BYTES)


def _silu(v):
    return v * jax.nn.sigmoid(v)


def _dot_split3(a, b):
    a_hi = a.astype(BF16)
    a_lo = (a - a_hi.astype(F32)).astype(BF16)
    b_hi = b.astype(BF16)
    b_lo = (b - b_hi.astype(F32)).astype(BF16)
    dot = functools.partial(jnp.dot, preferred_element_type=F32)
    return dot(a_hi, b_hi) + (dot(a_lo, b_hi) + dot(a_hi, b_lo))


def _mod_kernel(c_ref, w_ref, b_ref, o_ref):
    s = _silu(c_ref[...])
    o_ref[...] = jnp.dot(s, w_ref[...], preferred_element_type=F32, precision=HIGHEST) + b_ref[...]


def _modulation(c_rows, w_mod, b_mod):
    rows, d = c_rows.shape
    n = w_mod.shape[1]
    tn = 512
    return pl.pallas_call(
        _mod_kernel,
        grid=(n // tn,),
        in_specs=[pl.BlockSpec((rows, d), lambda j: (0, 0)),
                  pl.BlockSpec((d, tn), lambda j: (0, j)),
                  pl.BlockSpec((1, tn), lambda j: (0, j))],
        out_specs=pl.BlockSpec((rows, tn), lambda j: (0, j)),
        out_shape=jax.ShapeDtypeStruct((rows, n), F32),
        compiler_params=_params("arbitrary"),
        name="modulation",
    )(c_rows, w_mod, b_mod.reshape(1, n))


def _inproj_kernel(x_ref, sh_ref, sc_ref, nw_ref, wx_ref, wdt_ref, *rest, with_zf):
    if with_zf:
        wz_ref, wf_ref, xbc_ref, dt_ref, z_ref, f_ref = rest
    else:
        xbc_ref, dt_ref = rest
    x = x_ref[0]
    ms = jnp.mean(x * x, axis=-1, keepdims=True)
    xn = x * lax.rsqrt(ms + EPS) * nw_ref[...]
    xm = (xn * (1.0 + sc_ref[0]) + sh_ref[0]).astype(BF16)
    xbc_ref[0] = jnp.dot(xm, wx_ref[...], preferred_element_type=F32).astype(BF16)
    dt = jnp.dot(xm, wdt_ref[...], preferred_element_type=F32)
    for g in range(N_GROUPS):
        dt_ref[0, g] = dt[:, 8 * g:8 * g + 8]
    if with_zf:
        z_ref[0] = jnp.dot(xm, wz_ref[...], preferred_element_type=F32).astype(BF16)
        f_ref[0] = jnp.dot(xm, wf_ref[...], preferred_element_type=F32).astype(BF16)


def _in_projection(x, shift, scale, norm_w, wx, wdt, wz=None, wf=None):
    b, l, d = x.shape
    tm = min(512, l)
    with_zf = wz is not None
    row = lambda bi, i: (bi, i, 0)
    vec = lambda bi, i: (bi, 0, 0)
    const = lambda bi, i: (0, 0)
    in_specs = [pl.BlockSpec((1, tm, d), row),
                pl.BlockSpec((1, 1, d), vec),
                pl.BlockSpec((1, 1, d), vec),
                pl.BlockSpec((1, d), const),
                pl.BlockSpec(wx.shape, const),
                pl.BlockSpec(wdt.shape, const)]
    args = [x, shift, scale, norm_w, wx, wdt]
    out_specs = [pl.BlockSpec((1, tm, wx.shape[1]), row),
                 pl.BlockSpec((1, N_GROUPS, tm, 8), lambda bi, i: (bi, 0, i, 0))]
    out_shape = [jax.ShapeDtypeStruct((b, l, wx.shape[1]), BF16),
                 jax.ShapeDtypeStruct((b, N_GROUPS, l, 8), F32)]
    if with_zf:
        in_specs += [pl.BlockSpec(wz.shape, const), pl.BlockSpec(wf.shape, const)]
        args += [wz, wf]
        out_specs += [pl.BlockSpec((1, tm, wz.shape[1]), row), pl.BlockSpec((1, tm, wf.shape[1]), row)]
        out_shape += [jax.ShapeDtypeStruct((b, l, wz.shape[1]), BF16),
                      jax.ShapeDtypeStruct((b, l, wf.shape[1]), BF16)]
    return pl.pallas_call(
        functools.partial(_inproj_kernel, with_zf=with_zf),
        grid=(b, l // tm),
        in_specs=in_specs,
        out_specs=out_specs,
        out_shape=out_shape,
        compiler_params=_params("parallel", "arbitrary"),
        name="in_projection_zf" if with_zf else "in_projection",
    )(*args)


def _conv_silu(in_ref, w_ref, b_ref, out_ref, seq):
    ch = in_ref.shape[-1]
    rb = min(512, seq)
    w = w_ref[...]
    bias = b_ref[...]
    for i in range(seq // rb):
        lo, hi = i * rb - CONV_HALO, (i + 1) * rb + CONV_HALO
        parts = []
        if lo < 0:
            parts.append(jnp.zeros((CONV_HALO, ch), F32))
        parts.append(in_ref[0, max(lo, 0):min(hi, seq), :].astype(F32))
        if hi > seq:
            parts.append(jnp.zeros((CONV_HALO, ch), F32))
        win = jnp.concatenate(parts, axis=0) if len(parts) > 1 else parts[0]
        acc = jnp.broadcast_to(bias, (rb, ch))
        for k in range(CONV_W):
            off = CONV_HALO - CONV_W // 2 + k
            acc = acc + w[k:k + 1, :] * win[off:off + rb, :]
        out_ref[i * rb:(i + 1) * rb, :] = _silu(acc)


def _ssd_kernel(xs_ref, xb_ref, xc_ref, dt_ref, z_ref,
                cwx_ref, cwb_ref, cwc_ref, cbx_ref, cbb_ref, cbc_ref,
                dtb_ref, alog_ref, dsk_ref, h0f_ref, h0b_ref,
                y_ref, hf_ref, hb_ref,
                xs_s, b_s, c_s, dt_s, da_s, cs_s, cst_s, bt_s, cb_s, yf_s, hst_s, *, seq, emit_y):
    nc = seq // CHUNK
    _conv_silu(xs_ref, cwx_ref, cbx_ref, xs_s, seq)
    _conv_silu(xb_ref, cwb_ref, cbb_ref, b_s, seq)
    _conv_silu(xc_ref, cwc_ref, cbc_ref, c_s, seq)

    dtr = dt_ref[0, 0] + dtb_ref[0]
    dtv = jnp.maximum(dtr, 0.0) + jnp.log(1.0 + jnp.exp(-jnp.abs(dtr)))
    dt_s[...] = dtv
    da_s[...] = dtv * (-jnp.exp(alog_ref[0]))

    ri = lax.broadcasted_iota(jnp.int32, (CHUNK, CHUNK), 0)
    ci = lax.broadcasted_iota(jnp.int32, (CHUNK, CHUNK), 1)
    tri_fwd = ci <= ri
    tri_bwd = ci >= ri
    is_fwd_lane8 = lax.broadcasted_iota(jnp.int32, (CHUNK, 8), 1) < HEADS_PER_GROUP
    lane_pad = jnp.zeros((CHUNK, CHUNK - 8), F32)

    tri16 = tri_fwd.astype(BF16)

    def tables(c, carry):
        r0 = pl.multiple_of(c * CHUNK, CHUNK)
        rows = pl.ds(r0, CHUNK)
        da = da_s[rows, :] * LOG2E
        p0 = da.astype(BF16)
        r1 = da - p0.astype(F32)
        p1 = r1.astype(BF16)
        p2 = (r1 - p1.astype(F32)).astype(BF16)
        packed = jnp.concatenate([p0, p1, p2, jnp.zeros((CHUNK, CHUNK - 24), BF16)], axis=1)
        acc = jnp.dot(tri16, packed, preferred_element_type=F32)
        cs_f = acc[:, 0:8] + acc[:, 8:16] + acc[:, 16:24]
        cs_b = cs_f[CHUNK - 1:CHUNK, :] - cs_f + da
        cs = jnp.where(is_fwd_lane8, cs_f, cs_b)
        cs_s[rows, :] = cs
        cs_t = jnp.concatenate([cs, lane_pad], axis=1).T[:8, :]
        dt_t = jnp.concatenate([dt_s[rows, :], lane_pad], axis=1).T[:8, :]
        cst_s[c] = cs_t - jnp.log2(dt_t)
        bt = b_s[rows, :].T
        bt_s[c] = bt
        cb_s[c] = jnp.dot(c_s[rows, :].astype(BF16), bt.astype(BF16), preferred_element_type=F32)
        return carry

    lax.fori_loop(0, nc, tables, 0, unroll=2)

    for j in range(HEADS_PER_GROUP):
        hst_s[0, j] = h0f_ref[0, 0, j]
        hst_s[1, j] = h0b_ref[0, 0, j]

    def chunk_dir(d, c, final):
        tri = tri_fwd if d == 0 else tri_bwd
        tot_row = CHUNK - 1 if d == 0 else 0
        r0 = pl.multiple_of(c * CHUNK, CHUNK)
        rows = pl.ds(r0, CHUNK)
        xs = xs_s[rows, :]
        cc = c_s[rows, :]
        cs = cs_s[rows, :]
        cs_t = cst_s[c]
        bt = bt_s[c]
        cb = cb_s[c]
        ys = []
        for j in range(HEADS_PER_GROUP):
            lane = HEADS_PER_GROUP * d + j
            a_col1 = cs[:, lane:lane + 1]
            a_col = jnp.broadcast_to(a_col1, (CHUNK, CHUNK))
            a_tot = a_col1[tot_row:tot_row + 1, :]
            a_row = cs_t[lane:lane + 1, :]
            g = (cb * jnp.where(tri, jnp.exp2(a_col - a_row), 0.0)).astype(BF16)
            cea = (cc * jnp.exp2(a_col)).astype(BF16)
            xh = xs[:, HEADDIM * j:HEADDIM * (j + 1)].astype(BF16)
            h = hst_s[d, j]
            y_h = (jnp.dot(g, xh, preferred_element_type=F32)
                   + jnp.dot(cea, h.astype(BF16), preferred_element_type=F32))
            w_row = jnp.exp2(a_tot - a_row)
            s_new = jnp.dot((bt * w_row).astype(BF16), xh, preferred_element_type=F32)
            hst_s[d, j] = h * jnp.exp2(a_tot) + s_new
            ys.append(y_h)
        if not emit_y:
            return
        y_c = jnp.concatenate(ys, axis=1)
        if d == 0:
            y_c = y_c + dsk_ref[...] * xs
        if final:
            zc = z_ref[0, rows, :].astype(F32)
            y_ref[0, rows, :] = ((yf_s[rows, :] + y_c) * _silu(zc)).astype(BF16)
        else:
            yf_s[rows, :] = y_c

    def first_half(t, carry):
        chunk_dir(0, t, False)
        chunk_dir(1, nc - 1 - t, False)
        return carry

    def second_half(t, carry):
        chunk_dir(0, t, True)
        chunk_dir(1, nc - 1 - t, True)
        return carry

    lax.fori_loop(0, nc // 2, first_half, 0)
    lax.fori_loop(nc // 2, nc, second_half, 0)
    for j in range(HEADS_PER_GROUP):
        hf_ref[0, 0, j] = hst_s[0, j]
        hb_ref[0, 0, j] = hst_s[1, j]
    if not emit_y:
        y_ref[...] = jnp.zeros(y_ref.shape, y_ref.dtype)


def _ssd_mixer(xbc, dt, z, conv_w, conv_b, dtb, alog, dskip, h0f, h0b, emit_y):
    b, l, _ = xbc.shape
    gw = HEADS_PER_GROUP * HEADDIM
    nxb = (N_GROUPS * gw) // D_STATE
    y_rows = l if emit_y else 8
    st_shape = (b, N_GROUPS, HEADS_PER_GROUP, D_STATE, HEADDIM)
    st_spec = pl.BlockSpec((1, 1, HEADS_PER_GROUP, D_STATE, HEADDIM), lambda bi, g: (bi, g, 0, 0, 0))
    in_specs = [
        pl.BlockSpec((1, l, gw), lambda bi, g: (bi, 0, g)),
        pl.BlockSpec((1, l, D_STATE), lambda bi, g: (bi, 0, nxb + g)),
        pl.BlockSpec((1, l, D_STATE), lambda bi, g: (bi, 0, nxb + N_GROUPS + g)),
        pl.BlockSpec((1, 1, l, 8), lambda bi, g: (bi, g, 0, 0)),
        pl.BlockSpec((1, y_rows, gw), lambda bi, g: (bi, 0, g)),
        pl.BlockSpec((CONV_W, gw), lambda bi, g: (0, g)),
        pl.BlockSpec((CONV_W, D_STATE), lambda bi, g: (0, nxb + g)),
        pl.BlockSpec((CONV_W, D_STATE), lambda bi, g: (0, nxb + N_GROUPS + g)),
        pl.BlockSpec((1, gw), lambda bi, g: (0, g)),
        pl.BlockSpec((1, D_STATE), lambda bi, g: (0, nxb + g)),
        pl.BlockSpec((1, D_STATE), lambda bi, g: (0, nxb + N_GROUPS + g)),
        pl.BlockSpec((1, 1, 8), lambda bi, g: (g, 0, 0)),
        pl.BlockSpec((1, 1, 8), lambda bi, g: (g, 0, 0)),
        pl.BlockSpec((1, gw), lambda bi, g: (0, g)),
        st_spec, st_spec,
    ]
    out_specs = [pl.BlockSpec((1, y_rows, gw), lambda bi, g: (bi, 0, g)), st_spec, st_spec]
    out_shape = [jax.ShapeDtypeStruct((b, y_rows, N_GROUPS * gw), BF16),
                 jax.ShapeDtypeStruct(st_shape, F32), jax.ShapeDtypeStruct(st_shape, F32)]
    nc = l // CHUNK
    scratch = [pltpu.VMEM((l, gw), F32), pltpu.VMEM((l, D_STATE), F32), pltpu.VMEM((l, D_STATE), F32),
               pltpu.VMEM((l, 8), F32), pltpu.VMEM((l, 8), F32), pltpu.VMEM((l, 8), F32),
               pltpu.VMEM((nc, 8, CHUNK), F32),
               pltpu.VMEM((nc, D_STATE, CHUNK), F32), pltpu.VMEM((nc, CHUNK, CHUNK), F32),
               pltpu.VMEM((l if emit_y else 8, gw), F32),
               pltpu.VMEM((2, HEADS_PER_GROUP, D_STATE, HEADDIM), F32)]
    return pl.pallas_call(
        functools.partial(_ssd_kernel, seq=l, emit_y=emit_y),
        grid=(b, N_GROUPS),
        in_specs=in_specs, out_specs=out_specs, out_shape=out_shape,
        scratch_shapes=scratch,
        compiler_params=_params("parallel", "arbitrary"),
        name="ssd_mixer" if emit_y else "ssd_mixer_ctx",
    )(xbc, xbc, xbc, dt, z, conv_w, conv_w, conv_w, conv_b, conv_b, conv_b, dtb, alog, dskip, h0f, h0b)


def _fourier_kernel(f_ref, cc_ref, sc_ref, w_ref, cl_ref, sl_ref, o_ref, u_s, v_s, *, seq):
    @pl.when(pl.program_id(1) == 0)
    def _():
        scale = 1.0 / math.sqrt(seq * FOURIER_CH)
        for g in range(N_FOURIER_GROUPS):
            w = w_ref[g]
            a = jnp.dot(cc_ref[...], w, preferred_element_type=F32, precision=HIGHEST) * scale
            bm = jnp.dot(sc_ref[...], w, preferred_element_type=F32, precision=HIGHEST) * scale
            cols = slice(FOURIER_CH * g, FOURIER_CH * (g + 1))
            fg = f_ref[0, :, cols]
            u_s[:, cols] = jnp.dot(fg, a.astype(BF16), preferred_element_type=F32).astype(BF16)
            v_s[:, cols] = jnp.dot(fg, bm.astype(BF16), preferred_element_type=F32).astype(BF16)

    out = (jnp.dot(cl_ref[...], u_s[...], preferred_element_type=F32)
           - jnp.dot(sl_ref[...], v_s[...], preferred_element_type=F32))
    o_ref[0] = out.astype(BF16)


def _dft_tables(n, dtype):
    k = lax.broadcasted_iota(jnp.int32, (n, n), 0)
    l = lax.broadcasted_iota(jnp.int32, (n, n), 1)
    ang = ((k * l) % n).astype(F32) * (2.0 * math.pi / n)
    return jnp.cos(ang).astype(dtype), jnp.sin(ang).astype(dtype)


def _fourier_mixer(f, w_four):
    b, l, df = f.shape
    tr = min(512, l)
    cc, sc = _dft_tables(FOURIER_CH, F32)
    cl, sl = _dft_tables(l, BF16)
    return pl.pallas_call(
        functools.partial(_fourier_kernel, seq=l),
        grid=(b, l // tr),
        in_specs=[pl.BlockSpec((1, l, df), lambda bi, i: (bi, 0, 0)),
                  pl.BlockSpec((FOURIER_CH, FOURIER_CH), lambda bi, i: (0, 0)),
                  pl.BlockSpec((FOURIER_CH, FOURIER_CH), lambda bi, i: (0, 0)),
                  pl.BlockSpec(w_four.shape, lambda bi, i: (0, 0, 0)),
                  pl.BlockSpec((tr, l), lambda bi, i: (i, 0)),
                  pl.BlockSpec((tr, l), lambda bi, i: (i, 0))],
        out_specs=pl.BlockSpec((1, tr, df), lambda bi, i: (bi, i, 0)),
        out_shape=jax.ShapeDtypeStruct((b, l, df), BF16),
        scratch_shapes=[pltpu.VMEM((l, df), BF16), pltpu.VMEM((l, df), BF16)],
        compiler_params=_params("parallel", "arbitrary"),
        name="fourier_mixer",
    )(f, cc, sc, w_four, cl, sl)


def _outproj_router_kernel(y_ref, four_ref, x_ref, g1_ref, sh_ref, sc_ref, nssd_ref, n2_ref,
                           wos_ref, wof_ref, wr_ref, br_ref,
                           x1_ref, h_ref, eid_ref, gate_ref):
    y = y_ref[0].astype(F32)
    ms = jnp.mean(y * y, axis=-1, keepdims=True)
    yn = (y * lax.rsqrt(ms + EPS) * nssd_ref[...]).astype(BF16)
    mix = (jnp.dot(yn, wos_ref[...], preferred_element_type=F32)
           + jnp.dot(four_ref[0], wof_ref[...], preferred_element_type=F32))
    x1 = x_ref[0] + g1_ref[0] * mix
    x1_ref[0] = x1
    ms2 = jnp.mean(x1 * x1, axis=-1, keepdims=True)
    h = (x1 * lax.rsqrt(ms2 + EPS) * n2_ref[...]) * (1.0 + sc_ref[0]) + sh_ref[0]
    h_ref[0] = h

    lg = _dot_split3(h, wr_ref[...]) + br_ref[...]
    tm = lg.shape[0]
    lane = lax.broadcasted_iota(jnp.int32, (tm, ROUTE_LANES), 1)
    lane_f = lane.astype(F32)
    neg = jnp.float32(-1e30)
    big = jnp.float32(1e9)
    is_grp = lane < N_EXPERT_GROUPS
    gl = jnp.where(is_grp, lg, neg)
    gmax = jnp.max(gl, axis=-1, keepdims=True)
    gsum = jnp.sum(jnp.where(is_grp, jnp.exp(gl - gmax), 0.0), axis=-1, keepdims=True)
    grp = jnp.min(jnp.where(gl == gmax, lane_f, big), axis=-1, keepdims=True)
    p_grp = 1.0 / gsum
    lo = N_EXPERT_GROUPS + EXPERTS_PER_GROUP * grp
    in_grp = jnp.logical_and(lane_f >= lo, lane_f < lo + EXPERTS_PER_GROUP)
    el = jnp.where(in_grp, lg, neg)
    m1 = jnp.max(el, axis=-1, keepdims=True)
    i1 = jnp.min(jnp.where(el == m1, lane_f, big), axis=-1, keepdims=True)
    el2 = jnp.where(lane_f == i1, neg, el)
    m2 = jnp.max(el2, axis=-1, keepdims=True)
    i2 = jnp.min(jnp.where(el2 == m2, lane_f, big), axis=-1, keepdims=True)
    e2 = jnp.exp(m2 - m1)
    den = 1.0 + e2
    gate1 = p_grp / den
    gate2 = p_grp * e2 / den
    lane8 = lax.broadcasted_iota(jnp.int32, (tm, 8), 1)
    eid = jnp.where(lane8 == 0, i1 - N_EXPERT_GROUPS, jnp.where(lane8 == 1, i2 - N_EXPERT_GROUPS, 0.0))
    eid_ref[0] = eid.astype(jnp.int32)
    gate_ref[0] = jnp.where(lane8 == 0, gate1, jnp.where(lane8 == 1, gate2, 0.0))


def _outproj_router(y, four, x, g1, sh2, sc2, nssd, n2, wos, wof, wr, br):
    b, l, d = x.shape
    tm = min(512, l)
    row = lambda bi, i: (bi, i, 0)
    vec = lambda bi, i: (bi, 0, 0)
    const = lambda bi, i: (0, 0)
    return pl.pallas_call(
        _outproj_router_kernel,
        grid=(b, l // tm),
        in_specs=[pl.BlockSpec((1, tm, y.shape[2]), row),
                  pl.BlockSpec((1, tm, four.shape[2]), row),
                  pl.BlockSpec((1, tm, d), row),
                  pl.BlockSpec((1, 1, d), vec), pl.BlockSpec((1, 1, d), vec), pl.BlockSpec((1, 1, d), vec),
                  pl.BlockSpec((1, y.shape[2]), const), pl.BlockSpec((1, d), const),
                  pl.BlockSpec(wos.shape, const), pl.BlockSpec(wof.shape, const),
                  pl.BlockSpec(wr.shape, const), pl.BlockSpec(br.shape, const)],
        out_specs=[pl.BlockSpec((1, tm, d), row), pl.BlockSpec((1, tm, d), row),
                   pl.BlockSpec((1, tm, 8), row), pl.BlockSpec((1, tm, 8), row)],
        out_shape=[jax.ShapeDtypeStruct((b, l, d), F32), jax.ShapeDtypeStruct((b, l, d), F32),
                   jax.ShapeDtypeStruct((b, l, 8), jnp.int32), jax.ShapeDtypeStruct((b, l, 8), F32)],
        compiler_params=_params("parallel", "arbitrary"),
        name="outproj_router",
    )(y, four, x, g1, sh2, sc2, nssd, n2, wos, wof, wr, br)


def _expert_kernel(bexp_ref, tokc_ref, tokn_ref, dstp_ref, dstc_ref, h_hbm, wg_ref, wu_ref, wd_ref, y_hbm,
                   hbuf, ybuf, act_s, wg_s, wu_s, wd_s, sem_in, sem_out):
    i = pl.program_id(0)
    last = pl.num_programs(0) - 1
    slot = i % 2
    other = 1 - slot
    de = wg_s.shape[1]
    d = wd_s.shape[1]
    n_phase = 8
    rows_per_phase = MOE_ROWS // n_phase

    def start_gather(tok_ref, s, r):
        pltpu.make_async_copy(h_hbm.at[pl.ds(tok_ref[0, 0, r], 1), :],
                              hbuf.at[s, pl.ds(r, 1), :], sem_in.at[s]).start()

    def start_scatter(dst_ref, s, r):
        pltpu.make_async_copy(ybuf.at[s, pl.ds(r, 1), :],
                              y_hbm.at[pl.ds(dst_ref[0, 0, r], 1), :], sem_out.at[s]).start()

    def wait_gather(s):
        pltpu.make_async_copy(h_hbm.at[pl.ds(0, MOE_ROWS), :], hbuf.at[s], sem_in.at[s]).wait()

    def wait_scatter(s):
        pltpu.make_async_copy(ybuf.at[s], y_hbm.at[pl.ds(0, MOE_ROWS), :], sem_out.at[s]).wait()

    @pl.when(i == 0)
    def _():
        for r in range(MOE_ROWS):
            start_gather(tokc_ref, 0, r)

    prev = bexp_ref[jnp.maximum(i - 1, 0)]

    @pl.when(jnp.logical_or(i == 0, bexp_ref[i] != prev))
    def _():
        wg_s[...] = wg_ref[0].astype(BF16)
        wu_s[...] = wu_ref[0].astype(BF16)
        wd_s[...] = wd_ref[0].astype(BF16)

    wait_gather(slot)

    @pl.when(i >= 2)
    def _():
        wait_scatter(slot)

    def step(with_scatter, cur):
        nxt = 1 - cur
        xb = hbuf[cur].astype(BF16)
        for p in range(n_phase):
            for r in range(p * rows_per_phase, (p + 1) * rows_per_phase):
                start_gather(tokn_ref, nxt, r)
                if with_scatter:
                    start_scatter(dstp_ref, nxt, r)
            if p < n_phase // 2:
                w = de // (n_phase // 2)
                cols = slice(p * w, (p + 1) * w)
                gact = jnp.dot(xb, wg_s[:, cols], preferred_element_type=F32)
                up = jnp.dot(xb, wu_s[:, cols], preferred_element_type=F32)
                act_s[:, cols] = (_silu(gact) * up).astype(BF16)
            else:
                w = d // (n_phase // 2)
                cols = slice((p - n_phase // 2) * w, (p - n_phase // 2 + 1) * w)
                ybuf[cur, :, cols] = jnp.dot(act_s[...], wd_s[:, cols], preferred_element_type=F32)

    @pl.when(i == 0)
    def _():
        step(False, 0)

    @pl.when(jnp.logical_and(i > 0, slot == 0))
    def _():
        step(True, 0)

    @pl.when(slot == 1)
    def _():
        step(True, 1)

    @pl.when(i == last)
    def _():
        for r in range(MOE_ROWS):
            start_scatter(dstc_ref, slot, r)
        wait_gather(other)
        wait_scatter(other)
        wait_scatter(slot)


def _experts(h2d, row_tok, row_dst, blk_exp, w_eg, w_eu, w_ed):
    n_tok, d = h2d.shape
    n_blocks = blk_exp.shape[0]
    assert n_blocks >= 2
    de = w_eg.shape[2]
    idx_shape = (n_blocks, 1, MOE_ROWS)
    idx_block = (1, 1, MOE_ROWS)
    smem = pltpu.SMEM
    grid_spec = pltpu.PrefetchScalarGridSpec(
        num_scalar_prefetch=1,
        grid=(n_blocks,),
        in_specs=[pl.BlockSpec(idx_block, lambda i, be: (i, 0, 0), memory_space=smem),
                  pl.BlockSpec(idx_block, lambda i, be: (jnp.minimum(i + 1, n_blocks - 1), 0, 0), memory_space=smem),
                  pl.BlockSpec(idx_block, lambda i, be: (jnp.maximum(i - 1, 0), 0, 0), memory_space=smem),
                  pl.BlockSpec(idx_block, lambda i, be: (i, 0, 0), memory_space=smem),
                  pl.BlockSpec(memory_space=pl.ANY),
                  pl.BlockSpec((1, d, de), lambda i, be: (be[i], 0, 0)),
                  pl.BlockSpec((1, d, de), lambda i, be: (be[i], 0, 0)),
                  pl.BlockSpec((1, de, d), lambda i, be: (be[i], 0, 0))],
        out_specs=pl.BlockSpec(memory_space=pl.ANY),
        scratch_shapes=[pltpu.VMEM((2, MOE_ROWS, d), F32), pltpu.VMEM((2, MOE_ROWS, d), F32),
                        pltpu.VMEM((MOE_ROWS, de), BF16),
                        pltpu.VMEM((d, de), BF16), pltpu.VMEM((d, de), BF16), pltpu.VMEM((de, d), BF16),
                        pltpu.SemaphoreType.DMA((2,)), pltpu.SemaphoreType.DMA((2,))],
    )
    row_tok = row_tok.reshape(idx_shape)
    row_dst = row_dst.reshape(idx_shape)
    return pl.pallas_call(
        _expert_kernel,
        grid_spec=grid_spec,
        out_shape=jax.ShapeDtypeStruct((n_blocks * MOE_ROWS, d), F32),
        compiler_params=_params("arbitrary"),
        name="moe_experts",
    )(blk_exp, row_tok, row_tok, row_dst, row_dst, h2d, w_eg, w_eu, w_ed)


def _route_tables(eid, n_tok):
    n_assign = 2 * n_tok
    e_flat = jnp.concatenate([eid[:, 0], eid[:, 1]])
    order = jnp.argsort(e_flat).astype(jnp.int32)
    experts = jnp.arange(N_EXPERTS, dtype=jnp.int32)
    counts = jnp.sum((e_flat[:, None] == experts[None, :]).astype(jnp.int32), axis=0)
    start = jnp.cumsum(counts) - counts
    padded = (counts + MOE_ROWS - 1) // MOE_ROWS * MOE_ROWS
    end_pad = jnp.cumsum(padded)
    start_pad = end_pad - padded
    n_blocks = -(-(n_assign + N_EXPERTS * (MOE_ROWS - 1)) // MOE_ROWS)
    blk_row0 = jnp.arange(n_blocks, dtype=jnp.int32) * MOE_ROWS
    blk_exp = jnp.minimum(jnp.sum((end_pad[None, :] <= blk_row0[:, None]).astype(jnp.int32), axis=1),
                          N_EXPERTS - 1).astype(jnp.int32)
    j = (blk_row0 - start_pad[blk_exp])[:, None] + jnp.arange(MOE_ROWS, dtype=jnp.int32)[None, :]
    valid = j < counts[blk_exp][:, None]
    src = jnp.clip(start[blk_exp][:, None] + j, 0, n_assign - 1)
    assign = order[src.reshape(-1)].reshape(src.shape)
    row_tok = jnp.where(valid, jnp.where(assign >= n_tok, assign - n_tok, assign), 0)
    spare = n_assign + jnp.cumsum((~valid).reshape(-1).astype(jnp.int32)).reshape(valid.shape) - 1
    row_dst = jnp.where(valid, assign, spare)
    return row_tok.astype(jnp.int32), row_dst.astype(jnp.int32), blk_exp


def _combine_kernel(x1_ref, y0_ref, y1_ref, gate_ref, g2_ref, nw_ref, o_ref):
    gate = gate_ref[0]
    moe = gate[:, 0:1] * y0_ref[...] + gate[:, 1:2] * y1_ref[...]
    x2 = x1_ref[0] + g2_ref[0] * moe
    ms = jnp.mean(x2 * x2, axis=-1, keepdims=True)
    o_ref[0] = x2 * lax.rsqrt(ms + EPS) * nw_ref[...]


def _combine(x1, y_assign, gate, g2, final_norm):
    b, l, d = x1.shape
    tm = min(512, l)
    nt = l // tm
    row = lambda bi, i: (bi, i, 0)
    return pl.pallas_call(
        _combine_kernel,
        grid=(b, nt),
        in_specs=[pl.BlockSpec((1, tm, d), row),
                  pl.BlockSpec((tm, d), lambda bi, i: (bi * nt + i, 0)),
                  pl.BlockSpec((tm, d), lambda bi, i: (b * nt + bi * nt + i, 0)),
                  pl.BlockSpec((1, tm, 8), row),
                  pl.BlockSpec((1, 1, d), lambda bi, i: (bi, 0, 0)),
                  pl.BlockSpec((1, d), lambda bi, i: (0, 0))],
        out_specs=pl.BlockSpec((1, tm, d), row),
        out_shape=jax.ShapeDtypeStruct((b, l, d), F32),
        compiler_params=_params("parallel", "arbitrary"),
        name="moe_combine",
    )(x1, y_assign, y_assign, gate, g2, final_norm.reshape(1, d))


def _group_major(v):
    return v.reshape(2, N_GROUPS, HEADS_PER_GROUP).transpose(1, 0, 2).reshape(N_GROUPS, 1, 2 * HEADS_PER_GROUP)


def kernel(x, c, ctx, c_ctx, w_mod, b_mod, norm1, w_in, conv_w, conv_b, dt_bias, a_log, d_skip, ssd_norm,
           w_four, w_out, norm2, w_rg, b_rg, w_re, b_re, w_eg, w_eu, w_ed, final_norm):
    bsz, seq, d = x.shape
    n_tok = bsz * seq
    d_ssd = N_GROUPS * HEADS_PER_GROUP * HEADDIM
    conv_dim = d_ssd + 2 * N_GROUPS * D_STATE
    n_heads = N_GROUPS * HEADS_PER_GROUP
    layer = 0

    c_rows = jnp.zeros((16, d), F32).at[:bsz].set(c).at[bsz].set(c_ctx)
    mod = _modulation(c_rows, w_mod[layer], b_mod[layer])
    sh1, sc1, g1, sh2, sc2, g2 = [m[:bsz, None, :] for m in jnp.split(mod, 6, axis=-1)]
    sh1c, sc1c = [jnp.broadcast_to(m[bsz][None, None, :], (bsz, 1, d)) for m in jnp.split(mod, 6, axis=-1)[:2]]

    w = w_in[layer]
    wz = w[:, :d_ssd].astype(BF16)
    wx = w[:, d_ssd:d_ssd + conv_dim].astype(BF16)
    wdt = w[:, d_ssd + conv_dim:d_ssd + conv_dim + 2 * n_heads]
    wdt = wdt.reshape(d, 2, N_GROUPS, HEADS_PER_GROUP).transpose(0, 2, 1, 3).reshape(d, 2 * n_heads)
    wdt = jnp.pad(wdt, ((0, 0), (0, 128 - 2 * n_heads))).astype(BF16)
    wf = w[:, d_ssd + conv_dim + 2 * n_heads:].astype(BF16)
    n1 = norm1[layer].reshape(1, d)

    dtb = _group_major(dt_bias[layer])
    alog = _group_major(a_log[layer])
    dsk = jnp.repeat(d_skip[layer], HEADDIM).reshape(1, d_ssd)
    cw = conv_w[layer]
    cb = conv_b[layer].reshape(1, conv_dim)

    xbc_c, dt_c = _in_projection(ctx, sh1c, sc1c, n1, wx, wdt)
    h_zero = jnp.zeros((bsz, N_GROUPS, HEADS_PER_GROUP, D_STATE, HEADDIM), F32)
    z_dummy = jnp.zeros((bsz, 8, d_ssd), BF16)
    _, hf_c, hb_c = _ssd_mixer(xbc_c, dt_c, z_dummy, cw, cb, dtb, alog, dsk, h_zero, h_zero, emit_y=False)

    xbc_l, dt_l, z_l, f_l = _in_projection(x, sh1, sc1, n1, wx, wdt, wz, wf)
    y_l, _, _ = _ssd_mixer(xbc_l, dt_l, z_l, cw, cb, dtb, alog, dsk, hf_c, hb_c, emit_y=True)
    four = _fourier_mixer(f_l, w_four[layer])

    wo = w_out[layer]
    wr = jnp.concatenate([w_rg[layer], w_re[layer].transpose(1, 0, 2).reshape(d, N_EXPERTS)], axis=1)
    wr = jnp.pad(wr, ((0, 0), (0, ROUTE_LANES - wr.shape[1])))
    br = jnp.pad(jnp.concatenate([b_rg[layer], b_re[layer].reshape(-1)]),
                 (0, ROUTE_LANES - N_EXPERT_GROUPS - N_EXPERTS)).reshape(1, ROUTE_LANES)
    x1, h, eid, gate = _outproj_router(
        y_l, four, x, g1, sh2, sc2, ssd_norm[layer].reshape(1, d_ssd), norm2[layer].reshape(1, d),
        wo[:d_ssd].astype(BF16), wo[d_ssd:].astype(BF16), wr, br)

    row_tok, row_dst, blk_exp = _route_tables(eid.reshape(n_tok, 8), n_tok)
    y_assign = _experts(h.reshape(n_tok, d), row_tok, row_dst, blk_exp, w_eg[layer], w_eu[layer], w_ed[layer])
    return _combine(x1, y_assign, gate, g2, final_norm)
```

```python
import functools
import math

import jax
import jax.numpy as jnp
from jax import lax
from jax.experimental import pallas as pl
from jax.experimental.pallas import tpu as pltpu

F32 = jnp.float32
BF16 = jnp.bfloat16
HIGHEST = lax.Precision.HIGHEST

EPS = 1e-6
LOG2E = 1.4426950408889634
CHUNK = 128
N_GROUPS = 4
HEADS_PER_GROUP = 4
HEADDIM = 64
D_STATE = 128
CONV_W = 5
CONV_HALO = 16
N_FOURIER_GROUPS = 4
DFT_SPLIT = 64
FOURIER_CH = 128
N_EXPERT_GROUPS = 4
EXPERTS_PER_GROUP = 8
N_EXPERTS = N_EXPERT_GROUPS * EXPERTS_PER_GROUP
ROUTE_LANES = 128
MOE_ROWS = 256
VMEM_LIMIT_BYTES = 56 * 1024 * 1024


def _params(*sem):
    return pltpu.CompilerParams(dimension_semantics=sem, vmem_limit_bytes=VMEM_LIMIT_BYTES)


def _silu(v):
    return v * jax.nn.sigmoid(v)


LANES = 128
TILE_ROWS = 8


def _store_token_tiles(ref, lead, val):
    rows = val.shape[0]
    for j in range(TILE_ROWS):
        ref[lead + (pl.ds(j, rows, stride=TILE_ROWS), slice(None))] = val[:, LANES * j:LANES * (j + 1)]


def _load_token_tiles(ref, lead, rows):
    return jnp.concatenate(
        [ref[lead + (pl.ds(j, rows, stride=TILE_ROWS), slice(None))] for j in range(TILE_ROWS)], axis=1)


def _dot_split3(a, b):
    a_hi = a.astype(BF16)
    a_lo = (a - a_hi.astype(F32)).astype(BF16)
    b_hi = b.astype(BF16)
    b_lo = (b - b_hi.astype(F32)).astype(BF16)
    dot = functools.partial(jnp.dot, preferred_element_type=F32)
    return dot(a_hi, b_hi) + (dot(a_lo, b_hi) + dot(a_hi, b_lo))


def _mod_kernel(c_ref, w_ref, b_ref, o_ref):
    s = _silu(c_ref[...])
    o_ref[...] = jnp.dot(s, w_ref[...], preferred_element_type=F32, precision=HIGHEST) + b_ref[...]


def _modulation(c_rows, w_mod, b_mod):
    rows, d = c_rows.shape
    n = w_mod.shape[1]
    tn = 512
    return pl.pallas_call(
        _mod_kernel,
        grid=(n // tn,),
        in_specs=[pl.BlockSpec((rows, d), lambda j: (0, 0)),
                  pl.BlockSpec((d, tn), lambda j: (0, j)),
                  pl.BlockSpec((1, tn), lambda j: (0, j))],
        out_specs=pl.BlockSpec((rows, tn), lambda j: (0, j)),
        out_shape=jax.ShapeDtypeStruct((rows, n), F32),
        compiler_params=_params("arbitrary"),
        name="modulation",
    )(c_rows, w_mod, b_mod.reshape(1, n))


def _inproj_kernel(x_ref, sh_ref, sc_ref, nw_ref, wx_ref, wdt_ref, *rest, with_zf):
    if with_zf:
        wz_ref, wf_ref, xbc_ref, dt_ref, z_ref, f_ref = rest
    else:
        xbc_ref, dt_ref = rest
    x = x_ref[0]
    ms = jnp.mean(x * x, axis=-1, keepdims=True)
    xn = x * lax.rsqrt(ms + EPS) * nw_ref[...]
    xm = (xn * (1.0 + sc_ref[0]) + sh_ref[0]).astype(BF16)
    xbc_ref[0] = jnp.dot(xm, wx_ref[...], preferred_element_type=F32).astype(BF16)
    dt = jnp.dot(xm, wdt_ref[...], preferred_element_type=F32)
    for g in range(N_GROUPS):
        dt_ref[0, g] = dt[:, 8 * g:8 * g + 8]
    if with_zf:
        z_ref[0] = jnp.dot(xm, wz_ref[...], preferred_element_type=F32).astype(BF16)
        f_ref[0] = jnp.dot(xm, wf_ref[...], preferred_element_type=F32).astype(BF16)


def _in_projection(x, shift, scale, norm_w, wx, wdt, wz=None, wf=None):
    b, l, d = x.shape
    tm = min(512, l)
    with_zf = wz is not None
    row = lambda bi, i: (bi, i, 0)
    vec = lambda bi, i: (bi, 0, 0)
    const = lambda bi, i: (0, 0)
    in_specs = [pl.BlockSpec((1, tm, d), row),
                pl.BlockSpec((1, 1, d), vec),
                pl.BlockSpec((1, 1, d), vec),
                pl.BlockSpec((1, d), const),
                pl.BlockSpec(wx.shape, const),
                pl.BlockSpec(wdt.shape, const)]
    args = [x, shift, scale, norm_w, wx, wdt]
    out_specs = [pl.BlockSpec((1, tm, wx.shape[1]), row),
                 pl.BlockSpec((1, N_GROUPS, tm, 8), lambda bi, i: (bi, 0, i, 0))]
    out_shape = [jax.ShapeDtypeStruct((b, l, wx.shape[1]), BF16),
                 jax.ShapeDtypeStruct((b, N_GROUPS, l, 8), F32)]
    if with_zf:
        in_specs += [pl.BlockSpec(wz.shape, const), pl.BlockSpec(wf.shape, const)]
        args += [wz, wf]
        out_specs += [pl.BlockSpec((1, tm, wz.shape[1]), row), pl.BlockSpec((1, tm, wf.shape[1]), row)]
        out_shape += [jax.ShapeDtypeStruct((b, l, wz.shape[1]), BF16),
                      jax.ShapeDtypeStruct((b, l, wf.shape[1]), BF16)]
    return pl.pallas_call(
        functools.partial(_inproj_kernel, with_zf=with_zf),
        grid=(b, l // tm),
        in_specs=in_specs,
        out_specs=out_specs,
        out_shape=out_shape,
        compiler_params=_params("parallel", "arbitrary"),
        name="in_projection_zf" if with_zf else "in_projection",
    )(*args)


def _conv_silu(in_ref, w_ref, b_ref, out_ref, seq):
    ch = in_ref.shape[-1]
    rb = min(512, seq)
    w = w_ref[...]
    bias = b_ref[...]
    for i in range(seq // rb):
        lo, hi = i * rb - CONV_HALO, (i + 1) * rb + CONV_HALO
        parts = []
        if lo < 0:
            parts.append(jnp.zeros((CONV_HALO, ch), F32))
        parts.append(in_ref[0, max(lo, 0):min(hi, seq), :].astype(F32))
        if hi > seq:
            parts.append(jnp.zeros((CONV_HALO, ch), F32))
        win = jnp.concatenate(parts, axis=0) if len(parts) > 1 else parts[0]
        acc = jnp.broadcast_to(bias, (rb, ch))
        for k in range(CONV_W):
            off = CONV_HALO - CONV_W // 2 + k
            acc = acc + w[k:k + 1, :] * win[off:off + rb, :]
        out_ref[i * rb:(i + 1) * rb, :] = _silu(acc)


def _ssd_kernel(xs_ref, xb_ref, xc_ref, dt_ref, z_ref,
                cwx_ref, cwb_ref, cwc_ref, cbx_ref, cbb_ref, cbc_ref,
                dtb_ref, alog_ref, dsk_ref, h0f_ref, h0b_ref,
                y_ref, hf_ref, hb_ref,
                xs_s, b_s, c_s, dt_s, da_s, cs_s, cst_s, bt_s, cb_s, yf_s, hst_s, *, seq, emit_y):
    nc = seq // CHUNK
    _conv_silu(xs_ref, cwx_ref, cbx_ref, xs_s, seq)
    _conv_silu(xb_ref, cwb_ref, cbb_ref, b_s, seq)
    _conv_silu(xc_ref, cwc_ref, cbc_ref, c_s, seq)

    dtr = dt_ref[0, 0] + dtb_ref[0]
    dtv = jnp.maximum(dtr, 0.0) + jnp.log(1.0 + jnp.exp(-jnp.abs(dtr)))
    dt_s[...] = dtv
    da_s[...] = dtv * (-jnp.exp(alog_ref[0]))

    ri = lax.broadcasted_iota(jnp.int32, (CHUNK, CHUNK), 0)
    ci = lax.broadcasted_iota(jnp.int32, (CHUNK, CHUNK), 1)
    tri_fwd = ci <= ri
    tri_bwd = ci >= ri
    is_fwd_lane8 = lax.broadcasted_iota(jnp.int32, (CHUNK, 8), 1) < HEADS_PER_GROUP
    lane_pad = jnp.zeros((CHUNK, CHUNK - 8), F32)

    tri16 = tri_fwd.astype(BF16)

    def tables(c, carry):
        r0 = pl.multiple_of(c * CHUNK, CHUNK)
        rows = pl.ds(r0, CHUNK)
        da = da_s[rows, :] * LOG2E
        p0 = da.astype(BF16)
        r1 = da - p0.astype(F32)
        p1 = r1.astype(BF16)
        p2 = (r1 - p1.astype(F32)).astype(BF16)
        packed = jnp.concatenate([p0, p1, p2, jnp.zeros((CHUNK, CHUNK - 24), BF16)], axis=1)
        acc = jnp.dot(tri16, packed, preferred_element_type=F32)
        cs_f = acc[:, 0:8] + acc[:, 8:16] + acc[:, 16:24]
        cs_b = cs_f[CHUNK - 1:CHUNK, :] - cs_f + da
        cs = jnp.where(is_fwd_lane8, cs_f, cs_b)
        cs_s[rows, :] = cs
        cs_t = jnp.concatenate([cs, lane_pad], axis=1).T[:8, :]
        dt_t = jnp.concatenate([dt_s[rows, :], lane_pad], axis=1).T[:8, :]
        cst_s[c] = cs_t - jnp.log2(dt_t)
        bt = b_s[rows, :].T
        bt_s[c] = bt
        cb_s[c] = jnp.dot(c_s[rows, :].astype(BF16), bt.astype(BF16), preferred_element_type=F32)
        return carry

    lax.fori_loop(0, nc, tables, 0, unroll=2)

    for j in range(HEADS_PER_GROUP):
        hst_s[0, j] = h0f_ref[0, 0, j]
        hst_s[1, j] = h0b_ref[0, 0, j]

    def chunk_dir(d, c, final):
        tri = tri_fwd if d == 0 else tri_bwd
        tot_row = CHUNK - 1 if d == 0 else 0
        r0 = pl.multiple_of(c * CHUNK, CHUNK)
        rows = pl.ds(r0, CHUNK)
        xs = xs_s[rows, :]
        cc = c_s[rows, :]
        cs = cs_s[rows, :]
        cs_t = cst_s[c]
        bt = bt_s[c]
        cb = cb_s[c]
        ys = []
        for j in range(HEADS_PER_GROUP):
            lane = HEADS_PER_GROUP * d + j
            a_col1 = cs[:, lane:lane + 1]
            a_col = jnp.broadcast_to(a_col1, (CHUNK, CHUNK))
            a_tot = a_col1[tot_row:tot_row + 1, :]
            a_row = cs_t[lane:lane + 1, :]
            g = (cb * jnp.where(tri, jnp.exp2(a_col - a_row), 0.0)).astype(BF16)
            cea = (cc * jnp.exp2(a_col)).astype(BF16)
            xh = xs[:, HEADDIM * j:HEADDIM * (j + 1)].astype(BF16)
            h = hst_s[d, j]
            y_h = (jnp.dot(g, xh, preferred_element_type=F32)
                   + jnp.dot(cea, h.astype(BF16), preferred_element_type=F32))
            w_row = jnp.exp2(a_tot - a_row)
            s_new = jnp.dot((bt * w_row).astype(BF16), xh, preferred_element_type=F32)
            hst_s[d, j] = h * jnp.exp2(a_tot) + s_new
            ys.append(y_h)
        if not emit_y:
            return
        y_c = jnp.concatenate(ys, axis=1)
        if d == 0:
            y_c = y_c + dsk_ref[...] * xs
        if final:
            zc = z_ref[0, rows, :].astype(F32)
            y_ref[0, rows, :] = ((yf_s[rows, :] + y_c) * _silu(zc)).astype(BF16)
        else:
            yf_s[rows, :] = y_c

    def first_half(t, carry):
        chunk_dir(0, t, False)
        chunk_dir(1, nc - 1 - t, False)
        return carry

    def second_half(t, carry):
        chunk_dir(0, t, True)
        chunk_dir(1, nc - 1 - t, True)
        return carry

    lax.fori_loop(0, nc // 2, first_half, 0)
    lax.fori_loop(nc // 2, nc, second_half, 0)
    for j in range(HEADS_PER_GROUP):
        hf_ref[0, 0, j] = hst_s[0, j]
        hb_ref[0, 0, j] = hst_s[1, j]
    if not emit_y:
        y_ref[...] = jnp.zeros(y_ref.shape, y_ref.dtype)


def _ssd_mixer(xbc, dt, z, conv_w, conv_b, dtb, alog, dskip, h0f, h0b, emit_y):
    b, l, _ = xbc.shape
    gw = HEADS_PER_GROUP * HEADDIM
    nxb = (N_GROUPS * gw) // D_STATE
    y_rows = l if emit_y else 8
    st_shape = (b, N_GROUPS, HEADS_PER_GROUP, D_STATE, HEADDIM)
    st_spec = pl.BlockSpec((1, 1, HEADS_PER_GROUP, D_STATE, HEADDIM), lambda bi, g: (bi, g, 0, 0, 0))
    in_specs = [
        pl.BlockSpec((1, l, gw), lambda bi, g: (bi, 0, g)),
        pl.BlockSpec((1, l, D_STATE), lambda bi, g: (bi, 0, nxb + g)),
        pl.BlockSpec((1, l, D_STATE), lambda bi, g: (bi, 0, nxb + N_GROUPS + g)),
        pl.BlockSpec((1, 1, l, 8), lambda bi, g: (bi, g, 0, 0)),
        pl.BlockSpec((1, y_rows, gw), lambda bi, g: (bi, 0, g)),
        pl.BlockSpec((CONV_W, gw), lambda bi, g: (0, g)),
        pl.BlockSpec((CONV_W, D_STATE), lambda bi, g: (0, nxb + g)),
        pl.BlockSpec((CONV_W, D_STATE), lambda bi, g: (0, nxb + N_GROUPS + g)),
        pl.BlockSpec((1, gw), lambda bi, g: (0, g)),
        pl.BlockSpec((1, D_STATE), lambda bi, g: (0, nxb + g)),
        pl.BlockSpec((1, D_STATE), lambda bi, g: (0, nxb + N_GROUPS + g)),
        pl.BlockSpec((1, 1, 8), lambda bi, g: (g, 0, 0)),
        pl.BlockSpec((1, 1, 8), lambda bi, g: (g, 0, 0)),
        pl.BlockSpec((1, gw), lambda bi, g: (0, g)),
        st_spec, st_spec,
    ]
    out_specs = [pl.BlockSpec((1, y_rows, gw), lambda bi, g: (bi, 0, g)), st_spec, st_spec]
    out_shape = [jax.ShapeDtypeStruct((b, y_rows, N_GROUPS * gw), BF16),
                 jax.ShapeDtypeStruct(st_shape, F32), jax.ShapeDtypeStruct(st_shape, F32)]
    nc = l // CHUNK
    scratch = [pltpu.VMEM((l, gw), F32), pltpu.VMEM((l, D_STATE), F32), pltpu.VMEM((l, D_STATE), F32),
               pltpu.VMEM((l, 8), F32), pltpu.VMEM((l, 8), F32), pltpu.VMEM((l, 8), F32),
               pltpu.VMEM((nc, 8, CHUNK), F32),
               pltpu.VMEM((nc, D_STATE, CHUNK), F32), pltpu.VMEM((nc, CHUNK, CHUNK), F32),
               pltpu.VMEM((l if emit_y else 8, gw), F32),
               pltpu.VMEM((2, HEADS_PER_GROUP, D_STATE, HEADDIM), F32)]
    return pl.pallas_call(
        functools.partial(_ssd_kernel, seq=l, emit_y=emit_y),
        grid=(b, N_GROUPS),
        in_specs=in_specs, out_specs=out_specs, out_shape=out_shape,
        scratch_shapes=scratch,
        compiler_params=_params("parallel", "arbitrary"),
        name="ssd_mixer" if emit_y else "ssd_mixer_ctx",
    )(xbc, xbc, xbc, dt, z, conv_w, conv_w, conv_w, conv_b, conv_b, conv_b, dtb, alog, dskip, h0f, h0b)


def _fourier_kernel(f_ref, cc_ref, sc_ref, w_ref, ca_ref, sa_ref, cb_ref, sb_ref, o_ref,
                    u_s, v_s, cl_s, sl_s, *, seq, rows):
    @pl.when(pl.program_id(1) == 0)
    def _():
        scale = 1.0 / math.sqrt(seq * FOURIER_CH)
        for g in range(N_FOURIER_GROUPS):
            w = w_ref[g]
            a = jnp.dot(cc_ref[...], w, preferred_element_type=F32, precision=HIGHEST) * scale
            bm = jnp.dot(sc_ref[...], w, preferred_element_type=F32, precision=HIGHEST) * scale
            cols = slice(FOURIER_CH * g, FOURIER_CH * (g + 1))
            fg = f_ref[0, :, cols]
            u_s[:, cols] = jnp.dot(fg, a.astype(BF16), preferred_element_type=F32).astype(BF16)
            v_s[:, cols] = jnp.dot(fg, bm.astype(BF16), preferred_element_type=F32).astype(BF16)

    k2_0 = pl.program_id(1) * (rows // DFT_SPLIT)
    cbeta = cb_ref[...]
    sbeta = sb_ref[...]
    for j in range(rows // DFT_SPLIT):
        calpha = ca_ref[pl.ds(k2_0 + j, 1), :]
        salpha = sa_ref[pl.ds(k2_0 + j, 1), :]
        sub = slice(DFT_SPLIT * j, DFT_SPLIT * (j + 1))
        cl_s[sub, :] = (calpha * cbeta - salpha * sbeta).astype(BF16)
        sl_s[sub, :] = (salpha * cbeta + calpha * sbeta).astype(BF16)
    out = (jnp.dot(cl_s[...], u_s[...], preferred_element_type=F32)
           - jnp.dot(sl_s[...], v_s[...], preferred_element_type=F32))
    o_ref[0] = out.astype(BF16)


def _dft_tables(n, dtype):
    k = lax.broadcasted_iota(jnp.int32, (n, n), 0)
    l = lax.broadcasted_iota(jnp.int32, (n, n), 1)
    ang = ((k * l) % n).astype(F32) * (2.0 * math.pi / n)
    return jnp.cos(ang).astype(dtype), jnp.sin(ang).astype(dtype)


def _dft_factor_tables(n):
    n2 = n // DFT_SPLIT
    k2 = lax.broadcasted_iota(jnp.int32, (n2, n), 0)
    k1 = lax.broadcasted_iota(jnp.int32, (DFT_SPLIT, n), 0)
    alpha = ((k2 * lax.broadcasted_iota(jnp.int32, (n2, n), 1)) % n2).astype(F32) * (2.0 * math.pi / n2)
    beta = ((k1 * lax.broadcasted_iota(jnp.int32, (DFT_SPLIT, n), 1)) % n).astype(F32) * (2.0 * math.pi / n)
    return jnp.cos(alpha), jnp.sin(alpha), jnp.cos(beta), jnp.sin(beta)


def _fourier_mixer(f, w_four):
    b, l, df = f.shape
    tr = min(512, l)
    cc, sc = _dft_tables(FOURIER_CH, F32)
    ca, sa, cb, sb = _dft_factor_tables(l)
    const = lambda bi, i: (0, 0)
    return pl.pallas_call(
        functools.partial(_fourier_kernel, seq=l, rows=tr),
        grid=(b, l // tr),
        in_specs=[pl.BlockSpec((1, l, df), lambda bi, i: (bi, 0, 0)),
                  pl.BlockSpec((FOURIER_CH, FOURIER_CH), const),
                  pl.BlockSpec((FOURIER_CH, FOURIER_CH), const),
                  pl.BlockSpec(w_four.shape, lambda bi, i: (0, 0, 0)),
                  pl.BlockSpec(ca.shape, const), pl.BlockSpec(sa.shape, const),
                  pl.BlockSpec(cb.shape, const), pl.BlockSpec(sb.shape, const)],
        out_specs=pl.BlockSpec((1, tr, df), lambda bi, i: (bi, i, 0)),
        out_shape=jax.ShapeDtypeStruct((b, l, df), BF16),
        scratch_shapes=[pltpu.VMEM((l, df), BF16), pltpu.VMEM((l, df), BF16),
                        pltpu.VMEM((tr, l), BF16), pltpu.VMEM((tr, l), BF16)],
        compiler_params=_params("parallel", "arbitrary"),
        name="fourier_mixer",
    )(f, cc, sc, w_four, ca, sa, cb, sb)


def _outproj_router_kernel(y_ref, four_ref, x_ref, g1_ref, sh_ref, sc_ref, nssd_ref, n2_ref,
                           wos_ref, wof_ref, wr_ref, br_ref,
                           x1_ref, h_ref, eid_ref, gate_ref):
    y = y_ref[0].astype(F32)
    ms = jnp.mean(y * y, axis=-1, keepdims=True)
    yn = (y * lax.rsqrt(ms + EPS) * nssd_ref[...]).astype(BF16)
    mix = (jnp.dot(yn, wos_ref[...], preferred_element_type=F32)
           + jnp.dot(four_ref[0], wof_ref[...], preferred_element_type=F32))
    x1 = x_ref[0] + g1_ref[0] * mix
    x1_ref[0] = x1
    ms2 = jnp.mean(x1 * x1, axis=-1, keepdims=True)
    h = (x1 * lax.rsqrt(ms2 + EPS) * n2_ref[...]) * (1.0 + sc_ref[0]) + sh_ref[0]
    _store_token_tiles(h_ref, (0,), h)

    lg = _dot_split3(h, wr_ref[...]) + br_ref[...]
    tm = lg.shape[0]
    lane = lax.broadcasted_iota(jnp.int32, (tm, ROUTE_LANES), 1)
    lane_f = lane.astype(F32)
    neg = jnp.float32(-1e30)
    big = jnp.float32(1e9)
    is_grp = lane < N_EXPERT_GROUPS
    gl = jnp.where(is_grp, lg, neg)
    gmax = jnp.max(gl, axis=-1, keepdims=True)
    gsum = jnp.sum(jnp.where(is_grp, jnp.exp(gl - gmax), 0.0), axis=-1, keepdims=True)
    grp = jnp.min(jnp.where(gl == gmax, lane_f, big), axis=-1, keepdims=True)
    p_grp = 1.0 / gsum
    lo = N_EXPERT_GROUPS + EXPERTS_PER_GROUP * grp
    in_grp = jnp.logical_and(lane_f >= lo, lane_f < lo + EXPERTS_PER_GROUP)
    el = jnp.where(in_grp, lg, neg)
    m1 = jnp.max(el, axis=-1, keepdims=True)
    i1 = jnp.min(jnp.where(el == m1, lane_f, big), axis=-1, keepdims=True)
    el2 = jnp.where(lane_f == i1, neg, el)
    m2 = jnp.max(el2, axis=-1, keepdims=True)
    i2 = jnp.min(jnp.where(el2 == m2, lane_f, big), axis=-1, keepdims=True)
    e2 = jnp.exp(m2 - m1)
    den = 1.0 + e2
    gate1 = p_grp / den
    gate2 = p_grp * e2 / den
    lane8 = lax.broadcasted_iota(jnp.int32, (tm, 8), 1)
    eid = jnp.where(lane8 == 0, i1 - N_EXPERT_GROUPS, jnp.where(lane8 == 1, i2 - N_EXPERT_GROUPS, 0.0))
    eid_ref[0] = eid.astype(jnp.int32)
    gate_ref[0] = jnp.where(lane8 == 0, gate1, jnp.where(lane8 == 1, gate2, 0.0))


def _outproj_router(y, four, x, g1, sh2, sc2, nssd, n2, wos, wof, wr, br):
    b, l, d = x.shape
    tm = min(512, l)
    row = lambda bi, i: (bi, i, 0)
    vec = lambda bi, i: (bi, 0, 0)
    const = lambda bi, i: (0, 0)
    return pl.pallas_call(
        _outproj_router_kernel,
        grid=(b, l // tm),
        in_specs=[pl.BlockSpec((1, tm, y.shape[2]), row),
                  pl.BlockSpec((1, tm, four.shape[2]), row),
                  pl.BlockSpec((1, tm, d), row),
                  pl.BlockSpec((1, 1, d), vec), pl.BlockSpec((1, 1, d), vec), pl.BlockSpec((1, 1, d), vec),
                  pl.BlockSpec((1, y.shape[2]), const), pl.BlockSpec((1, d), const),
                  pl.BlockSpec(wos.shape, const), pl.BlockSpec(wof.shape, const),
                  pl.BlockSpec(wr.shape, const), pl.BlockSpec(br.shape, const)],
        out_specs=[pl.BlockSpec((1, tm, d), row), pl.BlockSpec((1, tm * TILE_ROWS, LANES), row),
                   pl.BlockSpec((1, tm, 8), row), pl.BlockSpec((1, tm, 8), row)],
        out_shape=[jax.ShapeDtypeStruct((b, l, d), F32), jax.ShapeDtypeStruct((b, l * TILE_ROWS, LANES), F32),
                   jax.ShapeDtypeStruct((b, l, 8), jnp.int32), jax.ShapeDtypeStruct((b, l, 8), F32)],
        compiler_params=_params("parallel", "arbitrary"),
        name="outproj_router",
    )(y, four, x, g1, sh2, sc2, nssd, n2, wos, wof, wr, br)


def _expert_kernel(bexp_ref, nused_ref, tokc_ref, tokn_ref, dstp_ref, dstc_ref, h_hbm, wg_ref, wu_ref, wd_ref,
                   y_hbm, hbuf, ybuf, act_s, wg_s, wu_s, wd_s, sem_in, sem_out):
    i = pl.program_id(0)
    n_used = nused_ref[0]
    active = i < n_used
    is_last = i == n_used - 1
    slot = i % 2
    other = 1 - slot
    de = wg_s.shape[1]
    d = wd_s.shape[1]
    n_phase = 8
    rows_per_phase = MOE_ROWS // n_phase
    blk_tile_rows = MOE_ROWS * TILE_ROWS

    def tile(ref, lead, row0):
        return ref.at[lead + (pl.ds(row0, TILE_ROWS), slice(None))]

    def start_gather(tok_ref, s, r):
        src = tile(h_hbm, (), pl.multiple_of(tok_ref[0, 0, r], TILE_ROWS))
        pltpu.make_async_copy(src, tile(hbuf, (s,), r * TILE_ROWS), sem_in.at[s]).start(priority=r % 2)

    def start_scatter(dst_ref, s, r):
        dst = tile(y_hbm, (), pl.multiple_of(dst_ref[0, 0, r], TILE_ROWS))
        pltpu.make_async_copy(tile(ybuf, (s,), r * TILE_ROWS), dst, sem_out.at[s]).start(priority=r % 2)

    def wait_gather(s):
        pltpu.make_async_copy(h_hbm.at[pl.ds(0, blk_tile_rows), :], hbuf.at[s], sem_in.at[s]).wait()

    def wait_scatter(s):
        pltpu.make_async_copy(ybuf.at[s], y_hbm.at[pl.ds(0, blk_tile_rows), :], sem_out.at[s]).wait()

    @pl.when(i == 0)
    def _():
        for r in range(MOE_ROWS):
            start_gather(tokc_ref, 0, r)

    prev = bexp_ref[jnp.maximum(i - 1, 0)]

    @pl.when(jnp.logical_and(active, jnp.logical_or(i == 0, bexp_ref[i] != prev)))
    def _():
        wg_s[...] = wg_ref[0].astype(BF16)
        wu_s[...] = wu_ref[0].astype(BF16)
        wd_s[...] = wd_ref[0].astype(BF16)

    @pl.when(active)
    def _():
        wait_gather(slot)

    @pl.when(jnp.logical_and(active, i >= 2))
    def _():
        wait_scatter(slot)

    def step(with_scatter, cur):
        nxt = 1 - cur
        xb = _load_token_tiles(hbuf, (cur,), MOE_ROWS).astype(BF16)
        ys = []
        for p in range(n_phase):
            for r in range(p * rows_per_phase, (p + 1) * rows_per_phase):
                start_gather(tokn_ref, nxt, r)
                if with_scatter:
                    start_scatter(dstp_ref, nxt, r)
            if p < n_phase // 2:
                w = de // (n_phase // 2)
                cols = slice(p * w, (p + 1) * w)
                gact = jnp.dot(xb, wg_s[:, cols], preferred_element_type=F32)
                up = jnp.dot(xb, wu_s[:, cols], preferred_element_type=F32)
                act_s[:, cols] = (_silu(gact) * up).astype(BF16)
            else:
                w = d // (n_phase // 2)
                cols = slice((p - n_phase // 2) * w, (p - n_phase // 2 + 1) * w)
                ys.append(jnp.dot(act_s[...], wd_s[:, cols], preferred_element_type=F32))
        _store_token_tiles(ybuf, (cur,), jnp.concatenate(ys, axis=1))

    @pl.when(i == 0)
    def _():
        step(False, 0)

    @pl.when(jnp.logical_and(active, jnp.logical_and(i > 0, slot == 0)))
    def _():
        step(True, 0)

    @pl.when(jnp.logical_and(active, slot == 1))
    def _():
        step(True, 1)

    @pl.when(is_last)
    def _():
        for r in range(MOE_ROWS):
            start_scatter(dstc_ref, slot, r)
        wait_gather(other)
        wait_scatter(other)
        wait_scatter(slot)

    @pl.when(jnp.logical_not(active))
    def _():
        ybuf[0] = jnp.zeros((blk_tile_rows, LANES), F32)
        dst0 = pl.multiple_of(dstc_ref[0, 0, 0], TILE_ROWS)
        fill = pltpu.make_async_copy(ybuf.at[0], y_hbm.at[pl.ds(dst0, blk_tile_rows), :], sem_out.at[0])
        fill.start()
        fill.wait()


def _experts(h_tiles, row_tok, row_dst, blk_exp, n_used, w_eg, w_eu, w_ed):
    n_blocks = blk_exp.shape[0]
    d, de = w_eg.shape[1], w_eg.shape[2]
    assert d == TILE_ROWS * LANES
    idx_shape = (n_blocks, 1, MOE_ROWS)
    idx_block = (1, 1, MOE_ROWS)
    smem = pltpu.SMEM
    blk_tile_rows = MOE_ROWS * TILE_ROWS
    grid_spec = pltpu.PrefetchScalarGridSpec(
        num_scalar_prefetch=2,
        grid=(n_blocks,),
        in_specs=[pl.BlockSpec(idx_block, lambda i, be, nu: (i, 0, 0), memory_space=smem),
                  pl.BlockSpec(idx_block, lambda i, be, nu: (jnp.minimum(i + 1, n_blocks - 1), 0, 0),
                               memory_space=smem),
                  pl.BlockSpec(idx_block, lambda i, be, nu: (jnp.maximum(i - 1, 0), 0, 0), memory_space=smem),
                  pl.BlockSpec(idx_block, lambda i, be, nu: (i, 0, 0), memory_space=smem),
                  pl.BlockSpec(memory_space=pl.ANY),
                  pl.BlockSpec((1, d, de), lambda i, be, nu: (be[i], 0, 0)),
                  pl.BlockSpec((1, d, de), lambda i, be, nu: (be[i], 0, 0)),
                  pl.BlockSpec((1, de, d), lambda i, be, nu: (be[i], 0, 0))],
        out_specs=pl.BlockSpec(memory_space=pl.ANY),
        scratch_shapes=[pltpu.VMEM((2, blk_tile_rows, LANES), F32), pltpu.VMEM((2, blk_tile_rows, LANES), F32),
                        pltpu.VMEM((MOE_ROWS, de), BF16),
                        pltpu.VMEM((d, de), BF16), pltpu.VMEM((d, de), BF16), pltpu.VMEM((de, d), BF16),
                        pltpu.SemaphoreType.DMA((2,)), pltpu.SemaphoreType.DMA((2,))],
    )
    row_tok = (row_tok * TILE_ROWS).reshape(idx_shape)
    row_dst = (row_dst * TILE_ROWS).reshape(idx_shape)
    return pl.pallas_call(
        _expert_kernel,
        grid_spec=grid_spec,
        out_shape=jax.ShapeDtypeStruct((n_blocks * blk_tile_rows, LANES), F32),
        compiler_params=_params("arbitrary"),
        name="moe_experts",
    )(blk_exp, n_used, row_tok, row_tok, row_dst, row_dst, h_tiles, w_eg, w_eu, w_ed)


def _route_tables(eid, n_tok):
    n_assign = 2 * n_tok
    e_flat = jnp.concatenate([eid[:, 0], eid[:, 1]])
    order = jnp.argsort(e_flat).astype(jnp.int32)
    experts = jnp.arange(N_EXPERTS, dtype=jnp.int32)
    counts = jnp.sum((e_flat[:, None] == experts[None, :]).astype(jnp.int32), axis=0)
    start = jnp.cumsum(counts) - counts
    padded = (counts + MOE_ROWS - 1) // MOE_ROWS * MOE_ROWS
    end_pad = jnp.cumsum(padded)
    start_pad = end_pad - padded
    n_blocks = -(-(n_assign + N_EXPERTS * (MOE_ROWS - 1)) // MOE_ROWS)
    blk_row0 = jnp.arange(n_blocks, dtype=jnp.int32) * MOE_ROWS
    blk_exp = jnp.minimum(jnp.sum((end_pad[None, :] <= blk_row0[:, None]).astype(jnp.int32), axis=1),
                          N_EXPERTS - 1).astype(jnp.int32)
    j = (blk_row0 - start_pad[blk_exp])[:, None] + jnp.arange(MOE_ROWS, dtype=jnp.int32)[None, :]
    valid = j < counts[blk_exp][:, None]
    src = jnp.clip(start[blk_exp][:, None] + j, 0, n_assign - 1)
    assign = order[src.reshape(-1)].reshape(src.shape)
    row_tok = jnp.where(valid, jnp.where(assign >= n_tok, assign - n_tok, assign), 0)
    spare = n_assign + jnp.cumsum((~valid).reshape(-1).astype(jnp.int32)).reshape(valid.shape) - 1
    row_dst = jnp.where(valid, assign, spare)
    n_used = (end_pad[-1:] // MOE_ROWS).astype(jnp.int32)
    return row_tok.astype(jnp.int32), row_dst.astype(jnp.int32), blk_exp, n_used


def _combine_kernel(x1_ref, y0_ref, y1_ref, gate_ref, g2_ref, nw_ref, o_ref):
    gate = gate_ref[0]
    tm = gate.shape[0]
    moe = (gate[:, 0:1] * _load_token_tiles(y0_ref, (), tm) + gate[:, 1:2] * _load_token_tiles(y1_ref, (), tm))
    x2 = x1_ref[0] + g2_ref[0] * moe
    ms = jnp.mean(x2 * x2, axis=-1, keepdims=True)
    o_ref[0] = x2 * lax.rsqrt(ms + EPS) * nw_ref[...]


def _combine(x1, y_assign, gate, g2, final_norm):
    b, l, d = x1.shape
    tm = min(512, l)
    nt = l // tm
    row = lambda bi, i: (bi, i, 0)
    return pl.pallas_call(
        _combine_kernel,
        grid=(b, nt),
        in_specs=[pl.BlockSpec((1, tm, d), row),
                  pl.BlockSpec((tm * TILE_ROWS, LANES), lambda bi, i: (bi * nt + i, 0)),
                  pl.BlockSpec((tm * TILE_ROWS, LANES), lambda bi, i: (b * nt + bi * nt + i, 0)),
                  pl.BlockSpec((1, tm, 8), row),
                  pl.BlockSpec((1, 1, d), lambda bi, i: (bi, 0, 0)),
                  pl.BlockSpec((1, d), lambda bi, i: (0, 0))],
        out_specs=pl.BlockSpec((1, tm, d), row),
        out_shape=jax.ShapeDtypeStruct((b, l, d), F32),
        compiler_params=_params("parallel", "arbitrary"),
        name="moe_combine",
    )(x1, y_assign, y_assign, gate, g2, final_norm.reshape(1, d))


def _group_major(v):
    return v.reshape(2, N_GROUPS, HEADS_PER_GROUP).transpose(1, 0, 2).reshape(N_GROUPS, 1, 2 * HEADS_PER_GROUP)


def kernel(x, c, ctx, c_ctx, w_mod, b_mod, norm1, w_in, conv_w, conv_b, dt_bias, a_log, d_skip, ssd_norm,
           w_four, w_out, norm2, w_rg, b_rg, w_re, b_re, w_eg, w_eu, w_ed, final_norm):
    bsz, seq, d = x.shape
    n_tok = bsz * seq
    d_ssd = N_GROUPS * HEADS_PER_GROUP * HEADDIM
    conv_dim = d_ssd + 2 * N_GROUPS * D_STATE
    n_heads = N_GROUPS * HEADS_PER_GROUP
    layer = 0

    c_rows = jnp.zeros((16, d), F32).at[:bsz].set(c).at[bsz].set(c_ctx)
    mod = _modulation(c_rows, w_mod[layer], b_mod[layer])
    sh1, sc1, g1, sh2, sc2, g2 = [m[:bsz, None, :] for m in jnp.split(mod, 6, axis=-1)]
    sh1c, sc1c = [jnp.broadcast_to(m[bsz][None, None, :], (bsz, 1, d)) for m in jnp.split(mod, 6, axis=-1)[:2]]

    w = w_in[layer]
    wz = w[:, :d_ssd].astype(BF16)
    wx = w[:, d_ssd:d_ssd + conv_dim].astype(BF16)
    wdt = w[:, d_ssd + conv_dim:d_ssd + conv_dim + 2 * n_heads]
    wdt = wdt.reshape(d, 2, N_GROUPS, HEADS_PER_GROUP).transpose(0, 2, 1, 3).reshape(d, 2 * n_heads)
    wdt = jnp.pad(wdt, ((0, 0), (0, 128 - 2 * n_heads))).astype(BF16)
    wf = w[:, d_ssd + conv_dim + 2 * n_heads:].astype(BF16)
    n1 = norm1[layer].reshape(1, d)

    dtb = _group_major(dt_bias[layer])
    alog = _group_major(a_log[layer])
    dsk = jnp.repeat(d_skip[layer], HEADDIM).reshape(1, d_ssd)
    cw = conv_w[layer]
    cb = conv_b[layer].reshape(1, conv_dim)

    xbc_c, dt_c = _in_projection(ctx, sh1c, sc1c, n1, wx, wdt)
    h_zero = jnp.zeros((bsz, N_GROUPS, HEADS_PER_GROUP, D_STATE, HEADDIM), F32)
    z_dummy = jnp.zeros((bsz, 8, d_ssd), BF16)
    _, hf_c, hb_c = _ssd_mixer(xbc_c, dt_c, z_dummy, cw, cb, dtb, alog, dsk, h_zero, h_zero, emit_y=False)

    xbc_l, dt_l, z_l, f_l = _in_projection(x, sh1, sc1, n1, wx, wdt, wz, wf)
    y_l, _, _ = _ssd_mixer(xbc_l, dt_l, z_l, cw, cb, dtb, alog, dsk, hf_c, hb_c, emit_y=True)
    four = _fourier_mixer(f_l, w_four[layer])

    wo = w_out[layer]
    wr = jnp.concatenate([w_rg[layer], w_re[layer].transpose(1, 0, 2).reshape(d, N_EXPERTS)], axis=1)
    wr = jnp.pad(wr, ((0, 0), (0, ROUTE_LANES - wr.shape[1])))
    br = jnp.pad(jnp.concatenate([b_rg[layer], b_re[layer].reshape(-1)]),
                 (0, ROUTE_LANES - N_EXPERT_GROUPS - N_EXPERTS)).reshape(1, ROUTE_LANES)
    x1, h, eid, gate = _outproj_router(
        y_l, four, x, g1, sh2, sc2, ssd_norm[layer].reshape(1, d_ssd), norm2[layer].reshape(1, d),
        wo[:d_ssd].astype(BF16), wo[d_ssd:].astype(BF16), wr, br)

    row_tok, row_dst, blk_exp, n_used = _route_tables(eid.reshape(n_tok, 8), n_tok)
    y_assign = _experts(h.reshape(n_tok * TILE_ROWS, LANES), row_tok, row_dst, blk_exp, n_used,
                        w_eg[layer], w_eu[layer], w_ed[layer])
    return _combine(x1, y_assign, gate, g2, final_norm)
```

```python
import functools
import math

import jax
import jax.numpy as jnp
from jax import lax
from jax.experimental import pallas as pl
from jax.experimental.pallas import tpu as pltpu

F32 = jnp.float32
BF16 = jnp.bfloat16
HIGHEST = lax.Precision.HIGHEST

EPS = 1e-6
LOG2E = 1.4426950408889634
CHUNK = 128
N_GROUPS = 4
HEADS_PER_GROUP = 4
HEADDIM = 64
D_STATE = 128
CONV_W = 5
CONV_HALO = 16
N_FOURIER_GROUPS = 4
DFT_SPLIT = 64
FOURIER_CH = 128
N_EXPERT_GROUPS = 4
EXPERTS_PER_GROUP = 8
N_EXPERTS = N_EXPERT_GROUPS * EXPERTS_PER_GROUP
ROUTE_LANES = 128
MOE_ROWS = 256
VMEM_LIMIT_BYTES = 56 * 1024 * 1024


def _params(*sem):
    return pltpu.CompilerParams(dimension_semantics=sem, vmem_limit_bytes=VMEM_LIMIT_BYTES)


def _silu(v):
    h = 0.5 * v
    return h + h * jnp.tanh(h)


LANES = 128
TILE_ROWS = 8


def _store_token_tiles(ref, lead, val):
    rows = val.shape[0]
    for j in range(TILE_ROWS):
        ref[lead + (pl.ds(j, rows, stride=TILE_ROWS), slice(None))] = val[:, LANES * j:LANES * (j + 1)]


def _load_token_tiles(ref, lead, rows):
    return jnp.concatenate(
        [ref[lead + (pl.ds(j, rows, stride=TILE_ROWS), slice(None))] for j in range(TILE_ROWS)], axis=1)


def _dot_split3(a, b):
    a_hi = a.astype(BF16)
    a_lo = (a - a_hi.astype(F32)).astype(BF16)
    b_hi = b.astype(BF16)
    b_lo = (b - b_hi.astype(F32)).astype(BF16)
    dot = functools.partial(jnp.dot, preferred_element_type=F32)
    return dot(a_hi, b_hi) + (dot(a_lo, b_hi) + dot(a_hi, b_lo))


def _mod_kernel(c_ref, w_ref, b_ref, o_ref):
    s = _silu(c_ref[...])
    o_ref[...] = jnp.dot(s, w_ref[...], preferred_element_type=F32, precision=HIGHEST) + b_ref[...]


def _modulation(c_rows, w_mod, b_mod):
    rows, d = c_rows.shape
    n = w_mod.shape[1]
    tn = 512
    return pl.pallas_call(
        _mod_kernel,
        grid=(n // tn,),
        in_specs=[pl.BlockSpec((rows, d), lambda j: (0, 0)),
                  pl.BlockSpec((d, tn), lambda j: (0, j)),
                  pl.BlockSpec((1, tn), lambda j: (0, j))],
        out_specs=pl.BlockSpec((rows, tn), lambda j: (0, j)),
        out_shape=jax.ShapeDtypeStruct((rows, n), F32),
        compiler_params=_params("arbitrary"),
        name="modulation",
    )(c_rows, w_mod, b_mod.reshape(1, n))


def _inproj_kernel(xp_ref, x_ref, xn_ref, sh_ref, sc_ref, nw_ref, wx_ref, wdt_ref, cw_ref, cb_ref, *rest,
                   with_zf):
    if with_zf:
        wz_ref, wf_ref, xbc_ref, dt_ref, z_ref, f_ref, pre_s = rest
    else:
        xbc_ref, dt_ref, pre_s = rest
    tm = x_ref.shape[1]
    i = pl.program_id(1)
    xa = jnp.concatenate([xp_ref[0], x_ref[0], xn_ref[0]], axis=0)
    ms = jnp.mean(xa * xa, axis=-1, keepdims=True)
    xnorm = xa * lax.rsqrt(ms + EPS) * nw_ref[...]
    xm_all = (xnorm * (1.0 + sc_ref[0]) + sh_ref[0]).astype(BF16)
    xm = xm_all[CONV_HALO:CONV_HALO + tm]

    pre_s[...] = jnp.dot(xm_all, wx_ref[...], preferred_element_type=F32)
    top, bot = slice(0, CONV_HALO), slice(CONV_HALO + tm, 2 * CONV_HALO + tm)
    pre_s[top, :] = pre_s[top, :] * (i > 0).astype(F32)
    pre_s[bot, :] = pre_s[bot, :] * (i < pl.num_programs(1) - 1).astype(F32)

    rblk, cblk = 64, 256
    for c0 in range(0, wx_ref.shape[1], cblk):
        cols = slice(c0, c0 + cblk)
        w = cw_ref[:, cols]
        bias = jnp.broadcast_to(cb_ref[:, cols], (rblk, cblk))
        for r0 in range(0, tm, rblk):
            lo = r0 + CONV_HALO - 8
            win = pre_s[lo:lo + rblk + 16, cols]
            acc = bias
            for k in range(CONV_W):
                off = 8 - CONV_W // 2 + k
                acc = acc + w[k:k + 1, :] * win[off:off + rblk, :]
            xbc_ref[0, r0:r0 + rblk, cols] = _silu(acc).astype(BF16)

    dt = jnp.dot(xm, wdt_ref[...], preferred_element_type=F32)
    for g in range(N_GROUPS):
        dt_ref[0, g] = dt[:, 8 * g:8 * g + 8]
    if with_zf:
        z_ref[0] = jnp.dot(xm, wz_ref[...], preferred_element_type=F32).astype(BF16)
        f_ref[0] = jnp.dot(xm, wf_ref[...], preferred_element_type=F32).astype(BF16)


def _in_projection(x, shift, scale, norm_w, wx, wdt, conv_w, conv_b, wz=None, wf=None):
    b, l, d = x.shape
    tm = min(512, l)
    with_zf = wz is not None
    hb = tm // CONV_HALO
    n_hb = l // CONV_HALO
    row = lambda bi, i: (bi, i, 0)
    vec = lambda bi, i: (bi, 0, 0)
    const = lambda bi, i: (0, 0)
    in_specs = [pl.BlockSpec((1, CONV_HALO, d), lambda bi, i: (bi, jnp.maximum(i * hb - 1, 0), 0)),
                pl.BlockSpec((1, tm, d), row),
                pl.BlockSpec((1, CONV_HALO, d), lambda bi, i: (bi, jnp.minimum((i + 1) * hb, n_hb - 1), 0)),
                pl.BlockSpec((1, 1, d), vec),
                pl.BlockSpec((1, 1, d), vec),
                pl.BlockSpec((1, d), const),
                pl.BlockSpec(wx.shape, const),
                pl.BlockSpec(wdt.shape, const),
                pl.BlockSpec(conv_w.shape, const),
                pl.BlockSpec(conv_b.shape, const)]
    args = [x, x, x, shift, scale, norm_w, wx, wdt, conv_w, conv_b]
    out_specs = [pl.BlockSpec((1, tm, wx.shape[1]), row),
                 pl.BlockSpec((1, N_GROUPS, tm, 8), lambda bi, i: (bi, 0, i, 0))]
    out_shape = [jax.ShapeDtypeStruct((b, l, wx.shape[1]), BF16),
                 jax.ShapeDtypeStruct((b, N_GROUPS, l, 8), F32)]
    if with_zf:
        in_specs += [pl.BlockSpec(wz.shape, const), pl.BlockSpec(wf.shape, const)]
        args += [wz, wf]
        out_specs += [pl.BlockSpec((1, tm, wz.shape[1]), row), pl.BlockSpec((1, tm, wf.shape[1]), row)]
        out_shape += [jax.ShapeDtypeStruct((b, l, wz.shape[1]), BF16),
                      jax.ShapeDtypeStruct((b, l, wf.shape[1]), BF16)]
    return pl.pallas_call(
        functools.partial(_inproj_kernel, with_zf=with_zf),
        grid=(b, l // tm),
        in_specs=in_specs,
        out_specs=out_specs,
        out_shape=out_shape,
        scratch_shapes=[pltpu.VMEM((tm + 2 * CONV_HALO, wx.shape[1]), F32)],
        compiler_params=_params("parallel", "arbitrary"),
        name="in_projection_zf" if with_zf else "in_projection",
    )(*args)


def _ssd_kernel(xs_ref, xb_ref, xc_ref, dt_ref, z_ref,
                dtb_ref, alog_ref, dsk_ref, h0f_ref, h0b_ref,
                y_ref, hf_ref, hb_ref,
                dt_s, cs_s, cst_s, bt_s, cb_s, yf_s, hst_s, *, seq, emit_y):
    nc = seq // CHUNK

    dtr = dt_ref[0, 0] + dtb_ref[0]
    dtv = jnp.maximum(dtr, 0.0) + jnp.log(1.0 + jnp.exp(-jnp.abs(dtr)))
    dt_s[...] = dtv
    a_log2 = -jnp.exp(alog_ref[0]) * LOG2E

    ri = lax.broadcasted_iota(jnp.int32, (CHUNK, CHUNK), 0)
    ci = lax.broadcasted_iota(jnp.int32, (CHUNK, CHUNK), 1)
    tri_fwd = ci <= ri
    tri_bwd = ci >= ri
    is_fwd_lane8 = lax.broadcasted_iota(jnp.int32, (CHUNK, 8), 1) < HEADS_PER_GROUP
    lane_pad = jnp.zeros((CHUNK, CHUNK - 8), F32)

    tri16 = tri_fwd.astype(BF16)

    def tables(c, carry):
        r0 = pl.multiple_of(c * CHUNK, CHUNK)
        rows = pl.ds(r0, CHUNK)
        da = dt_s[rows, :] * a_log2
        p0 = da.astype(BF16)
        r1 = da - p0.astype(F32)
        p1 = r1.astype(BF16)
        p2 = (r1 - p1.astype(F32)).astype(BF16)
        packed = jnp.concatenate([p0, p1, p2, jnp.zeros((CHUNK, CHUNK - 24), BF16)], axis=1)
        acc = jnp.dot(tri16, packed, preferred_element_type=F32)
        cs_f = acc[:, 0:8] + acc[:, 8:16] + acc[:, 16:24]
        cs_b = cs_f[CHUNK - 1:CHUNK, :] - cs_f + da
        cs = jnp.where(is_fwd_lane8, cs_f, cs_b)
        cs_s[rows, :] = cs
        cs_t = jnp.concatenate([cs, lane_pad], axis=1).T[:8, :]
        dt_t = jnp.concatenate([dt_s[rows, :], lane_pad], axis=1).T[:8, :]
        cst_s[c] = cs_t - jnp.log2(dt_t)
        bt = xb_ref[0, rows, :].astype(F32).T
        bt_s[c] = bt
        cb_s[c] = jnp.dot(xc_ref[0, rows, :], bt.astype(BF16), preferred_element_type=F32)
        return carry

    lax.fori_loop(0, nc, tables, 0, unroll=2)

    for j in range(HEADS_PER_GROUP):
        hst_s[0, j] = h0f_ref[0, 0, j]
        hst_s[1, j] = h0b_ref[0, 0, j]

    def chunk_dir(d, c, final):
        tri = tri_fwd if d == 0 else tri_bwd
        tot_row = CHUNK - 1 if d == 0 else 0
        r0 = pl.multiple_of(c * CHUNK, CHUNK)
        rows = pl.ds(r0, CHUNK)
        xs = xs_ref[0, rows, :]
        cc = xc_ref[0, rows, :].astype(F32)
        cs = cs_s[rows, :]
        cs_t = cst_s[c]
        bt = bt_s[c]
        cb = cb_s[c]
        ys = []
        for j in range(HEADS_PER_GROUP):
            lane = HEADS_PER_GROUP * d + j
            a_col1 = cs[:, lane:lane + 1]
            a_col = jnp.broadcast_to(a_col1, (CHUNK, CHUNK))
            a_tot = a_col1[tot_row:tot_row + 1, :]
            a_row = cs_t[lane:lane + 1, :]
            g = (cb * jnp.where(tri, jnp.exp2(a_col - a_row), 0.0)).astype(BF16)
            cea = (cc * jnp.exp2(a_col)).astype(BF16)
            xh = xs[:, HEADDIM * j:HEADDIM * (j + 1)]
            h = hst_s[d, j]
            y_h = (jnp.dot(g, xh, preferred_element_type=F32)
                   + jnp.dot(cea, h.astype(BF16), preferred_element_type=F32))
            w_row = jnp.exp2(a_tot - a_row)
            s_new = jnp.dot((bt * w_row).astype(BF16), xh, preferred_element_type=F32)
            hst_s[d, j] = h * jnp.exp2(a_tot) + s_new
            ys.append(y_h)
        if not emit_y:
            return
        y_c = jnp.concatenate(ys, axis=1)
        if d == 0:
            y_c = y_c + dsk_ref[...] * xs.astype(F32)
        if final:
            zc = z_ref[0, rows, :].astype(F32)
            y_ref[0, rows, :] = ((yf_s[rows, :] + y_c) * _silu(zc)).astype(BF16)
        else:
            yf_s[rows, :] = y_c

    def first_half(t, carry):
        chunk_dir(0, t, False)
        chunk_dir(1, nc - 1 - t, False)
        return carry

    def second_half(t, carry):
        chunk_dir(0, t, True)
        chunk_dir(1, nc - 1 - t, True)
        return carry

    lax.fori_loop(0, nc // 2, first_half, 0)
    lax.fori_loop(nc // 2, nc, second_half, 0)
    for j in range(HEADS_PER_GROUP):
        hf_ref[0, 0, j] = hst_s[0, j]
        hb_ref[0, 0, j] = hst_s[1, j]
    if not emit_y:
        y_ref[...] = jnp.zeros(y_ref.shape, y_ref.dtype)


def _ssd_mixer(xbc, dt, z, dtb, alog, dskip, h0f, h0b, emit_y):
    b, l, _ = xbc.shape
    gw = HEADS_PER_GROUP * HEADDIM
    nxb = (N_GROUPS * gw) // D_STATE
    y_rows = l if emit_y else 8
    nc = l // CHUNK
    st_shape = (b, N_GROUPS, HEADS_PER_GROUP, D_STATE, HEADDIM)
    st_spec = pl.BlockSpec((1, 1, HEADS_PER_GROUP, D_STATE, HEADDIM), lambda bi, g: (bi, g, 0, 0, 0))
    in_specs = [
        pl.BlockSpec((1, l, gw), lambda bi, g: (bi, 0, g)),
        pl.BlockSpec((1, l, D_STATE), lambda bi, g: (bi, 0, nxb + g)),
        pl.BlockSpec((1, l, D_STATE), lambda bi, g: (bi, 0, nxb + N_GROUPS + g)),
        pl.BlockSpec((1, 1, l, 8), lambda bi, g: (bi, g, 0, 0)),
        pl.BlockSpec((1, y_rows, gw), lambda bi, g: (bi, 0, g)),
        pl.BlockSpec((1, 1, 8), lambda bi, g: (g, 0, 0)),
        pl.BlockSpec((1, 1, 8), lambda bi, g: (g, 0, 0)),
        pl.BlockSpec((1, gw), lambda bi, g: (0, g)),
        st_spec, st_spec,
    ]
    out_specs = [pl.BlockSpec((1, y_rows, gw), lambda bi, g: (bi, 0, g)), st_spec, st_spec]
    out_shape = [jax.ShapeDtypeStruct((b, y_rows, N_GROUPS * gw), BF16),
                 jax.ShapeDtypeStruct(st_shape, F32), jax.ShapeDtypeStruct(st_shape, F32)]
    scratch = [pltpu.VMEM((l, 8), F32), pltpu.VMEM((l, 8), F32),
               pltpu.VMEM((nc, 8, CHUNK), F32),
               pltpu.VMEM((nc, D_STATE, CHUNK), F32), pltpu.VMEM((nc, CHUNK, CHUNK), F32),
               pltpu.VMEM((y_rows, gw), F32),
               pltpu.VMEM((2, HEADS_PER_GROUP, D_STATE, HEADDIM), F32)]
    return pl.pallas_call(
        functools.partial(_ssd_kernel, seq=l, emit_y=emit_y),
        grid=(b, N_GROUPS),
        in_specs=in_specs, out_specs=out_specs, out_shape=out_shape,
        scratch_shapes=scratch,
        compiler_params=_params("parallel", "arbitrary"),
        name="ssd_mixer" if emit_y else "ssd_mixer_ctx",
    )(xbc, xbc, xbc, dt, z, dtb, alog, dskip, h0f, h0b)


def _fourier_kernel(f_ref, cc_ref, sc_ref, w_ref, ca_ref, sa_ref, cb_ref, sb_ref, o_ref,
                    u_s, v_s, cl_s, sl_s, *, seq, rows):
    @pl.when(pl.program_id(1) == 0)
    def _():
        scale = 1.0 / math.sqrt(seq * FOURIER_CH)
        for g in range(N_FOURIER_GROUPS):
            w = w_ref[g]
            a = jnp.dot(cc_ref[...], w, preferred_element_type=F32, precision=HIGHEST) * scale
            bm = jnp.dot(sc_ref[...], w, preferred_element_type=F32, precision=HIGHEST) * scale
            cols = slice(FOURIER_CH * g, FOURIER_CH * (g + 1))
            fg = f_ref[0, :, cols]
            u_s[:, cols] = jnp.dot(fg, a.astype(BF16), preferred_element_type=F32).astype(BF16)
            v_s[:, cols] = jnp.dot(fg, bm.astype(BF16), preferred_element_type=F32).astype(BF16)

    k2_0 = pl.program_id(1) * (rows // DFT_SPLIT)
    cbeta = cb_ref[...]
    sbeta = sb_ref[...]
    for j in range(rows // DFT_SPLIT):
        calpha = ca_ref[pl.ds(k2_0 + j, 1), :]
        salpha = sa_ref[pl.ds(k2_0 + j, 1), :]
        sub = slice(DFT_SPLIT * j, DFT_SPLIT * (j + 1))
        cl_s[sub, :] = (calpha * cbeta - salpha * sbeta).astype(BF16)
        sl_s[sub, :] = (salpha * cbeta + calpha * sbeta).astype(BF16)
    out = (jnp.dot(cl_s[...], u_s[...], preferred_element_type=F32)
           - jnp.dot(sl_s[...], v_s[...], preferred_element_type=F32))
    o_ref[0] = out.astype(BF16)


def _dft_tables(n, dtype):
    k = lax.broadcasted_iota(jnp.int32, (n, n), 0)
    l = lax.broadcasted_iota(jnp.int32, (n, n), 1)
    ang = ((k * l) % n).astype(F32) * (2.0 * math.pi / n)
    return jnp.cos(ang).astype(dtype), jnp.sin(ang).astype(dtype)


def _dft_factor_tables(n):
    n2 = n // DFT_SPLIT
    k2 = lax.broadcasted_iota(jnp.int32, (n2, n), 0)
    k1 = lax.broadcasted_iota(jnp.int32, (DFT_SPLIT, n), 0)
    alpha = ((k2 * lax.broadcasted_iota(jnp.int32, (n2, n), 1)) % n2).astype(F32) * (2.0 * math.pi / n2)
    beta = ((k1 * lax.broadcasted_iota(jnp.int32, (DFT_SPLIT, n), 1)) % n).astype(F32) * (2.0 * math.pi / n)
    return jnp.cos(alpha), jnp.sin(alpha), jnp.cos(beta), jnp.sin(beta)


def _fourier_mixer(f, w_four):
    b, l, df = f.shape
    tr = min(512, l)
    cc, sc = _dft_tables(FOURIER_CH, F32)
    ca, sa, cb, sb = _dft_factor_tables(l)
    const = lambda bi, i: (0, 0)
    return pl.pallas_call(
        functools.partial(_fourier_kernel, seq=l, rows=tr),
        grid=(b, l // tr),
        in_specs=[pl.BlockSpec((1, l, df), lambda bi, i: (bi, 0, 0)),
                  pl.BlockSpec((FOURIER_CH, FOURIER_CH), const),
                  pl.BlockSpec((FOURIER_CH, FOURIER_CH), const),
                  pl.BlockSpec(w_four.shape, lambda bi, i: (0, 0, 0)),
                  pl.BlockSpec(ca.shape, const), pl.BlockSpec(sa.shape, const),
                  pl.BlockSpec(cb.shape, const), pl.BlockSpec(sb.shape, const)],
        out_specs=pl.BlockSpec((1, tr, df), lambda bi, i: (bi, i, 0)),
        out_shape=jax.ShapeDtypeStruct((b, l, df), BF16),
        scratch_shapes=[pltpu.VMEM((l, df), BF16), pltpu.VMEM((l, df), BF16),
                        pltpu.VMEM((tr, l), BF16), pltpu.VMEM((tr, l), BF16)],
        compiler_params=_params("parallel", "arbitrary"),
        name="fourier_mixer",
    )(f, cc, sc, w_four, ca, sa, cb, sb)


def _outproj_router_kernel(y_ref, four_ref, x_ref, g1_ref, sh_ref, sc_ref, nssd_ref, n2_ref,
                           wos_ref, wof_ref, wr_ref, br_ref,
                           x1_ref, h_ref, eid_ref, gate_ref):
    y = y_ref[0].astype(F32)
    ms = jnp.mean(y * y, axis=-1, keepdims=True)
    yn = (y * lax.rsqrt(ms + EPS) * nssd_ref[...]).astype(BF16)
    mix = (jnp.dot(yn, wos_ref[...], preferred_element_type=F32)
           + jnp.dot(four_ref[0], wof_ref[...], preferred_element_type=F32))
    x1 = x_ref[0] + g1_ref[0] * mix
    x1_ref[0] = x1
    ms2 = jnp.mean(x1 * x1, axis=-1, keepdims=True)
    h = (x1 * lax.rsqrt(ms2 + EPS) * n2_ref[...]) * (1.0 + sc_ref[0]) + sh_ref[0]
    _store_token_tiles(h_ref, (0,), h)

    lg = _dot_split3(h, wr_ref[...]) + br_ref[...]
    tm = lg.shape[0]
    lane = lax.broadcasted_iota(jnp.int32, (tm, ROUTE_LANES), 1)
    lane_f = lane.astype(F32)
    neg = jnp.float32(-1e30)
    big = jnp.float32(1e9)
    is_grp = lane < N_EXPERT_GROUPS
    gl = jnp.where(is_grp, lg, neg)
    gmax = jnp.max(gl, axis=-1, keepdims=True)
    gsum = jnp.sum(jnp.where(is_grp, jnp.exp(gl - gmax), 0.0), axis=-1, keepdims=True)
    grp = jnp.min(jnp.where(gl == gmax, lane_f, big), axis=-1, keepdims=True)
    p_grp = 1.0 / gsum
    lo = N_EXPERT_GROUPS + EXPERTS_PER_GROUP * grp
    in_grp = jnp.logical_and(lane_f >= lo, lane_f < lo + EXPERTS_PER_GROUP)
    el = jnp.where(in_grp, lg, neg)
    m1 = jnp.max(el, axis=-1, keepdims=True)
    i1 = jnp.min(jnp.where(el == m1, lane_f, big), axis=-1, keepdims=True)
    el2 = jnp.where(lane_f == i1, neg, el)
    m2 = jnp.max(el2, axis=-1, keepdims=True)
    i2 = jnp.min(jnp.where(el2 == m2, lane_f, big), axis=-1, keepdims=True)
    e2 = jnp.exp(m2 - m1)
    den = 1.0 + e2
    gate1 = p_grp / den
    gate2 = p_grp * e2 / den
    lane8 = lax.broadcasted_iota(jnp.int32, (tm, 8), 1)
    eid = jnp.where(lane8 == 0, i1 - N_EXPERT_GROUPS, jnp.where(lane8 == 1, i2 - N_EXPERT_GROUPS, 0.0))
    eid_ref[0] = eid.astype(jnp.int32)
    gate_ref[0] = jnp.where(lane8 == 0, gate1, jnp.where(lane8 == 1, gate2, 0.0))


def _outproj_router(y, four, x, g1, sh2, sc2, nssd, n2, wos, wof, wr, br):
    b, l, d = x.shape
    tm = min(512, l)
    row = lambda bi, i: (bi, i, 0)
    vec = lambda bi, i: (bi, 0, 0)
    const = lambda bi, i: (0, 0)
    return pl.pallas_call(
        _outproj_router_kernel,
        grid=(b, l // tm),
        in_specs=[pl.BlockSpec((1, tm, y.shape[2]), row),
                  pl.BlockSpec((1, tm, four.shape[2]), row),
                  pl.BlockSpec((1, tm, d), row),
                  pl.BlockSpec((1, 1, d), vec), pl.BlockSpec((1, 1, d), vec), pl.BlockSpec((1, 1, d), vec),
                  pl.BlockSpec((1, y.shape[2]), const), pl.BlockSpec((1, d), const),
                  pl.BlockSpec(wos.shape, const), pl.BlockSpec(wof.shape, const),
                  pl.BlockSpec(wr.shape, const), pl.BlockSpec(br.shape, const)],
        out_specs=[pl.BlockSpec((1, tm, d), row), pl.BlockSpec((1, tm * TILE_ROWS, LANES), row),
                   pl.BlockSpec((1, tm, 8), row), pl.BlockSpec((1, tm, 8), row)],
        out_shape=[jax.ShapeDtypeStruct((b, l, d), F32), jax.ShapeDtypeStruct((b, l * TILE_ROWS, LANES), F32),
                   jax.ShapeDtypeStruct((b, l, 8), jnp.int32), jax.ShapeDtypeStruct((b, l, 8), F32)],
        compiler_params=_params("parallel", "arbitrary"),
        name="outproj_router",
    )(y, four, x, g1, sh2, sc2, nssd, n2, wos, wof, wr, br)


def _expert_kernel(bexp_ref, nused_ref, tokc_ref, tokn_ref, dstp_ref, dstc_ref, h_hbm, wg_ref, wu_ref, wd_ref,
                   y_hbm, hbuf, ybuf, act_s, wg_s, wu_s, wd_s, sem_in, sem_out):
    i = pl.program_id(0)
    n_used = nused_ref[0]
    active = i < n_used
    is_last = i == n_used - 1
    slot = i % 2
    other = 1 - slot
    de = wg_s.shape[1]
    d = wd_s.shape[1]
    n_phase = 8
    rows_per_phase = MOE_ROWS // n_phase
    blk_tile_rows = MOE_ROWS * TILE_ROWS

    def tile(ref, lead, row0):
        return ref.at[lead + (pl.ds(row0, TILE_ROWS), slice(None))]

    def start_gather(tok_ref, s, r):
        src = tile(h_hbm, (), pl.multiple_of(tok_ref[0, 0, r], TILE_ROWS))
        pltpu.make_async_copy(src, tile(hbuf, (s,), r * TILE_ROWS), sem_in.at[s]).start(priority=0)

    def start_scatter(dst_ref, s, r):
        dst = tile(y_hbm, (), pl.multiple_of(dst_ref[0, 0, r], TILE_ROWS))
        pltpu.make_async_copy(tile(ybuf, (s,), r * TILE_ROWS), dst, sem_out.at[s]).start(priority=1)

    def wait_gather(s):
        pltpu.make_async_copy(h_hbm.at[pl.ds(0, blk_tile_rows), :], hbuf.at[s], sem_in.at[s]).wait()

    def wait_scatter(s):
        pltpu.make_async_copy(ybuf.at[s], y_hbm.at[pl.ds(0, blk_tile_rows), :], sem_out.at[s]).wait()

    @pl.when(i == 0)
    def _():
        for r in range(MOE_ROWS):
            start_gather(tokc_ref, 0, r)

    prev = bexp_ref[jnp.maximum(i - 1, 0)]

    @pl.when(jnp.logical_and(active, jnp.logical_or(i == 0, bexp_ref[i] != prev)))
    def _():
        wg_s[...] = wg_ref[0].astype(BF16)
        wu_s[...] = wu_ref[0].astype(BF16)
        wd_s[...] = wd_ref[0].astype(BF16)

    @pl.when(active)
    def _():
        wait_gather(slot)

    @pl.when(jnp.logical_and(active, i >= 2))
    def _():
        wait_scatter(slot)

    def step(with_scatter, cur):
        nxt = 1 - cur
        xb = _load_token_tiles(hbuf, (cur,), MOE_ROWS).astype(BF16)
        ys = []
        for p in range(n_phase):
            for r in range(p * rows_per_phase, (p + 1) * rows_per_phase):
                start_gather(tokn_ref, nxt, r)
                if with_scatter:
                    start_scatter(dstp_ref, nxt, r)
            if p < n_phase // 2:
                w = de // (n_phase // 2)
                cols = slice(p * w, (p + 1) * w)
                gact = jnp.dot(xb, wg_s[:, cols], preferred_element_type=F32)
                up = jnp.dot(xb, wu_s[:, cols], preferred_element_type=F32)
                act_s[:, cols] = (_silu(gact) * up).astype(BF16)
            else:
                w = d // (n_phase // 2)
                cols = slice((p - n_phase // 2) * w, (p - n_phase // 2 + 1) * w)
                ys.append(jnp.dot(act_s[...], wd_s[:, cols], preferred_element_type=F32))
        _store_token_tiles(ybuf, (cur,), jnp.concatenate(ys, axis=1))

    @pl.when(i == 0)
    def _():
        step(False, 0)

    @pl.when(jnp.logical_and(active, jnp.logical_and(i > 0, slot == 0)))
    def _():
        step(True, 0)

    @pl.when(jnp.logical_and(active, slot == 1))
    def _():
        step(True, 1)

    @pl.when(is_last)
    def _():
        for r in range(MOE_ROWS):
            start_scatter(dstc_ref, slot, r)
        wait_gather(other)
        wait_scatter(other)
        wait_scatter(slot)

    @pl.when(jnp.logical_not(active))
    def _():
        ybuf[0] = jnp.zeros((blk_tile_rows, LANES), F32)
        dst0 = pl.multiple_of(dstc_ref[0, 0, 0], TILE_ROWS)
        fill = pltpu.make_async_copy(ybuf.at[0], y_hbm.at[pl.ds(dst0, blk_tile_rows), :], sem_out.at[0])
        fill.start()
        fill.wait()


def _experts(h_tiles, row_tok, row_dst, blk_exp, n_used, w_eg, w_eu, w_ed):
    n_blocks = blk_exp.shape[0]
    d, de = w_eg.shape[1], w_eg.shape[2]
    assert d == TILE_ROWS * LANES
    idx_shape = (n_blocks, 1, MOE_ROWS)
    idx_block = (1, 1, MOE_ROWS)
    smem = pltpu.SMEM
    blk_tile_rows = MOE_ROWS * TILE_ROWS
    grid_spec = pltpu.PrefetchScalarGridSpec(
        num_scalar_prefetch=2,
        grid=(n_blocks,),
        in_specs=[pl.BlockSpec(idx_block, lambda i, be, nu: (i, 0, 0), memory_space=smem),
                  pl.BlockSpec(idx_block, lambda i, be, nu: (jnp.minimum(i + 1, n_blocks - 1), 0, 0),
                               memory_space=smem),
                  pl.BlockSpec(idx_block, lambda i, be, nu: (jnp.maximum(i - 1, 0), 0, 0), memory_space=smem),
                  pl.BlockSpec(idx_block, lambda i, be, nu: (i, 0, 0), memory_space=smem),
                  pl.BlockSpec(memory_space=pl.ANY),
                  pl.BlockSpec((1, d, de), lambda i, be, nu: (be[i], 0, 0)),
                  pl.BlockSpec((1, d, de), lambda i, be, nu: (be[i], 0, 0)),
                  pl.BlockSpec((1, de, d), lambda i, be, nu: (be[i], 0, 0))],
        out_specs=pl.BlockSpec(memory_space=pl.ANY),
        scratch_shapes=[pltpu.VMEM((2, blk_tile_rows, LANES), F32), pltpu.VMEM((2, blk_tile_rows, LANES), F32),
                        pltpu.VMEM((MOE_ROWS, de), BF16),
                        pltpu.VMEM((d, de), BF16), pltpu.VMEM((d, de), BF16), pltpu.VMEM((de, d), BF16),
                        pltpu.SemaphoreType.DMA((2,)), pltpu.SemaphoreType.DMA((2,))],
    )
    row_tok = (row_tok * TILE_ROWS).reshape(idx_shape)
    row_dst = (row_dst * TILE_ROWS).reshape(idx_shape)
    return pl.pallas_call(
        _expert_kernel,
        grid_spec=grid_spec,
        out_shape=jax.ShapeDtypeStruct((n_blocks * blk_tile_rows, LANES), F32),
        compiler_params=_params("arbitrary"),
        name="moe_experts",
    )(blk_exp, n_used, row_tok, row_tok, row_dst, row_dst, h_tiles, w_eg, w_eu, w_ed)


def _route_tables(eid, n_tok):
    n_assign = 2 * n_tok
    e_flat = jnp.concatenate([eid[:, 0], eid[:, 1]])
    order = jnp.argsort(e_flat).astype(jnp.int32)
    experts = jnp.arange(N_EXPERTS, dtype=jnp.int32)
    counts = jnp.sum((e_flat[:, None] == experts[None, :]).astype(jnp.int32), axis=0)
    start = jnp.cumsum(counts) - counts
    padded = (counts + MOE_ROWS - 1) // MOE_ROWS * MOE_ROWS
    end_pad = jnp.cumsum(padded)
    start_pad = end_pad - padded
    n_blocks = -(-(n_assign + N_EXPERTS * (MOE_ROWS - 1)) // MOE_ROWS)
    blk_row0 = jnp.arange(n_blocks, dtype=jnp.int32) * MOE_ROWS
    blk_exp = jnp.minimum(jnp.sum((end_pad[None, :] <= blk_row0[:, None]).astype(jnp.int32), axis=1),
                          N_EXPERTS - 1).astype(jnp.int32)
    j = (blk_row0 - start_pad[blk_exp])[:, None] + jnp.arange(MOE_ROWS, dtype=jnp.int32)[None, :]
    valid = j < counts[blk_exp][:, None]
    src = jnp.clip(start[blk_exp][:, None] + j, 0, n_assign - 1)
    assign = order[src.reshape(-1)].reshape(src.shape)
    row_tok = jnp.where(valid, jnp.where(assign >= n_tok, assign - n_tok, assign), 0)
    spare = n_assign + jnp.cumsum((~valid).reshape(-1).astype(jnp.int32)).reshape(valid.shape) - 1
    row_dst = jnp.where(valid, assign, spare)
    n_used = (end_pad[-1:] // MOE_ROWS).astype(jnp.int32)
    return row_tok.astype(jnp.int32), row_dst.astype(jnp.int32), blk_exp, n_used


def _combine_kernel(x1_ref, y0_ref, y1_ref, gate_ref, g2_ref, nw_ref, o_ref):
    gate = gate_ref[0]
    tm = gate.shape[0]
    moe = (gate[:, 0:1] * _load_token_tiles(y0_ref, (), tm) + gate[:, 1:2] * _load_token_tiles(y1_ref, (), tm))
    x2 = x1_ref[0] + g2_ref[0] * moe
    ms = jnp.mean(x2 * x2, axis=-1, keepdims=True)
    o_ref[0] = x2 * lax.rsqrt(ms + EPS) * nw_ref[...]


def _combine(x1, y_assign, gate, g2, final_norm):
    b, l, d = x1.shape
    tm = min(512, l)
    nt = l // tm
    row = lambda bi, i: (bi, i, 0)
    return pl.pallas_call(
        _combine_kernel,
        grid=(b, nt),
        in_specs=[pl.BlockSpec((1, tm, d), row),
                  pl.BlockSpec((tm * TILE_ROWS, LANES), lambda bi, i: (bi * nt + i, 0)),
                  pl.BlockSpec((tm * TILE_ROWS, LANES), lambda bi, i: (b * nt + bi * nt + i, 0)),
                  pl.BlockSpec((1, tm, 8), row),
                  pl.BlockSpec((1, 1, d), lambda bi, i: (bi, 0, 0)),
                  pl.BlockSpec((1, d), lambda bi, i: (0, 0))],
        out_specs=pl.BlockSpec((1, tm, d), row),
        out_shape=jax.ShapeDtypeStruct((b, l, d), F32),
        compiler_params=_params("parallel", "arbitrary"),
        name="moe_combine",
    )(x1, y_assign, y_assign, gate, g2, final_norm.reshape(1, d))


def _group_major(v):
    return v.reshape(2, N_GROUPS, HEADS_PER_GROUP).transpose(1, 0, 2).reshape(N_GROUPS, 1, 2 * HEADS_PER_GROUP)


def kernel(x, c, ctx, c_ctx, w_mod, b_mod, norm1, w_in, conv_w, conv_b, dt_bias, a_log, d_skip, ssd_norm,
           w_four, w_out, norm2, w_rg, b_rg, w_re, b_re, w_eg, w_eu, w_ed, final_norm):
    bsz, seq, d = x.shape
    n_tok = bsz * seq
    d_ssd = N_GROUPS * HEADS_PER_GROUP * HEADDIM
    conv_dim = d_ssd + 2 * N_GROUPS * D_STATE
    n_heads = N_GROUPS * HEADS_PER_GROUP
    layer = 0

    c_rows = jnp.zeros((16, d), F32).at[:bsz].set(c).at[bsz].set(c_ctx)
    mod = _modulation(c_rows, w_mod[layer], b_mod[layer])
    sh1, sc1, g1, sh2, sc2, g2 = [m[:bsz, None, :] for m in jnp.split(mod, 6, axis=-1)]
    sh1c, sc1c = [jnp.broadcast_to(m[bsz][None, None, :], (bsz, 1, d)) for m in jnp.split(mod, 6, axis=-1)[:2]]

    w = w_in[layer]
    wz = w[:, :d_ssd].astype(BF16)
    wx = w[:, d_ssd:d_ssd + conv_dim].astype(BF16)
    wdt = w[:, d_ssd + conv_dim:d_ssd + conv_dim + 2 * n_heads]
    wdt = wdt.reshape(d, 2, N_GROUPS, HEADS_PER_GROUP).transpose(0, 2, 1, 3).reshape(d, 2 * n_heads)
    wdt = jnp.pad(wdt, ((0, 0), (0, 128 - 2 * n_heads))).astype(BF16)
    wf = w[:, d_ssd + conv_dim + 2 * n_heads:].astype(BF16)
    n1 = norm1[layer].reshape(1, d)

    dtb = _group_major(dt_bias[layer])
    alog = _group_major(a_log[layer])
    dsk = jnp.repeat(d_skip[layer], HEADDIM).reshape(1, d_ssd)
    cw = conv_w[layer]
    cb = conv_b[layer].reshape(1, conv_dim)

    xbc_c, dt_c = _in_projection(ctx, sh1c, sc1c, n1, wx, wdt, cw, cb)
    h_zero = jnp.zeros((bsz, N_GROUPS, HEADS_PER_GROUP, D_STATE, HEADDIM), F32)
    z_dummy = jnp.zeros((bsz, 8, d_ssd), BF16)
    _, hf_c, hb_c = _ssd_mixer(xbc_c, dt_c, z_dummy, dtb, alog, dsk, h_zero, h_zero, emit_y=False)

    xbc_l, dt_l, z_l, f_l = _in_projection(x, sh1, sc1, n1, wx, wdt, cw, cb, wz, wf)
    y_l, _, _ = _ssd_mixer(xbc_l, dt_l, z_l, dtb, alog, dsk, hf_c, hb_c, emit_y=True)
    four = _fourier_mixer(f_l, w_four[layer])

    wo = w_out[layer]
    wr = jnp.concatenate([w_rg[layer], w_re[layer].transpose(1, 0, 2).reshape(d, N_EXPERTS)], axis=1)
    wr = jnp.pad(wr, ((0, 0), (0, ROUTE_LANES - wr.shape[1])))
    br = jnp.pad(jnp.concatenate([b_rg[layer], b_re[layer].reshape(-1)]),
                 (0, ROUTE_LANES - N_EXPERT_GROUPS - N_EXPERTS)).reshape(1, ROUTE_LANES)
    x1, h, eid, gate = _outproj_router(
        y_l, four, x, g1, sh2, sc2, ssd_norm[layer].reshape(1, d_ssd), norm2[layer].reshape(1, d),
        wo[:d_ssd].astype(BF16), wo[d_ssd:].astype(BF16), wr, br)

    row_tok, row_dst, blk_exp, n_used = _route_tables(eid.reshape(n_tok, 8), n_tok)
    y_assign = _experts(h.reshape(n_tok * TILE_ROWS, LANES), row_tok, row_dst, blk_exp, n_used,
                        w_eg[layer], w_eu[layer], w_ed[layer])
    return _combine(x1, y_assign, gate, g2, final_norm)
```

```python
import functools
import math

import jax
import jax.numpy as jnp
from jax import lax
from jax.experimental import pallas as pl
from jax.experimental.pallas import tpu as pltpu

F32 = jnp.float32
BF16 = jnp.bfloat16
HIGHEST = lax.Precision.HIGHEST

EPS = 1e-6
LOG2E = 1.4426950408889634
CHUNK = 128
N_GROUPS = 4
HEADS_PER_GROUP = 4
HEADDIM = 64
D_STATE = 128
CONV_W = 5
CONV_HALO = 16
N_FOURIER_GROUPS = 4
DFT_SPLIT = 64
DFT_EXTRA = 16
FOURIER_CH = 128
N_EXPERT_GROUPS = 4
EXPERTS_PER_GROUP = 8
N_EXPERTS = N_EXPERT_GROUPS * EXPERTS_PER_GROUP
ROUTE_LANES = 128
MOE_ROWS = 256
VMEM_LIMIT_BYTES = 56 * 1024 * 1024


def _params(*sem):
    return pltpu.CompilerParams(dimension_semantics=sem, vmem_limit_bytes=VMEM_LIMIT_BYTES)


def _silu(v):
    h = 0.5 * v
    return h + h * jnp.tanh(h)


LANES = 128
TILE_ROWS = 8


def _store_token_tiles(ref, lead, val):
    rows = val.shape[0]
    for j in range(TILE_ROWS):
        ref[lead + (pl.ds(j, rows, stride=TILE_ROWS), slice(None))] = val[:, LANES * j:LANES * (j + 1)]


def _load_token_tiles(ref, lead, rows):
    return jnp.concatenate(
        [ref[lead + (pl.ds(j, rows, stride=TILE_ROWS), slice(None))] for j in range(TILE_ROWS)], axis=1)


def _dot_split3(a, b):
    a_hi = a.astype(BF16)
    a_lo = (a - a_hi.astype(F32)).astype(BF16)
    b_hi = b.astype(BF16)
    b_lo = (b - b_hi.astype(F32)).astype(BF16)
    dot = functools.partial(jnp.dot, preferred_element_type=F32)
    return dot(a_hi, b_hi) + (dot(a_lo, b_hi) + dot(a_hi, b_lo))


def _mod_kernel(c_ref, w_ref, b_ref, o_ref):
    s = _silu(c_ref[...])
    o_ref[...] = jnp.dot(s, w_ref[...], preferred_element_type=F32, precision=HIGHEST) + b_ref[...]


def _modulation(c_rows, w_mod, b_mod):
    rows, d = c_rows.shape
    n = w_mod.shape[1]
    tn = 512
    return pl.pallas_call(
        _mod_kernel,
        grid=(n // tn,),
        in_specs=[pl.BlockSpec((rows, d), lambda j: (0, 0)),
                  pl.BlockSpec((d, tn), lambda j: (0, j)),
                  pl.BlockSpec((1, tn), lambda j: (0, j))],
        out_specs=pl.BlockSpec((rows, tn), lambda j: (0, j)),
        out_shape=jax.ShapeDtypeStruct((rows, n), F32),
        compiler_params=_params("arbitrary"),
        name="modulation",
    )(c_rows, w_mod, b_mod.reshape(1, n))


def _inproj_kernel(xp_ref, x_ref, xn_ref, sh_ref, sc_ref, nw_ref, wx_ref, wdt_ref, cw_ref, cb_ref, *rest,
                   with_zf):
    if with_zf:
        wz_ref, wf_ref, xbc_ref, dt_ref, z_ref, f_ref, pre_s = rest
    else:
        xbc_ref, dt_ref, pre_s = rest
    tm = x_ref.shape[1]
    i = pl.program_id(1)
    xa = jnp.concatenate([xp_ref[0], x_ref[0], xn_ref[0]], axis=0)
    ms = jnp.mean(xa * xa, axis=-1, keepdims=True)
    xnorm = xa * lax.rsqrt(ms + EPS) * nw_ref[...]
    xm_all = (xnorm * (1.0 + sc_ref[0]) + sh_ref[0]).astype(BF16)
    xm = xm_all[CONV_HALO:CONV_HALO + tm]

    pre_s[...] = jnp.dot(xm_all, wx_ref[...], preferred_element_type=F32)
    top, bot = slice(0, CONV_HALO), slice(CONV_HALO + tm, 2 * CONV_HALO + tm)
    pre_s[top, :] = pre_s[top, :] * (i > 0).astype(F32)
    pre_s[bot, :] = pre_s[bot, :] * (i < pl.num_programs(1) - 1).astype(F32)

    rblk, cblk = 64, 256
    for c0 in range(0, wx_ref.shape[1], cblk):
        cols = slice(c0, c0 + cblk)
        w = cw_ref[:, cols]
        bias = jnp.broadcast_to(cb_ref[:, cols], (rblk, cblk))
        for r0 in range(0, tm, rblk):
            lo = r0 + CONV_HALO - 8
            win = pre_s[lo:lo + rblk + 16, cols]
            acc = bias
            for k in range(CONV_W):
                off = 8 - CONV_W // 2 + k
                acc = acc + w[k:k + 1, :] * win[off:off + rblk, :]
            xbc_ref[0, r0:r0 + rblk, cols] = _silu(acc).astype(BF16)

    dt = jnp.dot(xm, wdt_ref[...], preferred_element_type=F32)
    for g in range(N_GROUPS):
        dt_ref[0, g] = dt[:, 8 * g:8 * g + 8]
    if with_zf:
        z_ref[0] = jnp.dot(xm, wz_ref[...], preferred_element_type=F32).astype(BF16)
        f_ref[0] = jnp.dot(xm, wf_ref[...], preferred_element_type=F32).astype(BF16)


def _in_projection(x, shift, scale, norm_w, wx, wdt, conv_w, conv_b, wz=None, wf=None):
    b, l, d = x.shape
    tm = min(512, l)
    with_zf = wz is not None
    hb = tm // CONV_HALO
    n_hb = l // CONV_HALO
    row = lambda bi, i: (bi, i, 0)
    vec = lambda bi, i: (bi, 0, 0)
    const = lambda bi, i: (0, 0)
    in_specs = [pl.BlockSpec((1, CONV_HALO, d), lambda bi, i: (bi, jnp.maximum(i * hb - 1, 0), 0)),
                pl.BlockSpec((1, tm, d), row),
                pl.BlockSpec((1, CONV_HALO, d), lambda bi, i: (bi, jnp.minimum((i + 1) * hb, n_hb - 1), 0)),
                pl.BlockSpec((1, 1, d), vec),
                pl.BlockSpec((1, 1, d), vec),
                pl.BlockSpec((1, d), const),
                pl.BlockSpec(wx.shape, const),
                pl.BlockSpec(wdt.shape, const),
                pl.BlockSpec(conv_w.shape, const),
                pl.BlockSpec(conv_b.shape, const)]
    args = [x, x, x, shift, scale, norm_w, wx, wdt, conv_w, conv_b]
    out_specs = [pl.BlockSpec((1, tm, wx.shape[1]), row),
                 pl.BlockSpec((1, N_GROUPS, tm, 8), lambda bi, i: (bi, 0, i, 0))]
    out_shape = [jax.ShapeDtypeStruct((b, l, wx.shape[1]), BF16),
                 jax.ShapeDtypeStruct((b, N_GROUPS, l, 8), F32)]
    if with_zf:
        in_specs += [pl.BlockSpec(wz.shape, const), pl.BlockSpec(wf.shape, const)]
        args += [wz, wf]
        out_specs += [pl.BlockSpec((1, tm, wz.shape[1]), row), pl.BlockSpec((1, tm, wf.shape[1]), row)]
        out_shape += [jax.ShapeDtypeStruct((b, l, wz.shape[1]), BF16),
                      jax.ShapeDtypeStruct((b, l, wf.shape[1]), BF16)]
    return pl.pallas_call(
        functools.partial(_inproj_kernel, with_zf=with_zf),
        grid=(b, l // tm),
        in_specs=in_specs,
        out_specs=out_specs,
        out_shape=out_shape,
        scratch_shapes=[pltpu.VMEM((tm + 2 * CONV_HALO, wx.shape[1]), F32)],
        compiler_params=_params("parallel", "arbitrary"),
        name="in_projection_zf" if with_zf else "in_projection",
    )(*args)


def _ssd_kernel(xs_ref, xb_ref, xc_ref, dt_ref, z_ref,
                dtb_ref, alog_ref, dsk_ref, h0f_ref, h0b_ref,
                y_ref, hf_ref, hb_ref,
                dt_s, cs_s, cst_s, bt_s, cb_s, yf_s, hst_s, *, seq, emit_y):
    nc = seq // CHUNK

    dtr = dt_ref[0, 0] + dtb_ref[0]
    dtv = jnp.maximum(dtr, 0.0) + jnp.log(1.0 + jnp.exp(-jnp.abs(dtr)))
    dt_s[...] = dtv
    a_log2 = -jnp.exp(alog_ref[0]) * LOG2E

    ri = lax.broadcasted_iota(jnp.int32, (CHUNK, CHUNK), 0)
    ci = lax.broadcasted_iota(jnp.int32, (CHUNK, CHUNK), 1)
    tri_fwd = ci <= ri
    tri_bwd = ci >= ri
    is_fwd_lane8 = lax.broadcasted_iota(jnp.int32, (CHUNK, 8), 1) < HEADS_PER_GROUP
    lane_pad = jnp.zeros((CHUNK, CHUNK - 8), F32)

    tri16 = tri_fwd.astype(BF16)

    def tables(c, carry):
        r0 = pl.multiple_of(c * CHUNK, CHUNK)
        rows = pl.ds(r0, CHUNK)
        da = dt_s[rows, :] * a_log2
        p0 = da.astype(BF16)
        r1 = da - p0.astype(F32)
        p1 = r1.astype(BF16)
        p2 = (r1 - p1.astype(F32)).astype(BF16)
        packed = jnp.concatenate([p0, p1, p2, jnp.zeros((CHUNK, CHUNK - 24), BF16)], axis=1)
        acc = jnp.dot(tri16, packed, preferred_element_type=F32)
        cs_f = acc[:, 0:8] + acc[:, 8:16] + acc[:, 16:24]
        cs_b = cs_f[CHUNK - 1:CHUNK, :] - cs_f + da
        cs = jnp.where(is_fwd_lane8, cs_f, cs_b)
        cs_s[rows, :] = cs
        cs_t = jnp.concatenate([cs, lane_pad], axis=1).T[:8, :]
        dt_t = jnp.concatenate([dt_s[rows, :], lane_pad], axis=1).T[:8, :]
        cst_s[c] = cs_t - jnp.log2(dt_t)
        bt = xb_ref[0, rows, :].astype(F32).T
        bt_s[c] = bt
        cb_s[c] = jnp.dot(xc_ref[0, rows, :], bt.astype(BF16), preferred_element_type=F32)
        return carry

    lax.fori_loop(0, nc, tables, 0, unroll=2)

    for j in range(HEADS_PER_GROUP):
        hst_s[0, j] = h0f_ref[0, 0, j]
        hst_s[1, j] = h0b_ref[0, 0, j]

    def chunk_dir(d, c, final):
        tri = tri_fwd if d == 0 else tri_bwd
        tot_row = CHUNK - 1 if d == 0 else 0
        r0 = pl.multiple_of(c * CHUNK, CHUNK)
        rows = pl.ds(r0, CHUNK)
        xs = xs_ref[0, rows, :]
        cc = xc_ref[0, rows, :].astype(F32)
        cs = cs_s[rows, :]
        cs_t = cst_s[c]
        bt = bt_s[c]
        cb = cb_s[c]
        ys = []
        for j in range(HEADS_PER_GROUP):
            lane = HEADS_PER_GROUP * d + j
            a_col1 = cs[:, lane:lane + 1]
            a_col = jnp.broadcast_to(a_col1, (CHUNK, CHUNK))
            a_tot = a_col1[tot_row:tot_row + 1, :]
            a_row = cs_t[lane:lane + 1, :]
            g = (cb * jnp.where(tri, jnp.exp2(a_col - a_row), 0.0)).astype(BF16)
            cea = (cc * jnp.exp2(a_col)).astype(BF16)
            xh = xs[:, HEADDIM * j:HEADDIM * (j + 1)]
            h = hst_s[d, j]
            y_h = (jnp.dot(g, xh, preferred_element_type=F32)
                   + jnp.dot(cea, h.astype(BF16), preferred_element_type=F32))
            w_row = jnp.exp2(a_tot - a_row)
            s_new = jnp.dot((bt * w_row).astype(BF16), xh, preferred_element_type=F32)
            hst_s[d, j] = h * jnp.exp2(a_tot) + s_new
            ys.append(y_h)
        if not emit_y:
            return
        y_c = jnp.concatenate(ys, axis=1)
        if d == 0:
            y_c = y_c + dsk_ref[...] * xs.astype(F32)
        if final:
            zc = z_ref[0, rows, :].astype(F32)
            y_ref[0, rows, :] = ((yf_s[rows, :] + y_c) * _silu(zc)).astype(BF16)
        else:
            yf_s[rows, :] = y_c

    def first_half(t, carry):
        chunk_dir(0, t, False)
        chunk_dir(1, nc - 1 - t, False)
        return carry

    def second_half(t, carry):
        chunk_dir(0, t, True)
        chunk_dir(1, nc - 1 - t, True)
        return carry

    lax.fori_loop(0, nc // 2, first_half, 0)
    lax.fori_loop(nc // 2, nc, second_half, 0)
    for j in range(HEADS_PER_GROUP):
        hf_ref[0, 0, j] = hst_s[0, j]
        hb_ref[0, 0, j] = hst_s[1, j]
    if not emit_y:
        y_ref[...] = jnp.zeros(y_ref.shape, y_ref.dtype)


def _ssd_mixer(xbc, dt, z, dtb, alog, dskip, h0f, h0b, emit_y):
    b, l, _ = xbc.shape
    gw = HEADS_PER_GROUP * HEADDIM
    nxb = (N_GROUPS * gw) // D_STATE
    y_rows = l if emit_y else 8
    nc = l // CHUNK
    st_shape = (b, N_GROUPS, HEADS_PER_GROUP, D_STATE, HEADDIM)
    st_spec = pl.BlockSpec((1, 1, HEADS_PER_GROUP, D_STATE, HEADDIM), lambda bi, g: (bi, g, 0, 0, 0))
    in_specs = [
        pl.BlockSpec((1, l, gw), lambda bi, g: (bi, 0, g)),
        pl.BlockSpec((1, l, D_STATE), lambda bi, g: (bi, 0, nxb + g)),
        pl.BlockSpec((1, l, D_STATE), lambda bi, g: (bi, 0, nxb + N_GROUPS + g)),
        pl.BlockSpec((1, 1, l, 8), lambda bi, g: (bi, g, 0, 0)),
        pl.BlockSpec((1, y_rows, gw), lambda bi, g: (bi, 0, g)),
        pl.BlockSpec((1, 1, 8), lambda bi, g: (g, 0, 0)),
        pl.BlockSpec((1, 1, 8), lambda bi, g: (g, 0, 0)),
        pl.BlockSpec((1, gw), lambda bi, g: (0, g)),
        st_spec, st_spec,
    ]
    out_specs = [pl.BlockSpec((1, y_rows, gw), lambda bi, g: (bi, 0, g)), st_spec, st_spec]
    out_shape = [jax.ShapeDtypeStruct((b, y_rows, N_GROUPS * gw), BF16),
                 jax.ShapeDtypeStruct(st_shape, F32), jax.ShapeDtypeStruct(st_shape, F32)]
    scratch = [pltpu.VMEM((l, 8), F32), pltpu.VMEM((l, 8), F32),
               pltpu.VMEM((nc, 8, CHUNK), F32),
               pltpu.VMEM((nc, D_STATE, CHUNK), F32), pltpu.VMEM((nc, CHUNK, CHUNK), F32),
               pltpu.VMEM((y_rows, gw), F32),
               pltpu.VMEM((2, HEADS_PER_GROUP, D_STATE, HEADDIM), F32)]
    return pl.pallas_call(
        functools.partial(_ssd_kernel, seq=l, emit_y=emit_y),
        grid=(b, N_GROUPS),
        in_specs=in_specs, out_specs=out_specs, out_shape=out_shape,
        scratch_shapes=scratch,
        compiler_params=_params("parallel", "arbitrary"),
        name="ssd_mixer" if emit_y else "ssd_mixer_ctx",
    )(xbc, xbc, xbc, dt, z, dtb, alog, dskip, h0f, h0b)


def _fourier_kernel(f_ref, cc_ref, sc_ref, w_ref, ca_ref, sa_ref, cb_ref, sb_ref, rev_ref, o_ref,
                    u_s, v_s, cl_s, sl_s, *, seq, rows):
    i = pl.program_id(1)
    n_blk = o_ref.shape[1]

    @pl.when(i == 0)
    def _():
        scale = 1.0 / math.sqrt(seq * FOURIER_CH)
        for g in range(N_FOURIER_GROUPS):
            w = w_ref[g]
            a = jnp.dot(cc_ref[...], w, preferred_element_type=F32, precision=HIGHEST) * scale
            bm = jnp.dot(sc_ref[...], w, preferred_element_type=F32, precision=HIGHEST) * scale
            cols = slice(FOURIER_CH * g, FOURIER_CH * (g + 1))
            fg = f_ref[0, :, cols]
            u_s[:, cols] = jnp.dot(fg, a.astype(BF16), preferred_element_type=F32).astype(BF16)
            v_s[:, cols] = jnp.dot(fg, bm.astype(BF16), preferred_element_type=F32).astype(BF16)

    k2_0 = i * (rows // DFT_SPLIT)
    cbeta = cb_ref[...]
    sbeta = sb_ref[...]
    for j in range(rows // DFT_SPLIT + 1):
        n_sub = DFT_SPLIT if j < rows // DFT_SPLIT else DFT_EXTRA
        calpha = ca_ref[pl.ds(k2_0 + j, 1), :]
        salpha = sa_ref[pl.ds(k2_0 + j, 1), :]
        sub = slice(DFT_SPLIT * j, DFT_SPLIT * j + n_sub)
        cl_s[sub, :] = (calpha * cbeta[:n_sub] - salpha * sbeta[:n_sub]).astype(BF16)
        sl_s[sub, :] = (salpha * cbeta[:n_sub] + calpha * sbeta[:n_sub]).astype(BF16)
    p = jnp.dot(cl_s[...], u_s[...], preferred_element_type=F32)
    q = jnp.dot(sl_s[...], v_s[...], preferred_element_type=F32)
    o_ref[0, i] = (p - q)[:rows].astype(BF16)
    o_ref[0, n_blk - 1 - i] = jnp.dot(rev_ref[...], (p + q).astype(BF16),
                                      preferred_element_type=F32).astype(BF16)


def _dft_tables(n, dtype):
    k = lax.broadcasted_iota(jnp.int32, (n, n), 0)
    l = lax.broadcasted_iota(jnp.int32, (n, n), 1)
    ang = ((k * l) % n).astype(F32) * (2.0 * math.pi / n)
    return jnp.cos(ang).astype(dtype), jnp.sin(ang).astype(dtype)


def _dft_factor_tables(n):
    n2 = n // DFT_SPLIT
    k2 = lax.broadcasted_iota(jnp.int32, (n2, n), 0)
    k1 = lax.broadcasted_iota(jnp.int32, (DFT_SPLIT, n), 0)
    alpha = ((k2 * lax.broadcasted_iota(jnp.int32, (n2, n), 1)) % n2).astype(F32) * (2.0 * math.pi / n2)
    beta = ((k1 * lax.broadcasted_iota(jnp.int32, (DFT_SPLIT, n), 1)) % n).astype(F32) * (2.0 * math.pi / n)
    return jnp.cos(alpha), jnp.sin(alpha), jnp.cos(beta), jnp.sin(beta)


def _fourier_mixer(f, w_four):
    b, l, df = f.shape
    tr = 512
    n_blk = l // tr
    assert l % (2 * tr) == 0
    cc, sc = _dft_tables(FOURIER_CH, F32)
    ca, sa, cb, sb = _dft_factor_tables(l)
    rev = (lax.broadcasted_iota(jnp.int32, (tr, tr + DFT_EXTRA), 0)
           + lax.broadcasted_iota(jnp.int32, (tr, tr + DFT_EXTRA), 1) == tr).astype(BF16)
    const = lambda bi, i: (0, 0)
    out = pl.pallas_call(
        functools.partial(_fourier_kernel, seq=l, rows=tr),
        grid=(b, n_blk // 2),
        in_specs=[pl.BlockSpec((1, l, df), lambda bi, i: (bi, 0, 0)),
                  pl.BlockSpec((FOURIER_CH, FOURIER_CH), const),
                  pl.BlockSpec((FOURIER_CH, FOURIER_CH), const),
                  pl.BlockSpec(w_four.shape, lambda bi, i: (0, 0, 0)),
                  pl.BlockSpec(ca.shape, const), pl.BlockSpec(sa.shape, const),
                  pl.BlockSpec(cb.shape, const), pl.BlockSpec(sb.shape, const),
                  pl.BlockSpec(rev.shape, const)],
        out_specs=pl.BlockSpec((1, n_blk, tr, df), lambda bi, i: (bi, 0, 0, 0)),
        out_shape=jax.ShapeDtypeStruct((b, n_blk, tr, df), BF16),
        scratch_shapes=[pltpu.VMEM((l, df), BF16), pltpu.VMEM((l, df), BF16),
                        pltpu.VMEM((tr + DFT_EXTRA, l), BF16), pltpu.VMEM((tr + DFT_EXTRA, l), BF16)],
        compiler_params=_params("parallel", "arbitrary"),
        name="fourier_mixer",
    )(f, cc, sc, w_four, ca, sa, cb, sb, rev)
    return out.reshape(b, l, df)


def _outproj_router_kernel(y_ref, four_ref, x_ref, g1_ref, sh_ref, sc_ref, nssd_ref, n2_ref,
                           wos_ref, wof_ref, wr_ref, br_ref,
                           x1_ref, h_ref, eid_ref, gate_ref):
    y = y_ref[0].astype(F32)
    ms = jnp.mean(y * y, axis=-1, keepdims=True)
    yn = (y * lax.rsqrt(ms + EPS) * nssd_ref[...]).astype(BF16)
    mix = (jnp.dot(yn, wos_ref[...], preferred_element_type=F32)
           + jnp.dot(four_ref[0], wof_ref[...], preferred_element_type=F32))
    x1 = x_ref[0] + g1_ref[0] * mix
    x1_ref[0] = x1
    ms2 = jnp.mean(x1 * x1, axis=-1, keepdims=True)
    h = (x1 * lax.rsqrt(ms2 + EPS) * n2_ref[...]) * (1.0 + sc_ref[0]) + sh_ref[0]
    _store_token_tiles(h_ref, (0,), h)

    lg = _dot_split3(h, wr_ref[...]) + br_ref[...]
    tm = lg.shape[0]
    lane = lax.broadcasted_iota(jnp.int32, (tm, ROUTE_LANES), 1)
    lane_f = lane.astype(F32)
    neg = jnp.float32(-1e30)
    big = jnp.float32(1e9)
    is_grp = lane < N_EXPERT_GROUPS
    gl = jnp.where(is_grp, lg, neg)
    gmax = jnp.max(gl, axis=-1, keepdims=True)
    gsum = jnp.sum(jnp.where(is_grp, jnp.exp(gl - gmax), 0.0), axis=-1, keepdims=True)
    grp = jnp.min(jnp.where(gl == gmax, lane_f, big), axis=-1, keepdims=True)
    p_grp = 1.0 / gsum
    lo = N_EXPERT_GROUPS + EXPERTS_PER_GROUP * grp
    in_grp = jnp.logical_and(lane_f >= lo, lane_f < lo + EXPERTS_PER_GROUP)
    el = jnp.where(in_grp, lg, neg)
    m1 = jnp.max(el, axis=-1, keepdims=True)
    i1 = jnp.min(jnp.where(el == m1, lane_f, big), axis=-1, keepdims=True)
    el2 = jnp.where(lane_f == i1, neg, el)
    m2 = jnp.max(el2, axis=-1, keepdims=True)
    i2 = jnp.min(jnp.where(el2 == m2, lane_f, big), axis=-1, keepdims=True)
    e2 = jnp.exp(m2 - m1)
    den = 1.0 + e2
    gate1 = p_grp / den
    gate2 = p_grp * e2 / den
    lane8 = lax.broadcasted_iota(jnp.int32, (tm, 8), 1)
    eid = jnp.where(lane8 == 0, i1 - N_EXPERT_GROUPS, jnp.where(lane8 == 1, i2 - N_EXPERT_GROUPS, 0.0))
    eid_ref[0] = eid.astype(jnp.int32)
    gate_ref[0] = jnp.where(lane8 == 0, gate1, jnp.where(lane8 == 1, gate2, 0.0))


def _outproj_router(y, four, x, g1, sh2, sc2, nssd, n2, wos, wof, wr, br):
    b, l, d = x.shape
    tm = min(512, l)
    row = lambda bi, i: (bi, i, 0)
    vec = lambda bi, i: (bi, 0, 0)
    const = lambda bi, i: (0, 0)
    return pl.pallas_call(
        _outproj_router_kernel,
        grid=(b, l // tm),
        in_specs=[pl.BlockSpec((1, tm, y.shape[2]), row),
                  pl.BlockSpec((1, tm, four.shape[2]), row),
                  pl.BlockSpec((1, tm, d), row),
                  pl.BlockSpec((1, 1, d), vec), pl.BlockSpec((1, 1, d), vec), pl.BlockSpec((1, 1, d), vec),
                  pl.BlockSpec((1, y.shape[2]), const), pl.BlockSpec((1, d), const),
                  pl.BlockSpec(wos.shape, const), pl.BlockSpec(wof.shape, const),
                  pl.BlockSpec(wr.shape, const), pl.BlockSpec(br.shape, const)],
        out_specs=[pl.BlockSpec((1, tm, d), row), pl.BlockSpec((1, tm * TILE_ROWS, LANES), row),
                   pl.BlockSpec((1, tm, 8), row), pl.BlockSpec((1, tm, 8), row)],
        out_shape=[jax.ShapeDtypeStruct((b, l, d), F32), jax.ShapeDtypeStruct((b, l * TILE_ROWS, LANES), F32),
                   jax.ShapeDtypeStruct((b, l, 8), jnp.int32), jax.ShapeDtypeStruct((b, l, 8), F32)],
        compiler_params=_params("parallel", "arbitrary"),
        name="outproj_router",
    )(y, four, x, g1, sh2, sc2, nssd, n2, wos, wof, wr, br)


def _expert_kernel(bexp_ref, nused_ref, tokc_ref, tokn_ref, dstp_ref, dstc_ref, h_hbm, wg_ref, wu_ref, wd_ref,
                   y_hbm, hbuf, ybuf, act_s, wg_s, wu_s, wd_s, sem_in, sem_out):
    i = pl.program_id(0)
    n_used = nused_ref[0]
    active = i < n_used
    is_last = i == n_used - 1
    slot = i % 2
    other = 1 - slot
    de = wg_s.shape[1]
    d = wd_s.shape[1]
    n_phase = 8
    rows_per_phase = MOE_ROWS // n_phase
    blk_tile_rows = MOE_ROWS * TILE_ROWS

    def tile(ref, lead, row0):
        return ref.at[lead + (pl.ds(row0, TILE_ROWS), slice(None))]

    def start_gather(tok_ref, s, r):
        src = tile(h_hbm, (), pl.multiple_of(tok_ref[0, 0, r], TILE_ROWS))
        pltpu.make_async_copy(src, tile(hbuf, (s,), r * TILE_ROWS), sem_in.at[s]).start(priority=r % 2)

    def start_scatter(dst_ref, s, r):
        dst = tile(y_hbm, (), pl.multiple_of(dst_ref[0, 0, r], TILE_ROWS))
        pltpu.make_async_copy(tile(ybuf, (s,), r * TILE_ROWS), dst, sem_out.at[s]).start(priority=r % 2)

    def wait_gather(s):
        pltpu.make_async_copy(h_hbm.at[pl.ds(0, blk_tile_rows), :], hbuf.at[s], sem_in.at[s]).wait()

    def wait_scatter(s):
        pltpu.make_async_copy(ybuf.at[s], y_hbm.at[pl.ds(0, blk_tile_rows), :], sem_out.at[s]).wait()

    @pl.when(i == 0)
    def _():
        for r in range(MOE_ROWS):
            start_gather(tokc_ref, 0, r)

    prev = bexp_ref[jnp.maximum(i - 1, 0)]

    @pl.when(jnp.logical_and(active, jnp.logical_or(i == 0, bexp_ref[i] != prev)))
    def _():
        wg_s[...] = wg_ref[0].astype(BF16)
        wu_s[...] = wu_ref[0].astype(BF16)
        wd_s[...] = wd_ref[0].astype(BF16)

    @pl.when(active)
    def _():
        wait_gather(slot)

    @pl.when(jnp.logical_and(active, i >= 2))
    def _():
        wait_scatter(slot)

    def step(with_scatter, cur):
        nxt = 1 - cur
        xb = _load_token_tiles(hbuf, (cur,), MOE_ROWS).astype(BF16)
        ys = []
        for p in range(n_phase):
            for r in range(p * rows_per_phase, (p + 1) * rows_per_phase):
                start_gather(tokn_ref, nxt, r)
                if with_scatter:
                    start_scatter(dstp_ref, nxt, r)
            if p < n_phase // 2:
                w = de // (n_phase // 2)
                cols = slice(p * w, (p + 1) * w)
                gact = jnp.dot(xb, wg_s[:, cols], preferred_element_type=F32)
                up = jnp.dot(xb, wu_s[:, cols], preferred_element_type=F32)
                act_s[:, cols] = (_silu(gact) * up).astype(BF16)
            else:
                w = d // (n_phase // 2)
                cols = slice((p - n_phase // 2) * w, (p - n_phase // 2 + 1) * w)
                ys.append(jnp.dot(act_s[...], wd_s[:, cols], preferred_element_type=F32))
        _store_token_tiles(ybuf, (cur,), jnp.concatenate(ys, axis=1))

    @pl.when(i == 0)
    def _():
        step(False, 0)

    @pl.when(jnp.logical_and(active, jnp.logical_and(i > 0, slot == 0)))
    def _():
        step(True, 0)

    @pl.when(jnp.logical_and(active, slot == 1))
    def _():
        step(True, 1)

    @pl.when(is_last)
    def _():
        for r in range(MOE_ROWS):
            start_scatter(dstc_ref, slot, r)
        wait_gather(other)
        wait_scatter(other)
        wait_scatter(slot)

    @pl.when(jnp.logical_not(active))
    def _():
        ybuf[0] = jnp.zeros((blk_tile_rows, LANES), F32)
        dst0 = pl.multiple_of(dstc_ref[0, 0, 0], TILE_ROWS)
        fill = pltpu.make_async_copy(ybuf.at[0], y_hbm.at[pl.ds(dst0, blk_tile_rows), :], sem_out.at[0])
        fill.start()
        fill.wait()


def _experts(h_tiles, row_tok, row_dst, blk_exp, n_used, w_eg, w_eu, w_ed):
    n_blocks = blk_exp.shape[0]
    d, de = w_eg.shape[1], w_eg.shape[2]
    assert d == TILE_ROWS * LANES
    idx_shape = (n_blocks, 1, MOE_ROWS)
    idx_block = (1, 1, MOE_ROWS)
    smem = pltpu.SMEM
    blk_tile_rows = MOE_ROWS * TILE_ROWS
    grid_spec = pltpu.PrefetchScalarGridSpec(
        num_scalar_prefetch=2,
        grid=(n_blocks,),
        in_specs=[pl.BlockSpec(idx_block, lambda i, be, nu: (i, 0, 0), memory_space=smem),
                  pl.BlockSpec(idx_block, lambda i, be, nu: (jnp.minimum(i + 1, n_blocks - 1), 0, 0),
                               memory_space=smem),
                  pl.BlockSpec(idx_block, lambda i, be, nu: (jnp.maximum(i - 1, 0), 0, 0), memory_space=smem),
                  pl.BlockSpec(idx_block, lambda i, be, nu: (i, 0, 0), memory_space=smem),
                  pl.BlockSpec(memory_space=pl.ANY),
                  pl.BlockSpec((1, d, de), lambda i, be, nu: (be[i], 0, 0)),
                  pl.BlockSpec((1, d, de), lambda i, be, nu: (be[i], 0, 0)),
                  pl.BlockSpec((1, de, d), lambda i, be, nu: (be[i], 0, 0))],
        out_specs=pl.BlockSpec(memory_space=pl.ANY),
        scratch_shapes=[pltpu.VMEM((2, blk_tile_rows, LANES), F32), pltpu.VMEM((2, blk_tile_rows, LANES), F32),
                        pltpu.VMEM((MOE_ROWS, de), BF16),
                        pltpu.VMEM((d, de), BF16), pltpu.VMEM((d, de), BF16), pltpu.VMEM((de, d), BF16),
                        pltpu.SemaphoreType.DMA((2,)), pltpu.SemaphoreType.DMA((2,))],
    )
    row_tok = (row_tok * TILE_ROWS).reshape(idx_shape)
    row_dst = (row_dst * TILE_ROWS).reshape(idx_shape)
    return pl.pallas_call(
        _expert_kernel,
        grid_spec=grid_spec,
        out_shape=jax.ShapeDtypeStruct((n_blocks * blk_tile_rows, LANES), F32),
        compiler_params=_params("arbitrary"),
        name="moe_experts",
    )(blk_exp, n_used, row_tok, row_tok, row_dst, row_dst, h_tiles, w_eg, w_eu, w_ed)


def _route_tables(eid, n_tok):
    n_assign = 2 * n_tok
    e_flat = jnp.concatenate([eid[:, 0], eid[:, 1]])
    order = jnp.argsort(e_flat).astype(jnp.int32)
    experts = jnp.arange(N_EXPERTS, dtype=jnp.int32)
    counts = jnp.sum((e_flat[:, None] == experts[None, :]).astype(jnp.int32), axis=0)
    start = jnp.cumsum(counts) - counts
    padded = (counts + MOE_ROWS - 1) // MOE_ROWS * MOE_ROWS
    end_pad = jnp.cumsum(padded)
    start_pad = end_pad - padded
    n_blocks = -(-(n_assign + N_EXPERTS * (MOE_ROWS - 1)) // MOE_ROWS)
    blk_row0 = jnp.arange(n_blocks, dtype=jnp.int32) * MOE_ROWS
    blk_exp = jnp.minimum(jnp.sum((end_pad[None, :] <= blk_row0[:, None]).astype(jnp.int32), axis=1),
                          N_EXPERTS - 1).astype(jnp.int32)
    j = (blk_row0 - start_pad[blk_exp])[:, None] + jnp.arange(MOE_ROWS, dtype=jnp.int32)[None, :]
    valid = j < counts[blk_exp][:, None]
    src = jnp.clip(start[blk_exp][:, None] + j, 0, n_assign - 1)
    assign = order[src.reshape(-1)].reshape(src.shape)
    row_tok = jnp.where(valid, jnp.where(assign >= n_tok, assign - n_tok, assign), 0)
    spare = n_assign + jnp.cumsum((~valid).reshape(-1).astype(jnp.int32)).reshape(valid.shape) - 1
    row_dst = jnp.where(valid, assign, spare)
    n_used = (end_pad[-1:] // MOE_ROWS).astype(jnp.int32)
    return row_tok.astype(jnp.int32), row_dst.astype(jnp.int32), blk_exp, n_used


def _combine_kernel(x1_ref, y0_ref, y1_ref, gate_ref, g2_ref, nw_ref, o_ref):
    gate = gate_ref[0]
    tm = gate.shape[0]
    moe = (gate[:, 0:1] * _load_token_tiles(y0_ref, (), tm) + gate[:, 1:2] * _load_token_tiles(y1_ref, (), tm))
    x2 = x1_ref[0] + g2_ref[0] * moe
    ms = jnp.mean(x2 * x2, axis=-1, keepdims=True)
    o_ref[0] = x2 * lax.rsqrt(ms + EPS) * nw_ref[...]


def _combine(x1, y_assign, gate, g2, final_norm):
    b, l, d = x1.shape
    tm = min(512, l)
    nt = l // tm
    row = lambda bi, i: (bi, i, 0)
    return pl.pallas_call(
        _combine_kernel,
        grid=(b, nt),
        in_specs=[pl.BlockSpec((1, tm, d), row),
                  pl.BlockSpec((tm * TILE_ROWS, LANES), lambda bi, i: (bi * nt + i, 0)),
                  pl.BlockSpec((tm * TILE_ROWS, LANES), lambda bi, i: (b * nt + bi * nt + i, 0)),
                  pl.BlockSpec((1, tm, 8), row),
                  pl.BlockSpec((1, 1, d), lambda bi, i: (bi, 0, 0)),
                  pl.BlockSpec((1, d), lambda bi, i: (0, 0))],
        out_specs=pl.BlockSpec((1, tm, d), row),
        out_shape=jax.ShapeDtypeStruct((b, l, d), F32),
        compiler_params=_params("parallel", "arbitrary"),
        name="moe_combine",
    )(x1, y_assign, y_assign, gate, g2, final_norm.reshape(1, d))


def _group_major(v):
    return v.reshape(2, N_GROUPS, HEADS_PER_GROUP).transpose(1, 0, 2).reshape(N_GROUPS, 1, 2 * HEADS_PER_GROUP)


def kernel(x, c, ctx, c_ctx, w_mod, b_mod, norm1, w_in, conv_w, conv_b, dt_bias, a_log, d_skip, ssd_norm,
           w_four, w_out, norm2, w_rg, b_rg, w_re, b_re, w_eg, w_eu, w_ed, final_norm):
    bsz, seq, d = x.shape
    n_tok = bsz * seq
    d_ssd = N_GROUPS * HEADS_PER_GROUP * HEADDIM
    conv_dim = d_ssd + 2 * N_GROUPS * D_STATE
    n_heads = N_GROUPS * HEADS_PER_GROUP
    layer = 0

    c_rows = jnp.zeros((16, d), F32).at[:bsz].set(c).at[bsz].set(c_ctx)
    mod = _modulation(c_rows, w_mod[layer], b_mod[layer])
    sh1, sc1, g1, sh2, sc2, g2 = [m[:bsz, None, :] for m in jnp.split(mod, 6, axis=-1)]
    sh1c, sc1c = [jnp.broadcast_to(m[bsz][None, None, :], (bsz, 1, d)) for m in jnp.split(mod, 6, axis=-1)[:2]]

    w = w_in[layer]
    wz = w[:, :d_ssd].astype(BF16)
    wx = w[:, d_ssd:d_ssd + conv_dim].astype(BF16)
    wdt = w[:, d_ssd + conv_dim:d_ssd + conv_dim + 2 * n_heads]
    wdt = wdt.reshape(d, 2, N_GROUPS, HEADS_PER_GROUP).transpose(0, 2, 1, 3).reshape(d, 2 * n_heads)
    wdt = jnp.pad(wdt, ((0, 0), (0, 128 - 2 * n_heads))).astype(BF16)
    wf = w[:, d_ssd + conv_dim + 2 * n_heads:].astype(BF16)
    n1 = norm1[layer].reshape(1, d)

    dtb = _group_major(dt_bias[layer])
    alog = _group_major(a_log[layer])
    dsk = jnp.repeat(d_skip[layer], HEADDIM).reshape(1, d_ssd)
    cw = conv_w[layer]
    cb = conv_b[layer].reshape(1, conv_dim)

    xbc_c, dt_c = _in_projection(ctx, sh1c, sc1c, n1, wx, wdt, cw, cb)
    h_zero = jnp.zeros((bsz, N_GROUPS, HEADS_PER_GROUP, D_STATE, HEADDIM), F32)
    z_dummy = jnp.zeros((bsz, 8, d_ssd), BF16)
    _, hf_c, hb_c = _ssd_mixer(xbc_c, dt_c, z_dummy, dtb, alog, dsk, h_zero, h_zero, emit_y=False)

    xbc_l, dt_l, z_l, f_l = _in_projection(x, sh1, sc1, n1, wx, wdt, cw, cb, wz, wf)
    y_l, _, _ = _ssd_mixer(xbc_l, dt_l, z_l, dtb, alog, dsk, hf_c, hb_c, emit_y=True)
    four = _fourier_mixer(f_l, w_four[layer])

    wo = w_out[layer]
    wr = jnp.concatenate([w_rg[layer], w_re[layer].transpose(1, 0, 2).reshape(d, N_EXPERTS)], axis=1)
    wr = jnp.pad(wr, ((0, 0), (0, ROUTE_LANES - wr.shape[1])))
    br = jnp.pad(jnp.concatenate([b_rg[layer], b_re[layer].reshape(-1)]),
                 (0, ROUTE_LANES - N_EXPERT_GROUPS - N_EXPERTS)).reshape(1, ROUTE_LANES)
    x1, h, eid, gate = _outproj_router(
        y_l, four, x, g1, sh2, sc2, ssd_norm[layer].reshape(1, d_ssd), norm2[layer].reshape(1, d),
        wo[:d_ssd].astype(BF16), wo[d_ssd:].astype(BF16), wr, br)

    row_tok, row_dst, blk_exp, n_used = _route_tables(eid.reshape(n_tok, 8), n_tok)
    y_assign = _experts(h.reshape(n_tok * TILE_ROWS, LANES), row_tok, row_dst, blk_exp, n_used,
                        w_eg[layer], w_eu[layer], w_ed[layer])
    return _combine(x1, y_assign, gate, g2, final_norm)
```

```python
import functools
import math

import jax
import jax.numpy as jnp
from jax import lax
from jax.experimental import pallas as pl
from jax.experimental.pallas import tpu as pltpu

F32 = jnp.float32
BF16 = jnp.bfloat16
HIGHEST = lax.Precision.HIGHEST

EPS = 1e-6
LOG2E = 1.4426950408889634
CHUNK = 128
N_GROUPS = 4
HEADS_PER_GROUP = 4
HEADDIM = 64
D_STATE = 128
CONV_W = 5
CONV_HALO = 16
N_FOURIER_GROUPS = 4
DFT_SPLIT = 64
DFT_EXTRA = 16
FOURIER_CH = 128
N_EXPERT_GROUPS = 4
EXPERTS_PER_GROUP = 8
N_EXPERTS = N_EXPERT_GROUPS * EXPERTS_PER_GROUP
ROUTE_LANES = 128
MOE_ROWS = 128
VMEM_LIMIT_BYTES = 56 * 1024 * 1024


def _params(*sem):
    return pltpu.CompilerParams(dimension_semantics=sem, vmem_limit_bytes=VMEM_LIMIT_BYTES)


def _silu(v):
    h = 0.5 * v
    return h + h * jnp.tanh(h)


LANES = 128
TILE_ROWS = 8


def _store_token_tiles(ref, lead, val):
    rows = val.shape[0]
    for j in range(TILE_ROWS):
        ref[lead + (pl.ds(j, rows, stride=TILE_ROWS), slice(None))] = val[:, LANES * j:LANES * (j + 1)]


def _load_token_tiles(ref, lead, rows):
    return jnp.concatenate(
        [ref[lead + (pl.ds(j, rows, stride=TILE_ROWS), slice(None))] for j in range(TILE_ROWS)], axis=1)


def _dot_split3(a, b):
    a_hi = a.astype(BF16)
    a_lo = (a - a_hi.astype(F32)).astype(BF16)
    b_hi = b.astype(BF16)
    b_lo = (b - b_hi.astype(F32)).astype(BF16)
    dot = functools.partial(jnp.dot, preferred_element_type=F32)
    return dot(a_hi, b_hi) + (dot(a_lo, b_hi) + dot(a_hi, b_lo))


def _mod_kernel(c_ref, w_ref, b_ref, o_ref):
    s = _silu(c_ref[...])
    o_ref[...] = jnp.dot(s, w_ref[...], preferred_element_type=F32, precision=HIGHEST) + b_ref[...]


def _modulation(c_rows, w_mod, b_mod):
    rows, d = c_rows.shape
    n = w_mod.shape[1]
    tn = 512
    return pl.pallas_call(
        _mod_kernel,
        grid=(n // tn,),
        in_specs=[pl.BlockSpec((rows, d), lambda j: (0, 0)),
                  pl.BlockSpec((d, tn), lambda j: (0, j)),
                  pl.BlockSpec((1, tn), lambda j: (0, j))],
        out_specs=pl.BlockSpec((rows, tn), lambda j: (0, j)),
        out_shape=jax.ShapeDtypeStruct((rows, n), F32),
        compiler_params=_params("arbitrary"),
        name="modulation",
    )(c_rows, w_mod, b_mod.reshape(1, n))


def _inproj_kernel(xp_ref, x_ref, xn_ref, sh_ref, sc_ref, nw_ref, wx_ref, wdt_ref, cw_ref, cb_ref, *rest,
                   with_zf):
    if with_zf:
        wz_ref, wf_ref, xbc_ref, dt_ref, z_ref, f_ref, pre_s = rest
    else:
        xbc_ref, dt_ref, pre_s = rest
    tm = x_ref.shape[1]
    i = pl.program_id(1)
    xa = jnp.concatenate([xp_ref[0], x_ref[0], xn_ref[0]], axis=0)
    ms = jnp.mean(xa * xa, axis=-1, keepdims=True)
    xnorm = xa * lax.rsqrt(ms + EPS) * nw_ref[...]
    xm_all = (xnorm * (1.0 + sc_ref[0]) + sh_ref[0]).astype(BF16)
    xm = xm_all[CONV_HALO:CONV_HALO + tm]

    pre_s[...] = jnp.dot(xm_all, wx_ref[...], preferred_element_type=F32)
    top, bot = slice(0, CONV_HALO), slice(CONV_HALO + tm, 2 * CONV_HALO + tm)
    pre_s[top, :] = pre_s[top, :] * (i > 0).astype(F32)
    pre_s[bot, :] = pre_s[bot, :] * (i < pl.num_programs(1) - 1).astype(F32)

    rblk, cblk = 64, 256
    for c0 in range(0, wx_ref.shape[1], cblk):
        cols = slice(c0, c0 + cblk)
        w = cw_ref[:, cols]
        bias = jnp.broadcast_to(cb_ref[:, cols], (rblk, cblk))
        for r0 in range(0, tm, rblk):
            lo = r0 + CONV_HALO - 8
            win = pre_s[lo:lo + rblk + 16, cols]
            acc = bias
            for k in range(CONV_W):
                off = 8 - CONV_W // 2 + k
                acc = acc + w[k:k + 1, :] * win[off:off + rblk, :]
            xbc_ref[0, r0:r0 + rblk, cols] = _silu(acc).astype(BF16)

    dt = jnp.dot(xm, wdt_ref[...], preferred_element_type=F32)
    for g in range(N_GROUPS):
        dt_ref[0, g] = dt[:, 8 * g:8 * g + 8]
    if with_zf:
        z_ref[0] = jnp.dot(xm, wz_ref[...], preferred_element_type=F32).astype(BF16)
        f_ref[0] = jnp.dot(xm, wf_ref[...], preferred_element_type=F32).astype(BF16)


def _in_projection(x, shift, scale, norm_w, wx, wdt, conv_w, conv_b, wz=None, wf=None):
    b, l, d = x.shape
    tm = min(512, l)
    with_zf = wz is not None
    hb = tm // CONV_HALO
    n_hb = l // CONV_HALO
    row = lambda bi, i: (bi, i, 0)
    vec = lambda bi, i: (bi, 0, 0)
    const = lambda bi, i: (0, 0)
    in_specs = [pl.BlockSpec((1, CONV_HALO, d), lambda bi, i: (bi, jnp.maximum(i * hb - 1, 0), 0)),
                pl.BlockSpec((1, tm, d), row),
                pl.BlockSpec((1, CONV_HALO, d), lambda bi, i: (bi, jnp.minimum((i + 1) * hb, n_hb - 1), 0)),
                pl.BlockSpec((1, 1, d), vec),
                pl.BlockSpec((1, 1, d), vec),
                pl.BlockSpec((1, d), const),
                pl.BlockSpec(wx.shape, const),
                pl.BlockSpec(wdt.shape, const),
                pl.BlockSpec(conv_w.shape, const),
                pl.BlockSpec(conv_b.shape, const)]
    args = [x, x, x, shift, scale, norm_w, wx, wdt, conv_w, conv_b]
    out_specs = [pl.BlockSpec((1, tm, wx.shape[1]), row),
                 pl.BlockSpec((1, N_GROUPS, tm, 8), lambda bi, i: (bi, 0, i, 0))]
    out_shape = [jax.ShapeDtypeStruct((b, l, wx.shape[1]), BF16),
                 jax.ShapeDtypeStruct((b, N_GROUPS, l, 8), F32)]
    if with_zf:
        in_specs += [pl.BlockSpec(wz.shape, const), pl.BlockSpec(wf.shape, const)]
        args += [wz, wf]
        out_specs += [pl.BlockSpec((1, tm, wz.shape[1]), row), pl.BlockSpec((1, tm, wf.shape[1]), row)]
        out_shape += [jax.ShapeDtypeStruct((b, l, wz.shape[1]), BF16),
                      jax.ShapeDtypeStruct((b, l, wf.shape[1]), BF16)]
    return pl.pallas_call(
        functools.partial(_inproj_kernel, with_zf=with_zf),
        grid=(b, l // tm),
        in_specs=in_specs,
        out_specs=out_specs,
        out_shape=out_shape,
        scratch_shapes=[pltpu.VMEM((tm + 2 * CONV_HALO, wx.shape[1]), F32)],
        compiler_params=_params("parallel", "arbitrary"),
        name="in_projection_zf" if with_zf else "in_projection",
    )(*args)


def _ssd_kernel(xs_ref, xb_ref, xc_ref, dt_ref, z_ref,
                dtb_ref, alog_ref, dsk_ref, h0f_ref, h0b_ref,
                y_ref, hf_ref, hb_ref,
                dt_s, cs_s, cst_s, bt_s, cb_s, yf_s, hst_s, *, seq, emit_y):
    nc = seq // CHUNK

    dtr = dt_ref[0, 0] + dtb_ref[0]
    dtv = jnp.maximum(dtr, 0.0) + jnp.log(1.0 + jnp.exp(-jnp.abs(dtr)))
    dt_s[...] = dtv
    a_log2 = -jnp.exp(alog_ref[0]) * LOG2E

    ri = lax.broadcasted_iota(jnp.int32, (CHUNK, CHUNK), 0)
    ci = lax.broadcasted_iota(jnp.int32, (CHUNK, CHUNK), 1)
    tri_fwd = ci <= ri
    tri_bwd = ci >= ri
    is_fwd_lane8 = lax.broadcasted_iota(jnp.int32, (CHUNK, 8), 1) < HEADS_PER_GROUP
    lane_pad = jnp.zeros((CHUNK, CHUNK - 8), F32)

    tri16 = tri_fwd.astype(BF16)

    def tables(c, carry):
        r0 = pl.multiple_of(c * CHUNK, CHUNK)
        rows = pl.ds(r0, CHUNK)
        da = dt_s[rows, :] * a_log2
        p0 = da.astype(BF16)
        r1 = da - p0.astype(F32)
        p1 = r1.astype(BF16)
        p2 = (r1 - p1.astype(F32)).astype(BF16)
        packed = jnp.concatenate([p0, p1, p2, jnp.zeros((CHUNK, CHUNK - 24), BF16)], axis=1)
        acc = jnp.dot(tri16, packed, preferred_element_type=F32)
        cs_f = acc[:, 0:8] + acc[:, 8:16] + acc[:, 16:24]
        cs_b = cs_f[CHUNK - 1:CHUNK, :] - cs_f + da
        cs = jnp.where(is_fwd_lane8, cs_f, cs_b)
        cs_s[rows, :] = cs
        cs_t = jnp.concatenate([cs, lane_pad], axis=1).T[:8, :]
        dt_t = jnp.concatenate([dt_s[rows, :], lane_pad], axis=1).T[:8, :]
        cst_s[c] = cs_t - jnp.log2(dt_t)
        bt = xb_ref[0, rows, :].astype(F32).T
        bt_s[c] = bt
        cb_s[c] = jnp.dot(xc_ref[0, rows, :], bt.astype(BF16), preferred_element_type=F32)
        return carry

    lax.fori_loop(0, nc, tables, 0, unroll=2)

    for j in range(HEADS_PER_GROUP):
        hst_s[0, j] = h0f_ref[0, 0, j]
        hst_s[1, j] = h0b_ref[0, 0, j]

    def chunk_dir(d, c, final):
        tri = tri_fwd if d == 0 else tri_bwd
        tot_row = CHUNK - 1 if d == 0 else 0
        r0 = pl.multiple_of(c * CHUNK, CHUNK)
        rows = pl.ds(r0, CHUNK)
        xs = xs_ref[0, rows, :]
        cc = xc_ref[0, rows, :].astype(F32)
        cs = cs_s[rows, :]
        cs_t = cst_s[c]
        bt = bt_s[c]
        cb = cb_s[c]
        ys = []
        for j in range(HEADS_PER_GROUP):
            lane = HEADS_PER_GROUP * d + j
            a_col1 = cs[:, lane:lane + 1]
            a_col = jnp.broadcast_to(a_col1, (CHUNK, CHUNK))
            a_tot = a_col1[tot_row:tot_row + 1, :]
            a_row = cs_t[lane:lane + 1, :]
            g = (cb * jnp.where(tri, jnp.exp2(a_col - a_row), 0.0)).astype(BF16)
            cea = (cc * jnp.exp2(a_col)).astype(BF16)
            xh = xs[:, HEADDIM * j:HEADDIM * (j + 1)]
            h = hst_s[d, j]
            y_h = (jnp.dot(g, xh, preferred_element_type=F32)
                   + jnp.dot(cea, h.astype(BF16), preferred_element_type=F32))
            w_row = jnp.exp2(a_tot - a_row)
            s_new = jnp.dot((bt * w_row).astype(BF16), xh, preferred_element_type=F32)
            hst_s[d, j] = h * jnp.exp2(a_tot) + s_new
            ys.append(y_h)
        if not emit_y:
            return
        y_c = jnp.concatenate(ys, axis=1)
        if d == 0:
            y_c = y_c + dsk_ref[...] * xs.astype(F32)
        if final:
            zc = z_ref[0, rows, :].astype(F32)
            y_ref[0, rows, :] = ((yf_s[rows, :] + y_c) * _silu(zc)).astype(BF16)
        else:
            yf_s[rows, :] = y_c

    def first_half(t, carry):
        chunk_dir(0, t, False)
        chunk_dir(1, nc - 1 - t, False)
        return carry

    def second_half(t, carry):
        chunk_dir(0, t, True)
        chunk_dir(1, nc - 1 - t, True)
        return carry

    lax.fori_loop(0, nc // 2, first_half, 0)
    lax.fori_loop(nc // 2, nc, second_half, 0)
    for j in range(HEADS_PER_GROUP):
        hf_ref[0, 0, j] = hst_s[0, j]
        hb_ref[0, 0, j] = hst_s[1, j]
    if not emit_y:
        y_ref[...] = jnp.zeros(y_ref.shape, y_ref.dtype)


def _ssd_mixer(xbc, dt, z, dtb, alog, dskip, h0f, h0b, emit_y):
    b, l, _ = xbc.shape
    gw = HEADS_PER_GROUP * HEADDIM
    nxb = (N_GROUPS * gw) // D_STATE
    y_rows = l if emit_y else 8
    nc = l // CHUNK
    st_shape = (b, N_GROUPS, HEADS_PER_GROUP, D_STATE, HEADDIM)
    st_spec = pl.BlockSpec((1, 1, HEADS_PER_GROUP, D_STATE, HEADDIM), lambda bi, g: (bi, g, 0, 0, 0))
    in_specs = [
        pl.BlockSpec((1, l, gw), lambda bi, g: (bi, 0, g)),
        pl.BlockSpec((1, l, D_STATE), lambda bi, g: (bi, 0, nxb + g)),
        pl.BlockSpec((1, l, D_STATE), lambda bi, g: (bi, 0, nxb + N_GROUPS + g)),
        pl.BlockSpec((1, 1, l, 8), lambda bi, g: (bi, g, 0, 0)),
        pl.BlockSpec((1, y_rows, gw), lambda bi, g: (bi, 0, g)),
        pl.BlockSpec((1, 1, 8), lambda bi, g: (g, 0, 0)),
        pl.BlockSpec((1, 1, 8), lambda bi, g: (g, 0, 0)),
        pl.BlockSpec((1, gw), lambda bi, g: (0, g)),
        st_spec, st_spec,
    ]
    out_specs = [pl.BlockSpec((1, y_rows, gw), lambda bi, g: (bi, 0, g)), st_spec, st_spec]
    out_shape = [jax.ShapeDtypeStruct((b, y_rows, N_GROUPS * gw), BF16),
                 jax.ShapeDtypeStruct(st_shape, F32), jax.ShapeDtypeStruct(st_shape, F32)]
    scratch = [pltpu.VMEM((l, 8), F32), pltpu.VMEM((l, 8), F32),
               pltpu.VMEM((nc, 8, CHUNK), F32),
               pltpu.VMEM((nc, D_STATE, CHUNK), F32), pltpu.VMEM((nc, CHUNK, CHUNK), F32),
               pltpu.VMEM((y_rows, gw), F32),
               pltpu.VMEM((2, HEADS_PER_GROUP, D_STATE, HEADDIM), F32)]
    return pl.pallas_call(
        functools.partial(_ssd_kernel, seq=l, emit_y=emit_y),
        grid=(b, N_GROUPS),
        in_specs=in_specs, out_specs=out_specs, out_shape=out_shape,
        scratch_shapes=scratch,
        compiler_params=_params("parallel", "arbitrary"),
        name="ssd_mixer" if emit_y else "ssd_mixer_ctx",
    )(xbc, xbc, xbc, dt, z, dtb, alog, dskip, h0f, h0b)


def _fourier_kernel(f_ref, cc_ref, sc_ref, w_ref, ca_ref, sa_ref, cb_ref, sb_ref, rev_ref, o_ref,
                    u_s, v_s, cl_s, sl_s, *, seq, rows):
    i = pl.program_id(1)
    n_blk = o_ref.shape[1]

    @pl.when(i == 0)
    def _():
        scale = 1.0 / math.sqrt(seq * FOURIER_CH)
        for g in range(N_FOURIER_GROUPS):
            w = w_ref[g]
            a = jnp.dot(cc_ref[...], w, preferred_element_type=F32, precision=HIGHEST) * scale
            bm = jnp.dot(sc_ref[...], w, preferred_element_type=F32, precision=HIGHEST) * scale
            cols = slice(FOURIER_CH * g, FOURIER_CH * (g + 1))
            fg = f_ref[0, :, cols]
            u_s[:, cols] = jnp.dot(fg, a.astype(BF16), preferred_element_type=F32).astype(BF16)
            v_s[:, cols] = jnp.dot(fg, bm.astype(BF16), preferred_element_type=F32).astype(BF16)

    k2_0 = i * (rows // DFT_SPLIT)
    cbeta = cb_ref[...]
    sbeta = sb_ref[...]
    for j in range(rows // DFT_SPLIT + 1):
        n_sub = DFT_SPLIT if j < rows // DFT_SPLIT else DFT_EXTRA
        calpha = ca_ref[pl.ds(k2_0 + j, 1), :]
        salpha = sa_ref[pl.ds(k2_0 + j, 1), :]
        sub = slice(DFT_SPLIT * j, DFT_SPLIT * j + n_sub)
        cl_s[sub, :] = (calpha * cbeta[:n_sub] - salpha * sbeta[:n_sub]).astype(BF16)
        sl_s[sub, :] = (salpha * cbeta[:n_sub] + calpha * sbeta[:n_sub]).astype(BF16)
    p = jnp.dot(cl_s[...], u_s[...], preferred_element_type=F32)
    q = jnp.dot(sl_s[...], v_s[...], preferred_element_type=F32)
    o_ref[0, i] = (p - q)[:rows].astype(BF16)
    o_ref[0, n_blk - 1 - i] = jnp.dot(rev_ref[...], (p + q).astype(BF16),
                                      preferred_element_type=F32).astype(BF16)


def _dft_tables(n, dtype):
    k = lax.broadcasted_iota(jnp.int32, (n, n), 0)
    l = lax.broadcasted_iota(jnp.int32, (n, n), 1)
    ang = ((k * l) % n).astype(F32) * (2.0 * math.pi / n)
    return jnp.cos(ang).astype(dtype), jnp.sin(ang).astype(dtype)


def _dft_factor_tables(n):
    n2 = n // DFT_SPLIT
    k2 = lax.broadcasted_iota(jnp.int32, (n2, n), 0)
    k1 = lax.broadcasted_iota(jnp.int32, (DFT_SPLIT, n), 0)
    alpha = ((k2 * lax.broadcasted_iota(jnp.int32, (n2, n), 1)) % n2).astype(F32) * (2.0 * math.pi / n2)
    beta = ((k1 * lax.broadcasted_iota(jnp.int32, (DFT_SPLIT, n), 1)) % n).astype(F32) * (2.0 * math.pi / n)
    return jnp.cos(alpha), jnp.sin(alpha), jnp.cos(beta), jnp.sin(beta)


def _fourier_mixer(f, w_four):
    b, l, df = f.shape
    tr = 512
    n_blk = l // tr
    assert l % (2 * tr) == 0
    cc, sc = _dft_tables(FOURIER_CH, F32)
    ca, sa, cb, sb = _dft_factor_tables(l)
    rev = (lax.broadcasted_iota(jnp.int32, (tr, tr + DFT_EXTRA), 0)
           + lax.broadcasted_iota(jnp.int32, (tr, tr + DFT_EXTRA), 1) == tr).astype(BF16)
    const = lambda bi, i: (0, 0)
    out = pl.pallas_call(
        functools.partial(_fourier_kernel, seq=l, rows=tr),
        grid=(b, n_blk // 2),
        in_specs=[pl.BlockSpec((1, l, df), lambda bi, i: (bi, 0, 0)),
                  pl.BlockSpec((FOURIER_CH, FOURIER_CH), const),
                  pl.BlockSpec((FOURIER_CH, FOURIER_CH), const),
                  pl.BlockSpec(w_four.shape, lambda bi, i: (0, 0, 0)),
                  pl.BlockSpec(ca.shape, const), pl.BlockSpec(sa.shape, const),
                  pl.BlockSpec(cb.shape, const), pl.BlockSpec(sb.shape, const),
                  pl.BlockSpec(rev.shape, const)],
        out_specs=pl.BlockSpec((1, n_blk, tr, df), lambda bi, i: (bi, 0, 0, 0)),
        out_shape=jax.ShapeDtypeStruct((b, n_blk, tr, df), BF16),
        scratch_shapes=[pltpu.VMEM((l, df), BF16), pltpu.VMEM((l, df), BF16),
                        pltpu.VMEM((tr + DFT_EXTRA, l), BF16), pltpu.VMEM((tr + DFT_EXTRA, l), BF16)],
        compiler_params=_params("parallel", "arbitrary"),
        name="fourier_mixer",
    )(f, cc, sc, w_four, ca, sa, cb, sb, rev)
    return out.reshape(b, l, df)


def _outproj_router_kernel(y_ref, four_ref, x_ref, g1_ref, sh_ref, sc_ref, nssd_ref, n2_ref,
                           wos_ref, wof_ref, wr_ref, br_ref,
                           x1_ref, h_ref, eid_ref, gate_ref):
    y = y_ref[0].astype(F32)
    ms = jnp.mean(y * y, axis=-1, keepdims=True)
    yn = (y * lax.rsqrt(ms + EPS) * nssd_ref[...]).astype(BF16)
    mix = (jnp.dot(yn, wos_ref[...], preferred_element_type=F32)
           + jnp.dot(four_ref[0], wof_ref[...], preferred_element_type=F32))
    x1 = x_ref[0] + g1_ref[0] * mix
    x1_ref[0] = x1
    ms2 = jnp.mean(x1 * x1, axis=-1, keepdims=True)
    h = (x1 * lax.rsqrt(ms2 + EPS) * n2_ref[...]) * (1.0 + sc_ref[0]) + sh_ref[0]
    _store_token_tiles(h_ref, (0,), h)

    lg = _dot_split3(h, wr_ref[...]) + br_ref[...]
    tm = lg.shape[0]
    lane = lax.broadcasted_iota(jnp.int32, (tm, ROUTE_LANES), 1)
    lane_f = lane.astype(F32)
    neg = jnp.float32(-1e30)
    big = jnp.float32(1e9)
    is_grp = lane < N_EXPERT_GROUPS
    gl = jnp.where(is_grp, lg, neg)
    gmax = jnp.max(gl, axis=-1, keepdims=True)
    gsum = jnp.sum(jnp.where(is_grp, jnp.exp(gl - gmax), 0.0), axis=-1, keepdims=True)
    grp = jnp.min(jnp.where(gl == gmax, lane_f, big), axis=-1, keepdims=True)
    p_grp = 1.0 / gsum
    lo = N_EXPERT_GROUPS + EXPERTS_PER_GROUP * grp
    in_grp = jnp.logical_and(lane_f >= lo, lane_f < lo + EXPERTS_PER_GROUP)
    el = jnp.where(in_grp, lg, neg)
    m1 = jnp.max(el, axis=-1, keepdims=True)
    i1 = jnp.min(jnp.where(el == m1, lane_f, big), axis=-1, keepdims=True)
    el2 = jnp.where(lane_f == i1, neg, el)
    m2 = jnp.max(el2, axis=-1, keepdims=True)
    i2 = jnp.min(jnp.where(el2 == m2, lane_f, big), axis=-1, keepdims=True)
    e2 = jnp.exp(m2 - m1)
    den = 1.0 + e2
    gate1 = p_grp / den
    gate2 = p_grp * e2 / den
    lane8 = lax.broadcasted_iota(jnp.int32, (tm, 8), 1)
    eid = jnp.where(lane8 == 0, i1 - N_EXPERT_GROUPS, jnp.where(lane8 == 1, i2 - N_EXPERT_GROUPS, 0.0))
    eid_ref[0] = eid.astype(jnp.int32)
    gate_ref[0] = jnp.where(lane8 == 0, gate1, jnp.where(lane8 == 1, gate2, 0.0))


def _outproj_router(y, four, x, g1, sh2, sc2, nssd, n2, wos, wof, wr, br):
    b, l, d = x.shape
    tm = min(512, l)
    row = lambda bi, i: (bi, i, 0)
    vec = lambda bi, i: (bi, 0, 0)
    const = lambda bi, i: (0, 0)
    return pl.pallas_call(
        _outproj_router_kernel,
        grid=(b, l // tm),
        in_specs=[pl.BlockSpec((1, tm, y.shape[2]), row),
                  pl.BlockSpec((1, tm, four.shape[2]), row),
                  pl.BlockSpec((1, tm, d), row),
                  pl.BlockSpec((1, 1, d), vec), pl.BlockSpec((1, 1, d), vec), pl.BlockSpec((1, 1, d), vec),
                  pl.BlockSpec((1, y.shape[2]), const), pl.BlockSpec((1, d), const),
                  pl.BlockSpec(wos.shape, const), pl.BlockSpec(wof.shape, const),
                  pl.BlockSpec(wr.shape, const), pl.BlockSpec(br.shape, const)],
        out_specs=[pl.BlockSpec((1, tm, d), row), pl.BlockSpec((1, tm * TILE_ROWS, LANES), row),
                   pl.BlockSpec((1, tm, 8), row), pl.BlockSpec((1, tm, 8), row)],
        out_shape=[jax.ShapeDtypeStruct((b, l, d), F32), jax.ShapeDtypeStruct((b, l * TILE_ROWS, LANES), F32),
                   jax.ShapeDtypeStruct((b, l, 8), jnp.int32), jax.ShapeDtypeStruct((b, l, 8), F32)],
        compiler_params=_params("parallel", "arbitrary"),
        name="outproj_router",
    )(y, four, x, g1, sh2, sc2, nssd, n2, wos, wof, wr, br)


def _expert_kernel(blo_ref, bhi_ref, nused_ref, tokc_ref, tokn_ref, dstp_ref, dstc_ref, gate_ref, h_hbm,
                   wgl_ref, wul_ref, wdl_ref, wgh_ref, wuh_ref, wdh_ref,
                   y_hbm, hbuf, ybuf, actl_s, acth_s, wgl_s, wul_s, wdl_s, wgh_s, wuh_s, wdh_s, sem_in, sem_out):
    i = pl.program_id(0)
    n_used = nused_ref[0]
    active = i < n_used
    is_last = i == n_used - 1
    slot = i % 2
    other = 1 - slot
    de = wgl_s.shape[1]
    d = wdl_s.shape[1]
    n_phase = 8
    rows_per_phase = MOE_ROWS // n_phase
    blk_tile_rows = MOE_ROWS * TILE_ROWS

    def tile(ref, lead, row0):
        return ref.at[lead + (pl.ds(row0, TILE_ROWS), slice(None))]

    def start_gather(tok_ref, s, r):
        src = tile(h_hbm, (), pl.multiple_of(tok_ref[0, 0, r], TILE_ROWS))
        pltpu.make_async_copy(src, tile(hbuf, (s,), r * TILE_ROWS), sem_in.at[s]).start(priority=r % 2)

    def start_scatter(dst_ref, s, r):
        dst = tile(y_hbm, (), pl.multiple_of(dst_ref[0, 0, r], TILE_ROWS))
        pltpu.make_async_copy(tile(ybuf, (s,), r * TILE_ROWS), dst, sem_out.at[s]).start(priority=r % 2)

    def wait_gather(s):
        pltpu.make_async_copy(h_hbm.at[pl.ds(0, blk_tile_rows), :], hbuf.at[s], sem_in.at[s]).wait()

    def wait_scatter(s):
        pltpu.make_async_copy(ybuf.at[s], y_hbm.at[pl.ds(0, blk_tile_rows), :], sem_out.at[s]).wait()

    @pl.when(i == 0)
    def _():
        for r in range(MOE_ROWS):
            start_gather(tokc_ref, 0, r)

    prev = jnp.maximum(i - 1, 0)

    @pl.when(jnp.logical_and(active, jnp.logical_or(i == 0, blo_ref[i] != blo_ref[prev])))
    def _():
        wgl_s[...] = wgl_ref[0].astype(BF16)
        wul_s[...] = wul_ref[0].astype(BF16)
        wdl_s[...] = wdl_ref[0].astype(BF16)

    @pl.when(jnp.logical_and(active, jnp.logical_or(i == 0, bhi_ref[i] != bhi_ref[prev])))
    def _():
        wgh_s[...] = wgh_ref[0].astype(BF16)
        wuh_s[...] = wuh_ref[0].astype(BF16)
        wdh_s[...] = wdh_ref[0].astype(BF16)

    @pl.when(active)
    def _():
        wait_gather(slot)

    @pl.when(jnp.logical_and(active, i >= 2))
    def _():
        wait_scatter(slot)

    def step(with_scatter, cur):
        nxt = 1 - cur
        xb = _load_token_tiles(hbuf, (cur,), MOE_ROWS).astype(BF16)
        gate = gate_ref[0]
        g_lo = gate[:, 0:1]
        g_hi = gate[:, 1:2]
        half = de // 2
        quarter = d // 4

        def hidden(wg_s, wu_s, act_s, c):
            cols = slice(c * half, (c + 1) * half)
            gact = jnp.dot(xb, wg_s[:, cols], preferred_element_type=F32)
            up = jnp.dot(xb, wu_s[:, cols], preferred_element_type=F32)
            act_s[:, cols] = (_silu(gact) * up).astype(BF16)

        ys = []

        def output(c):
            cols = slice(c * quarter, (c + 1) * quarter)
            ys.append(g_lo * jnp.dot(actl_s[...], wdl_s[:, cols], preferred_element_type=F32)
                      + g_hi * jnp.dot(acth_s[...], wdh_s[:, cols], preferred_element_type=F32))

        work = [functools.partial(hidden, wgl_s, wul_s, actl_s, 0), functools.partial(hidden, wgl_s, wul_s, actl_s, 1),
                functools.partial(hidden, wgh_s, wuh_s, acth_s, 0), functools.partial(hidden, wgh_s, wuh_s, acth_s, 1),
                functools.partial(output, 0), functools.partial(output, 1),
                functools.partial(output, 2), functools.partial(output, 3)]
        for p in range(n_phase):
            for r in range(p * rows_per_phase, (p + 1) * rows_per_phase):
                start_gather(tokn_ref, nxt, r)
                if with_scatter:
                    start_scatter(dstp_ref, nxt, r)
            work[p]()
        _store_token_tiles(ybuf, (cur,), jnp.concatenate(ys, axis=1))

    @pl.when(i == 0)
    def _():
        step(False, 0)

    @pl.when(jnp.logical_and(active, jnp.logical_and(i > 0, slot == 0)))
    def _():
        step(True, 0)

    @pl.when(jnp.logical_and(active, slot == 1))
    def _():
        step(True, 1)

    @pl.when(is_last)
    def _():
        for r in range(MOE_ROWS):
            start_scatter(dstc_ref, slot, r)
        wait_gather(other)
        wait_scatter(other)
        wait_scatter(slot)

    @pl.when(jnp.logical_not(active))
    def _():
        ybuf[0] = jnp.zeros((blk_tile_rows, LANES), F32)
        dst0 = pl.multiple_of(dstc_ref[0, 0, 0], TILE_ROWS)
        fill = pltpu.make_async_copy(ybuf.at[0], y_hbm.at[pl.ds(dst0, blk_tile_rows), :], sem_out.at[0])
        fill.start()
        fill.wait()


def _experts(h_tiles, row_tok, row_dst, row_gate, blk_lo, blk_hi, n_used, w_eg, w_eu, w_ed):
    n_blocks = blk_lo.shape[0]
    d, de = w_eg.shape[1], w_eg.shape[2]
    assert d == TILE_ROWS * LANES
    idx_shape = (n_blocks, 1, MOE_ROWS)
    idx_block = (1, 1, MOE_ROWS)
    smem = pltpu.SMEM
    blk_tile_rows = MOE_ROWS * TILE_ROWS
    lo = lambda i, bl, bh, nu: (bl[i], 0, 0)
    hi = lambda i, bl, bh, nu: (bh[i], 0, 0)
    grid_spec = pltpu.PrefetchScalarGridSpec(
        num_scalar_prefetch=3,
        grid=(n_blocks,),
        in_specs=[pl.BlockSpec(idx_block, lambda i, bl, bh, nu: (i, 0, 0), memory_space=smem),
                  pl.BlockSpec(idx_block, lambda i, bl, bh, nu: (jnp.minimum(i + 1, n_blocks - 1), 0, 0),
                               memory_space=smem),
                  pl.BlockSpec(idx_block, lambda i, bl, bh, nu: (jnp.maximum(i - 1, 0), 0, 0), memory_space=smem),
                  pl.BlockSpec(idx_block, lambda i, bl, bh, nu: (i, 0, 0), memory_space=smem),
                  pl.BlockSpec((1, MOE_ROWS, 8), lambda i, bl, bh, nu: (i, 0, 0)),
                  pl.BlockSpec(memory_space=pl.ANY),
                  pl.BlockSpec((1, d, de), lo), pl.BlockSpec((1, d, de), lo), pl.BlockSpec((1, de, d), lo),
                  pl.BlockSpec((1, d, de), hi), pl.BlockSpec((1, d, de), hi), pl.BlockSpec((1, de, d), hi)],
        out_specs=pl.BlockSpec(memory_space=pl.ANY),
        scratch_shapes=[pltpu.VMEM((2, blk_tile_rows, LANES), F32), pltpu.VMEM((2, blk_tile_rows, LANES), F32),
                        pltpu.VMEM((MOE_ROWS, de), BF16), pltpu.VMEM((MOE_ROWS, de), BF16),
                        pltpu.VMEM((d, de), BF16), pltpu.VMEM((d, de), BF16), pltpu.VMEM((de, d), BF16),
                        pltpu.VMEM((d, de), BF16), pltpu.VMEM((d, de), BF16), pltpu.VMEM((de, d), BF16),
                        pltpu.SemaphoreType.DMA((2,)), pltpu.SemaphoreType.DMA((2,))],
    )
    row_tok = (row_tok * TILE_ROWS).reshape(idx_shape)
    row_dst = (row_dst * TILE_ROWS).reshape(idx_shape)
    return pl.pallas_call(
        _expert_kernel,
        grid_spec=grid_spec,
        out_shape=jax.ShapeDtypeStruct((n_blocks * blk_tile_rows, LANES), F32),
        compiler_params=_params("arbitrary"),
        name="moe_experts",
    )(blk_lo, blk_hi, n_used, row_tok, row_tok, row_dst, row_dst, row_gate, h_tiles,
      w_eg, w_eu, w_ed, w_eg, w_eu, w_ed)


_PAIRS = [(a, b) for a in range(EXPERTS_PER_GROUP) for b in range(a + 1, EXPERTS_PER_GROUP)]


def _route_tables(eid, gate, n_tok):
    n_pairs = len(_PAIRS)
    n_buckets = N_EXPERT_GROUPS * n_pairs
    e1, e2 = eid[:, 0], eid[:, 1]
    first_is_lo = e1 <= e2
    lo = jnp.where(first_is_lo, e1, e2)
    hi = jnp.where(first_is_lo, e2, e1)
    gate_lh = jnp.stack([jnp.where(first_is_lo, gate[:, 0], gate[:, 1]),
                         jnp.where(first_is_lo, gate[:, 1], gate[:, 0])], axis=1)
    a = lo % EXPERTS_PER_GROUP
    b = hi % EXPERTS_PER_GROUP
    pair = a * (2 * EXPERTS_PER_GROUP - 1 - a) // 2 + (b - a - 1)
    bucket = (lo // EXPERTS_PER_GROUP) * n_pairs + pair
    order = jnp.argsort(bucket).astype(jnp.int32)
    ids = jnp.arange(n_buckets, dtype=jnp.int32)
    counts = jnp.sum((bucket[:, None] == ids[None, :]).astype(jnp.int32), axis=0)
    start = jnp.cumsum(counts) - counts
    padded = (counts + MOE_ROWS - 1) // MOE_ROWS * MOE_ROWS
    end_pad = jnp.cumsum(padded)
    start_pad = end_pad - padded
    n_blocks = -(-(n_tok + n_buckets * (MOE_ROWS - 1)) // MOE_ROWS)
    n_blocks += n_blocks % 2
    blk_row0 = jnp.arange(n_blocks, dtype=jnp.int32) * MOE_ROWS
    blk_bucket = jnp.minimum(jnp.sum((end_pad[None, :] <= blk_row0[:, None]).astype(jnp.int32), axis=1),
                             n_buckets - 1).astype(jnp.int32)
    j = (blk_row0 - start_pad[blk_bucket])[:, None] + jnp.arange(MOE_ROWS, dtype=jnp.int32)[None, :]
    valid = j < counts[blk_bucket][:, None]
    src = jnp.clip(start[blk_bucket][:, None] + j, 0, n_tok - 1)
    tok = order[src.reshape(-1)].reshape(src.shape)
    row_tok = jnp.where(valid, tok, 0).astype(jnp.int32)
    spare = n_tok + jnp.cumsum((~valid).reshape(-1).astype(jnp.int32)).reshape(valid.shape) - 1
    row_dst = jnp.where(valid, tok, spare).astype(jnp.int32)
    row_gate = jnp.where(valid[..., None], gate_lh[row_tok.reshape(-1)].reshape(valid.shape + (2,)), 0.0)
    row_gate = jnp.pad(row_gate, ((0, 0), (0, 0), (0, 6)))
    pair_a = jnp.array([p[0] for p in _PAIRS], jnp.int32)
    pair_b = jnp.array([p[1] for p in _PAIRS], jnp.int32)
    blk_grp = blk_bucket // n_pairs
    blk_lo = (blk_grp * EXPERTS_PER_GROUP + pair_a[blk_bucket % n_pairs]).astype(jnp.int32)
    blk_hi = (blk_grp * EXPERTS_PER_GROUP + pair_b[blk_bucket % n_pairs]).astype(jnp.int32)
    n_used = (end_pad[-1:] // MOE_ROWS).astype(jnp.int32)
    return row_tok, row_dst, row_gate, blk_lo, blk_hi, n_used


def _combine_kernel(x1_ref, y_ref, g2_ref, nw_ref, o_ref):
    tm = x1_ref.shape[1]
    x2 = x1_ref[0] + g2_ref[0] * _load_token_tiles(y_ref, (), tm)
    ms = jnp.mean(x2 * x2, axis=-1, keepdims=True)
    o_ref[0] = x2 * lax.rsqrt(ms + EPS) * nw_ref[...]


def _combine(x1, y_tiles, g2, final_norm):
    b, l, d = x1.shape
    tm = min(512, l)
    nt = l // tm
    row = lambda bi, i: (bi, i, 0)
    return pl.pallas_call(
        _combine_kernel,
        grid=(b, nt),
        in_specs=[pl.BlockSpec((1, tm, d), row),
                  pl.BlockSpec((tm * TILE_ROWS, LANES), lambda bi, i: (bi * nt + i, 0)),
                  pl.BlockSpec((1, 1, d), lambda bi, i: (bi, 0, 0)),
                  pl.BlockSpec((1, d), lambda bi, i: (0, 0))],
        out_specs=pl.BlockSpec((1, tm, d), row),
        out_shape=jax.ShapeDtypeStruct((b, l, d), F32),
        compiler_params=_params("parallel", "arbitrary"),
        name="moe_combine",
    )(x1, y_tiles, g2, final_norm.reshape(1, d))


def _group_major(v):
    return v.reshape(2, N_GROUPS, HEADS_PER_GROUP).transpose(1, 0, 2).reshape(N_GROUPS, 1, 2 * HEADS_PER_GROUP)


def kernel(x, c, ctx, c_ctx, w_mod, b_mod, norm1, w_in, conv_w, conv_b, dt_bias, a_log, d_skip, ssd_norm,
           w_four, w_out, norm2, w_rg, b_rg, w_re, b_re, w_eg, w_eu, w_ed, final_norm):
    bsz, seq, d = x.shape
    n_tok = bsz * seq
    d_ssd = N_GROUPS * HEADS_PER_GROUP * HEADDIM
    conv_dim = d_ssd + 2 * N_GROUPS * D_STATE
    n_heads = N_GROUPS * HEADS_PER_GROUP
    layer = 0

    c_rows = jnp.zeros((16, d), F32).at[:bsz].set(c).at[bsz].set(c_ctx)
    mod = _modulation(c_rows, w_mod[layer], b_mod[layer])
    sh1, sc1, g1, sh2, sc2, g2 = [m[:bsz, None, :] for m in jnp.split(mod, 6, axis=-1)]
    sh1c, sc1c = [jnp.broadcast_to(m[bsz][None, None, :], (bsz, 1, d)) for m in jnp.split(mod, 6, axis=-1)[:2]]

    w = w_in[layer]
    wz = w[:, :d_ssd].astype(BF16)
    wx = w[:, d_ssd:d_ssd + conv_dim].astype(BF16)
    wdt = w[:, d_ssd + conv_dim:d_ssd + conv_dim + 2 * n_heads]
    wdt = wdt.reshape(d, 2, N_GROUPS, HEADS_PER_GROUP).transpose(0, 2, 1, 3).reshape(d, 2 * n_heads)
    wdt = jnp.pad(wdt, ((0, 0), (0, 128 - 2 * n_heads))).astype(BF16)
    wf = w[:, d_ssd + conv_dim + 2 * n_heads:].astype(BF16)
    n1 = norm1[layer].reshape(1, d)

    dtb = _group_major(dt_bias[layer])
    alog = _group_major(a_log[layer])
    dsk = jnp.repeat(d_skip[layer], HEADDIM).reshape(1, d_ssd)
    cw = conv_w[layer]
    cb = conv_b[layer].reshape(1, conv_dim)

    xbc_c, dt_c = _in_projection(ctx, sh1c, sc1c, n1, wx, wdt, cw, cb)
    h_zero = jnp.zeros((bsz, N_GROUPS, HEADS_PER_GROUP, D_STATE, HEADDIM), F32)
    z_dummy = jnp.zeros((bsz, 8, d_ssd), BF16)
    _, hf_c, hb_c = _ssd_mixer(xbc_c, dt_c, z_dummy, dtb, alog, dsk, h_zero, h_zero, emit_y=False)

    xbc_l, dt_l, z_l, f_l = _in_projection(x, sh1, sc1, n1, wx, wdt, cw, cb, wz, wf)
    y_l, _, _ = _ssd_mixer(xbc_l, dt_l, z_l, dtb, alog, dsk, hf_c, hb_c, emit_y=True)
    four = _fourier_mixer(f_l, w_four[layer])

    wo = w_out[layer]
    wr = jnp.concatenate([w_rg[layer], w_re[layer].transpose(1, 0, 2).reshape(d, N_EXPERTS)], axis=1)
    wr = jnp.pad(wr, ((0, 0), (0, ROUTE_LANES - wr.shape[1])))
    br = jnp.pad(jnp.concatenate([b_rg[layer], b_re[layer].reshape(-1)]),
                 (0, ROUTE_LANES - N_EXPERT_GROUPS - N_EXPERTS)).reshape(1, ROUTE_LANES)
    x1, h, eid, gate = _outproj_router(
        y_l, four, x, g1, sh2, sc2, ssd_norm[layer].reshape(1, d_ssd), norm2[layer].reshape(1, d),
        wo[:d_ssd].astype(BF16), wo[d_ssd:].astype(BF16), wr, br)

    row_tok, row_dst, row_gate, blk_lo, blk_hi, n_used = _route_tables(
        eid.reshape(n_tok, 8), gate.reshape(n_tok, 8), n_tok)
    y_tiles = _experts(h.reshape(n_tok * TILE_ROWS, LANES), row_tok, row_dst, row_gate, blk_lo, blk_hi, n_used,
                       w_eg[layer], w_eu[layer], w_ed[layer])
    return _combine(x1, y_tiles, g2, final_norm)
```

```python
import functools
import math

import jax
import jax.numpy as jnp
from jax import lax
from jax.experimental import pallas as pl
from jax.experimental.pallas import tpu as pltpu

F32 = jnp.float32
BF16 = jnp.bfloat16
HIGHEST = lax.Precision.HIGHEST

EPS = 1e-6
LOG2E = 1.4426950408889634
CHUNK = 128
N_GROUPS = 4
HEADS_PER_GROUP = 4
HEADDIM = 64
D_STATE = 128
CONV_W = 5
CONV_HALO = 16
N_FOURIER_GROUPS = 4
DFT_SPLIT = 64
DFT_EXTRA = 16
FOURIER_CH = 128
N_EXPERT_GROUPS = 4
EXPERTS_PER_GROUP = 8
N_EXPERTS = N_EXPERT_GROUPS * EXPERTS_PER_GROUP
ROUTE_LANES = 128
MOE_ROWS = 128
VMEM_LIMIT_BYTES = 56 * 1024 * 1024


def _params(*sem):
    return pltpu.CompilerParams(dimension_semantics=sem, vmem_limit_bytes=VMEM_LIMIT_BYTES)


def _silu(v):
    h = 0.5 * v
    return h + h * jnp.tanh(h)


LANES = 128
TILE_ROWS = 8


def _store_token_tiles(ref, lead, val):
    rows = val.shape[0]
    for j in range(TILE_ROWS):
        ref[lead + (pl.ds(j, rows, stride=TILE_ROWS), slice(None))] = val[:, LANES * j:LANES * (j + 1)]


def _load_token_tiles(ref, lead, rows):
    return jnp.concatenate(
        [ref[lead + (pl.ds(j, rows, stride=TILE_ROWS), slice(None))] for j in range(TILE_ROWS)], axis=1)


def _dot_split3(a, b):
    a_hi = a.astype(BF16)
    a_lo = (a - a_hi.astype(F32)).astype(BF16)
    b_hi = b.astype(BF16)
    b_lo = (b - b_hi.astype(F32)).astype(BF16)
    dot = functools.partial(jnp.dot, preferred_element_type=F32)
    return dot(a_hi, b_hi) + (dot(a_lo, b_hi) + dot(a_hi, b_lo))


def _mod_kernel(c_ref, w_ref, b_ref, o_ref):
    s = _silu(c_ref[...])
    o_ref[...] = jnp.dot(s, w_ref[...], preferred_element_type=F32, precision=HIGHEST) + b_ref[...]


def _modulation(c_rows, w_mod, b_mod):
    rows, d = c_rows.shape
    n = w_mod.shape[1]
    tn = 512
    return pl.pallas_call(
        _mod_kernel,
        grid=(n // tn,),
        in_specs=[pl.BlockSpec((rows, d), lambda j: (0, 0)),
                  pl.BlockSpec((d, tn), lambda j: (0, j)),
                  pl.BlockSpec((1, tn), lambda j: (0, j))],
        out_specs=pl.BlockSpec((rows, tn), lambda j: (0, j)),
        out_shape=jax.ShapeDtypeStruct((rows, n), F32),
        compiler_params=_params("arbitrary"),
        name="modulation",
    )(c_rows, w_mod, b_mod.reshape(1, n))


def _inproj_kernel(xp_ref, x_ref, xn_ref, sh_ref, sc_ref, nw_ref, wx_ref, wdt_ref, cw_ref, cb_ref, *rest,
                   with_zf):
    if with_zf:
        wz_ref, wf_ref, xbc_ref, dt_ref, z_ref, f_ref, pre_s = rest
    else:
        xbc_ref, dt_ref, pre_s = rest
    tm = x_ref.shape[1]
    i = pl.program_id(1)
    xa = jnp.concatenate([xp_ref[0], x_ref[0], xn_ref[0]], axis=0)
    ms = jnp.mean(xa * xa, axis=-1, keepdims=True)
    xnorm = xa * lax.rsqrt(ms + EPS) * nw_ref[...]
    xm_all = (xnorm * (1.0 + sc_ref[0]) + sh_ref[0]).astype(BF16)
    xm = xm_all[CONV_HALO:CONV_HALO + tm]

    pre_s[...] = jnp.dot(xm_all, wx_ref[...], preferred_element_type=F32)
    top, bot = slice(0, CONV_HALO), slice(CONV_HALO + tm, 2 * CONV_HALO + tm)
    pre_s[top, :] = pre_s[top, :] * (i > 0).astype(F32)
    pre_s[bot, :] = pre_s[bot, :] * (i < pl.num_programs(1) - 1).astype(F32)

    rblk, cblk = 64, 256
    for c0 in range(0, wx_ref.shape[1], cblk):
        cols = slice(c0, c0 + cblk)
        w = cw_ref[:, cols]
        bias = jnp.broadcast_to(cb_ref[:, cols], (rblk, cblk))
        for r0 in range(0, tm, rblk):
            lo = r0 + CONV_HALO - 8
            win = pre_s[lo:lo + rblk + 16, cols]
            acc = bias
            for k in range(CONV_W):
                off = 8 - CONV_W // 2 + k
                acc = acc + w[k:k + 1, :] * win[off:off + rblk, :]
            xbc_ref[0, r0:r0 + rblk, cols] = _silu(acc).astype(BF16)

    dt = jnp.dot(xm, wdt_ref[...], preferred_element_type=F32)
    for g in range(N_GROUPS):
        dt_ref[0, g] = dt[:, 8 * g:8 * g + 8]
    if with_zf:
        z_ref[0] = jnp.dot(xm, wz_ref[...], preferred_element_type=F32).astype(BF16)
        f_ref[0] = jnp.dot(xm, wf_ref[...], preferred_element_type=F32).astype(BF16)


def _in_projection(x, shift, scale, norm_w, wx, wdt, conv_w, conv_b, wz=None, wf=None):
    b, l, d = x.shape
    tm = min(512, l)
    with_zf = wz is not None
    hb = tm // CONV_HALO
    n_hb = l // CONV_HALO
    row = lambda bi, i: (bi, i, 0)
    vec = lambda bi, i: (bi, 0, 0)
    const = lambda bi, i: (0, 0)
    in_specs = [pl.BlockSpec((1, CONV_HALO, d), lambda bi, i: (bi, jnp.maximum(i * hb - 1, 0), 0)),
                pl.BlockSpec((1, tm, d), row),
                pl.BlockSpec((1, CONV_HALO, d), lambda bi, i: (bi, jnp.minimum((i + 1) * hb, n_hb - 1), 0)),
                pl.BlockSpec((1, 1, d), vec),
                pl.BlockSpec((1, 1, d), vec),
                pl.BlockSpec((1, d), const),
                pl.BlockSpec(wx.shape, const),
                pl.BlockSpec(wdt.shape, const),
                pl.BlockSpec(conv_w.shape, const),
                pl.BlockSpec(conv_b.shape, const)]
    args = [x, x, x, shift, scale, norm_w, wx, wdt, conv_w, conv_b]
    out_specs = [pl.BlockSpec((1, tm, wx.shape[1]), row),
                 pl.BlockSpec((1, N_GROUPS, tm, 8), lambda bi, i: (bi, 0, i, 0))]
    out_shape = [jax.ShapeDtypeStruct((b, l, wx.shape[1]), BF16),
                 jax.ShapeDtypeStruct((b, N_GROUPS, l, 8), F32)]
    if with_zf:
        in_specs += [pl.BlockSpec(wz.shape, const), pl.BlockSpec(wf.shape, const)]
        args += [wz, wf]
        out_specs += [pl.BlockSpec((1, tm, wz.shape[1]), row), pl.BlockSpec((1, tm, wf.shape[1]), row)]
        out_shape += [jax.ShapeDtypeStruct((b, l, wz.shape[1]), BF16),
                      jax.ShapeDtypeStruct((b, l, wf.shape[1]), BF16)]
    return pl.pallas_call(
        functools.partial(_inproj_kernel, with_zf=with_zf),
        grid=(b, l // tm),
        in_specs=in_specs,
        out_specs=out_specs,
        out_shape=out_shape,
        scratch_shapes=[pltpu.VMEM((tm + 2 * CONV_HALO, wx.shape[1]), F32)],
        compiler_params=_params("parallel", "arbitrary"),
        name="in_projection_zf" if with_zf else "in_projection",
    )(*args)


def _ssd_kernel(xs_ref, xb_ref, xc_ref, dt_ref, z_ref,
                dtb_ref, alog_ref, dsk_ref, h0f_ref, h0b_ref,
                y_ref, hf_ref, hb_ref,
                dt_s, cs_s, cst_s, bt_s, cb_s, yf_s, hst_s, *, seq, emit_y):
    nc = seq // CHUNK

    dtr = dt_ref[0, 0] + dtb_ref[0]
    dtv = jnp.maximum(dtr, 0.0) + jnp.log(1.0 + jnp.exp(-jnp.abs(dtr)))
    dt_s[...] = dtv
    a_log2 = -jnp.exp(alog_ref[0]) * LOG2E

    ri = lax.broadcasted_iota(jnp.int32, (CHUNK, CHUNK), 0)
    ci = lax.broadcasted_iota(jnp.int32, (CHUNK, CHUNK), 1)
    tri_fwd = ci <= ri
    tri_bwd = ci >= ri
    is_fwd_lane8 = lax.broadcasted_iota(jnp.int32, (CHUNK, 8), 1) < HEADS_PER_GROUP
    lane_pad = jnp.zeros((CHUNK, CHUNK - 8), F32)

    tri16 = tri_fwd.astype(BF16)

    def tables(c, carry):
        r0 = pl.multiple_of(c * CHUNK, CHUNK)
        rows = pl.ds(r0, CHUNK)
        da = dt_s[rows, :] * a_log2
        p0 = da.astype(BF16)
        r1 = da - p0.astype(F32)
        p1 = r1.astype(BF16)
        p2 = (r1 - p1.astype(F32)).astype(BF16)
        packed = jnp.concatenate([p0, p1, p2, jnp.zeros((CHUNK, CHUNK - 24), BF16)], axis=1)
        acc = jnp.dot(tri16, packed, preferred_element_type=F32)
        cs_f = acc[:, 0:8] + acc[:, 8:16] + acc[:, 16:24]
        cs_b = cs_f[CHUNK - 1:CHUNK, :] - cs_f + da
        cs = jnp.where(is_fwd_lane8, cs_f, cs_b)
        cs_s[rows, :] = cs
        cs_t = jnp.concatenate([cs, lane_pad], axis=1).T[:8, :]
        dt_t = jnp.concatenate([dt_s[rows, :], lane_pad], axis=1).T[:8, :]
        cst_s[c] = cs_t - jnp.log2(dt_t)
        bt = xb_ref[0, rows, :].astype(F32).T
        bt_s[c] = bt
        cb_s[c] = jnp.dot(xc_ref[0, rows, :], bt.astype(BF16), preferred_element_type=F32)
        return carry

    lax.fori_loop(0, nc, tables, 0, unroll=2)

    for j in range(HEADS_PER_GROUP):
        hst_s[0, j] = h0f_ref[0, 0, j]
        hst_s[1, j] = h0b_ref[0, 0, j]

    def chunk_dir(d, c, final):
        tri = tri_fwd if d == 0 else tri_bwd
        tot_row = CHUNK - 1 if d == 0 else 0
        r0 = pl.multiple_of(c * CHUNK, CHUNK)
        rows = pl.ds(r0, CHUNK)
        xs = xs_ref[0, rows, :]
        cc = xc_ref[0, rows, :].astype(F32)
        cs = cs_s[rows, :]
        cs_t = cst_s[c]
        bt = bt_s[c]
        cb = cb_s[c]
        ys = []
        for j in range(HEADS_PER_GROUP):
            lane = HEADS_PER_GROUP * d + j
            a_col1 = cs[:, lane:lane + 1]
            a_col = jnp.broadcast_to(a_col1, (CHUNK, CHUNK))
            a_tot = a_col1[tot_row:tot_row + 1, :]
            a_row = cs_t[lane:lane + 1, :]
            g = (cb * jnp.where(tri, jnp.exp2(a_col - a_row), 0.0)).astype(BF16)
            cea = (cc * jnp.exp2(a_col)).astype(BF16)
            xh = xs[:, HEADDIM * j:HEADDIM * (j + 1)]
            h = hst_s[d, j]
            y_h = (jnp.dot(g, xh, preferred_element_type=F32)
                   + jnp.dot(cea, h.astype(BF16), preferred_element_type=F32))
            w_row = jnp.exp2(a_tot - a_row)
            s_new = jnp.dot((bt * w_row).astype(BF16), xh, preferred_element_type=F32)
            hst_s[d, j] = h * jnp.exp2(a_tot) + s_new
            ys.append(y_h)
        if not emit_y:
            return
        y_c = jnp.concatenate(ys, axis=1)
        if d == 0:
            y_c = y_c + dsk_ref[...] * xs.astype(F32)
        if final:
            zc = z_ref[0, rows, :].astype(F32)
            y_ref[0, rows, :] = ((yf_s[rows, :] + y_c) * _silu(zc)).astype(BF16)
        else:
            yf_s[rows, :] = y_c

    def first_half(t, carry):
        chunk_dir(0, t, False)
        chunk_dir(1, nc - 1 - t, False)
        return carry

    def second_half(t, carry):
        chunk_dir(0, t, True)
        chunk_dir(1, nc - 1 - t, True)
        return carry

    lax.fori_loop(0, nc // 2, first_half, 0)
    lax.fori_loop(nc // 2, nc, second_half, 0)
    for j in range(HEADS_PER_GROUP):
        hf_ref[0, 0, j] = hst_s[0, j]
        hb_ref[0, 0, j] = hst_s[1, j]
    if not emit_y:
        y_ref[...] = jnp.zeros(y_ref.shape, y_ref.dtype)


def _ssd_mixer(xbc, dt, z, dtb, alog, dskip, h0f, h0b, emit_y):
    b, l, _ = xbc.shape
    gw = HEADS_PER_GROUP * HEADDIM
    nxb = (N_GROUPS * gw) // D_STATE
    y_rows = l if emit_y else 8
    nc = l // CHUNK
    st_shape = (b, N_GROUPS, HEADS_PER_GROUP, D_STATE, HEADDIM)
    st_spec = pl.BlockSpec((1, 1, HEADS_PER_GROUP, D_STATE, HEADDIM), lambda bi, g: (bi, g, 0, 0, 0))
    in_specs = [
        pl.BlockSpec((1, l, gw), lambda bi, g: (bi, 0, g)),
        pl.BlockSpec((1, l, D_STATE), lambda bi, g: (bi, 0, nxb + g)),
        pl.BlockSpec((1, l, D_STATE), lambda bi, g: (bi, 0, nxb + N_GROUPS + g)),
        pl.BlockSpec((1, 1, l, 8), lambda bi, g: (bi, g, 0, 0)),
        pl.BlockSpec((1, y_rows, gw), lambda bi, g: (bi, 0, g)),
        pl.BlockSpec((1, 1, 8), lambda bi, g: (g, 0, 0)),
        pl.BlockSpec((1, 1, 8), lambda bi, g: (g, 0, 0)),
        pl.BlockSpec((1, gw), lambda bi, g: (0, g)),
        st_spec, st_spec,
    ]
    out_specs = [pl.BlockSpec((1, y_rows, gw), lambda bi, g: (bi, 0, g)), st_spec, st_spec]
    out_shape = [jax.ShapeDtypeStruct((b, y_rows, N_GROUPS * gw), BF16),
                 jax.ShapeDtypeStruct(st_shape, F32), jax.ShapeDtypeStruct(st_shape, F32)]
    scratch = [pltpu.VMEM((l, 8), F32), pltpu.VMEM((l, 8), F32),
               pltpu.VMEM((nc, 8, CHUNK), F32),
               pltpu.VMEM((nc, D_STATE, CHUNK), F32), pltpu.VMEM((nc, CHUNK, CHUNK), F32),
               pltpu.VMEM((y_rows, gw), F32),
               pltpu.VMEM((2, HEADS_PER_GROUP, D_STATE, HEADDIM), F32)]
    return pl.pallas_call(
        functools.partial(_ssd_kernel, seq=l, emit_y=emit_y),
        grid=(b, N_GROUPS),
        in_specs=in_specs, out_specs=out_specs, out_shape=out_shape,
        scratch_shapes=scratch,
        compiler_params=_params("parallel", "arbitrary"),
        name="ssd_mixer" if emit_y else "ssd_mixer_ctx",
    )(xbc, xbc, xbc, dt, z, dtb, alog, dskip, h0f, h0b)


def _fourier_kernel(f_ref, cc_ref, sc_ref, w_ref, ca_ref, sa_ref, cb_ref, sb_ref, rev_ref, o_ref,
                    u_s, v_s, cl_s, sl_s, *, seq, rows):
    i = pl.program_id(1)
    n_blk = o_ref.shape[1]

    @pl.when(i == 0)
    def _():
        scale = 1.0 / math.sqrt(seq * FOURIER_CH)
        for g in range(N_FOURIER_GROUPS):
            w = w_ref[g]
            a = jnp.dot(cc_ref[...], w, preferred_element_type=F32, precision=HIGHEST) * scale
            bm = jnp.dot(sc_ref[...], w, preferred_element_type=F32, precision=HIGHEST) * scale
            cols = slice(FOURIER_CH * g, FOURIER_CH * (g + 1))
            fg = f_ref[0, :, cols]
            u_s[:, cols] = jnp.dot(fg, a.astype(BF16), preferred_element_type=F32).astype(BF16)
            v_s[:, cols] = jnp.dot(fg, bm.astype(BF16), preferred_element_type=F32).astype(BF16)

    k2_0 = i * (rows // DFT_SPLIT)
    cbeta = cb_ref[...]
    sbeta = sb_ref[...]
    for j in range(rows // DFT_SPLIT + 1):
        n_sub = DFT_SPLIT if j < rows // DFT_SPLIT else DFT_EXTRA
        calpha = ca_ref[pl.ds(k2_0 + j, 1), :]
        salpha = sa_ref[pl.ds(k2_0 + j, 1), :]
        sub = slice(DFT_SPLIT * j, DFT_SPLIT * j + n_sub)
        cl_s[sub, :] = (calpha * cbeta[:n_sub] - salpha * sbeta[:n_sub]).astype(BF16)
        sl_s[sub, :] = (salpha * cbeta[:n_sub] + calpha * sbeta[:n_sub]).astype(BF16)
    p = jnp.dot(cl_s[...], u_s[...], preferred_element_type=F32)
    q = jnp.dot(sl_s[...], v_s[...], preferred_element_type=F32)
    o_ref[0, i] = (p - q)[:rows].astype(BF16)
    o_ref[0, n_blk - 1 - i] = jnp.dot(rev_ref[...], (p + q).astype(BF16),
                                      preferred_element_type=F32).astype(BF16)


def _dft_tables(n, dtype):
    k = lax.broadcasted_iota(jnp.int32, (n, n), 0)
    l = lax.broadcasted_iota(jnp.int32, (n, n), 1)
    ang = ((k * l) % n).astype(F32) * (2.0 * math.pi / n)
    return jnp.cos(ang).astype(dtype), jnp.sin(ang).astype(dtype)


def _dft_factor_tables(n):
    n2 = n // DFT_SPLIT
    k2 = lax.broadcasted_iota(jnp.int32, (n2, n), 0)
    k1 = lax.broadcasted_iota(jnp.int32, (DFT_SPLIT, n), 0)
    alpha = ((k2 * lax.broadcasted_iota(jnp.int32, (n2, n), 1)) % n2).astype(F32) * (2.0 * math.pi / n2)
    beta = ((k1 * lax.broadcasted_iota(jnp.int32, (DFT_SPLIT, n), 1)) % n).astype(F32) * (2.0 * math.pi / n)
    return jnp.cos(alpha), jnp.sin(alpha), jnp.cos(beta), jnp.sin(beta)


def _fourier_mixer(f, w_four):
    b, l, df = f.shape
    tr = 512
    n_blk = l // tr
    assert l % (2 * tr) == 0
    cc, sc = _dft_tables(FOURIER_CH, F32)
    ca, sa, cb, sb = _dft_factor_tables(l)
    rev = (lax.broadcasted_iota(jnp.int32, (tr, tr + DFT_EXTRA), 0)
           + lax.broadcasted_iota(jnp.int32, (tr, tr + DFT_EXTRA), 1) == tr).astype(BF16)
    const = lambda bi, i: (0, 0)
    out = pl.pallas_call(
        functools.partial(_fourier_kernel, seq=l, rows=tr),
        grid=(b, n_blk // 2),
        in_specs=[pl.BlockSpec((1, l, df), lambda bi, i: (bi, 0, 0)),
                  pl.BlockSpec((FOURIER_CH, FOURIER_CH), const),
                  pl.BlockSpec((FOURIER_CH, FOURIER_CH), const),
                  pl.BlockSpec(w_four.shape, lambda bi, i: (0, 0, 0)),
                  pl.BlockSpec(ca.shape, const), pl.BlockSpec(sa.shape, const),
                  pl.BlockSpec(cb.shape, const), pl.BlockSpec(sb.shape, const),
                  pl.BlockSpec(rev.shape, const)],
        out_specs=pl.BlockSpec((1, n_blk, tr, df), lambda bi, i: (bi, 0, 0, 0)),
        out_shape=jax.ShapeDtypeStruct((b, n_blk, tr, df), BF16),
        scratch_shapes=[pltpu.VMEM((l, df), BF16), pltpu.VMEM((l, df), BF16),
                        pltpu.VMEM((tr + DFT_EXTRA, l), BF16), pltpu.VMEM((tr + DFT_EXTRA, l), BF16)],
        compiler_params=_params("parallel", "arbitrary"),
        name="fourier_mixer",
    )(f, cc, sc, w_four, ca, sa, cb, sb, rev)
    return out.reshape(b, l, df)


def _outproj_router_kernel(y_ref, four_ref, x_ref, g1_ref, sh_ref, sc_ref, nssd_ref, n2_ref,
                           wos_ref, wof_ref, wr_ref, br_ref,
                           x1_ref, h_ref, eid_ref, gate_ref):
    y = y_ref[0].astype(F32)
    ms = jnp.mean(y * y, axis=-1, keepdims=True)
    yn = (y * lax.rsqrt(ms + EPS) * nssd_ref[...]).astype(BF16)
    mix = (jnp.dot(yn, wos_ref[...], preferred_element_type=F32)
           + jnp.dot(four_ref[0], wof_ref[...], preferred_element_type=F32))
    x1 = x_ref[0] + g1_ref[0] * mix
    x1_ref[0] = x1
    ms2 = jnp.mean(x1 * x1, axis=-1, keepdims=True)
    h = (x1 * lax.rsqrt(ms2 + EPS) * n2_ref[...]) * (1.0 + sc_ref[0]) + sh_ref[0]
    _store_token_tiles(h_ref, (0,), h)

    lg = _dot_split3(h, wr_ref[...]) + br_ref[...]
    tm = lg.shape[0]
    lane = lax.broadcasted_iota(jnp.int32, (tm, ROUTE_LANES), 1)
    lane_f = lane.astype(F32)
    neg = jnp.float32(-1e30)
    big = jnp.float32(1e9)
    is_grp = lane < N_EXPERT_GROUPS
    gl = jnp.where(is_grp, lg, neg)
    gmax = jnp.max(gl, axis=-1, keepdims=True)
    gsum = jnp.sum(jnp.where(is_grp, jnp.exp(gl - gmax), 0.0), axis=-1, keepdims=True)
    grp = jnp.min(jnp.where(gl == gmax, lane_f, big), axis=-1, keepdims=True)
    p_grp = 1.0 / gsum
    lo = N_EXPERT_GROUPS + EXPERTS_PER_GROUP * grp
    in_grp = jnp.logical_and(lane_f >= lo, lane_f < lo + EXPERTS_PER_GROUP)
    el = jnp.where(in_grp, lg, neg)
    m1 = jnp.max(el, axis=-1, keepdims=True)
    i1 = jnp.min(jnp.where(el == m1, lane_f, big), axis=-1, keepdims=True)
    el2 = jnp.where(lane_f == i1, neg, el)
    m2 = jnp.max(el2, axis=-1, keepdims=True)
    i2 = jnp.min(jnp.where(el2 == m2, lane_f, big), axis=-1, keepdims=True)
    e2 = jnp.exp(m2 - m1)
    den = 1.0 + e2
    gate1 = p_grp / den
    gate2 = p_grp * e2 / den
    lane8 = lax.broadcasted_iota(jnp.int32, (tm, 8), 1)
    eid = jnp.where(lane8 == 0, i1 - N_EXPERT_GROUPS, jnp.where(lane8 == 1, i2 - N_EXPERT_GROUPS, 0.0))
    eid_ref[0] = eid.astype(jnp.int32)
    gate_ref[0] = jnp.where(lane8 == 0, gate1, jnp.where(lane8 == 1, gate2, 0.0))


def _outproj_router(y, four, x, g1, sh2, sc2, nssd, n2, wos, wof, wr, br):
    b, l, d = x.shape
    tm = min(512, l)
    row = lambda bi, i: (bi, i, 0)
    vec = lambda bi, i: (bi, 0, 0)
    const = lambda bi, i: (0, 0)
    return pl.pallas_call(
        _outproj_router_kernel,
        grid=(b, l // tm),
        in_specs=[pl.BlockSpec((1, tm, y.shape[2]), row),
                  pl.BlockSpec((1, tm, four.shape[2]), row),
                  pl.BlockSpec((1, tm, d), row),
                  pl.BlockSpec((1, 1, d), vec), pl.BlockSpec((1, 1, d), vec), pl.BlockSpec((1, 1, d), vec),
                  pl.BlockSpec((1, y.shape[2]), const), pl.BlockSpec((1, d), const),
                  pl.BlockSpec(wos.shape, const), pl.BlockSpec(wof.shape, const),
                  pl.BlockSpec(wr.shape, const), pl.BlockSpec(br.shape, const)],
        out_specs=[pl.BlockSpec((1, tm, d), row), pl.BlockSpec((1, tm * TILE_ROWS, LANES), row),
                   pl.BlockSpec((1, tm, 8), row), pl.BlockSpec((1, tm, 8), row)],
        out_shape=[jax.ShapeDtypeStruct((b, l, d), F32), jax.ShapeDtypeStruct((b, l * TILE_ROWS, LANES), F32),
                   jax.ShapeDtypeStruct((b, l, 8), jnp.int32), jax.ShapeDtypeStruct((b, l, 8), F32)],
        compiler_params=_params("parallel", "arbitrary"),
        name="outproj_router",
    )(y, four, x, g1, sh2, sc2, nssd, n2, wos, wof, wr, br)


def _expert_kernel(blo_ref, bhi_ref, nused_ref, tokc_ref, tokn_ref, dstp_ref, dstc_ref, gate_ref, h_hbm,
                   wgl_ref, wul_ref, wdl_ref, wgh_ref, wuh_ref, wdh_ref,
                   y_hbm, hbuf, ybuf, actl_s, acth_s, sem_in, sem_out):
    i = pl.program_id(0)
    n_used = nused_ref[0]
    active = i < n_used
    is_last = i == n_used - 1
    slot = i % 2
    other = 1 - slot
    de = wgl_ref.shape[2]
    d = wdl_ref.shape[2]
    n_phase = 8
    rows_per_phase = MOE_ROWS // n_phase
    blk_tile_rows = MOE_ROWS * TILE_ROWS

    def tile(ref, lead, row0):
        return ref.at[lead + (pl.ds(row0, TILE_ROWS), slice(None))]

    def start_gather(tok_ref, s, r):
        src = tile(h_hbm, (), pl.multiple_of(tok_ref[0, 0, r], TILE_ROWS))
        pltpu.make_async_copy(src, tile(hbuf, (s,), r * TILE_ROWS), sem_in.at[s]).start(priority=r % 2)

    def start_scatter(dst_ref, s, r):
        dst = tile(y_hbm, (), pl.multiple_of(dst_ref[0, 0, r], TILE_ROWS))
        pltpu.make_async_copy(tile(ybuf, (s,), r * TILE_ROWS), dst, sem_out.at[s]).start(priority=r % 2)

    def wait_gather(s):
        pltpu.make_async_copy(h_hbm.at[pl.ds(0, blk_tile_rows), :], hbuf.at[s], sem_in.at[s]).wait()

    def wait_scatter(s):
        pltpu.make_async_copy(ybuf.at[s], y_hbm.at[pl.ds(0, blk_tile_rows), :], sem_out.at[s]).wait()

    @pl.when(i == 0)
    def _():
        for r in range(MOE_ROWS):
            start_gather(tokc_ref, 0, r)

    @pl.when(active)
    def _():
        wait_gather(slot)

    @pl.when(jnp.logical_and(active, i >= 2))
    def _():
        wait_scatter(slot)

    def step(with_scatter, cur):
        nxt = 1 - cur
        xb = _load_token_tiles(hbuf, (cur,), MOE_ROWS).astype(BF16)
        gate = gate_ref[0]
        g_lo = gate[:, 0:1]
        g_hi = gate[:, 1:2]
        half = de // 2
        quarter = d // 4

        def hidden(wg_ref, wu_ref, act_s, c):
            cols = slice(c * half, (c + 1) * half)
            gact = jnp.dot(xb, wg_ref[0, :, cols], preferred_element_type=F32)
            up = jnp.dot(xb, wu_ref[0, :, cols], preferred_element_type=F32)
            act_s[:, cols] = (_silu(gact) * up).astype(BF16)

        ys = []

        def output(c):
            cols = slice(c * quarter, (c + 1) * quarter)
            ys.append(g_lo * jnp.dot(actl_s[...], wdl_ref[0, :, cols], preferred_element_type=F32)
                      + g_hi * jnp.dot(acth_s[...], wdh_ref[0, :, cols], preferred_element_type=F32))

        work = [functools.partial(hidden, wgl_ref, wul_ref, actl_s, 0), functools.partial(hidden, wgl_ref, wul_ref, actl_s, 1),
                functools.partial(hidden, wgh_ref, wuh_ref, acth_s, 0), functools.partial(hidden, wgh_ref, wuh_ref, acth_s, 1),
                functools.partial(output, 0), functools.partial(output, 1),
                functools.partial(output, 2), functools.partial(output, 3)]
        for p in range(n_phase):
            for r in range(p * rows_per_phase, (p + 1) * rows_per_phase):
                start_gather(tokn_ref, nxt, r)
                if with_scatter:
                    start_scatter(dstp_ref, nxt, r)
            work[p]()
        _store_token_tiles(ybuf, (cur,), jnp.concatenate(ys, axis=1))

    @pl.when(i == 0)
    def _():
        step(False, 0)

    @pl.when(jnp.logical_and(active, jnp.logical_and(i > 0, slot == 0)))
    def _():
        step(True, 0)

    @pl.when(jnp.logical_and(active, slot == 1))
    def _():
        step(True, 1)

    @pl.when(is_last)
    def _():
        for r in range(MOE_ROWS):
            start_scatter(dstc_ref, slot, r)
        wait_gather(other)
        wait_scatter(other)
        wait_scatter(slot)

    @pl.when(jnp.logical_not(active))
    def _():
        ybuf[0] = jnp.zeros((blk_tile_rows, LANES), F32)
        dst0 = pl.multiple_of(dstc_ref[0, 0, 0], TILE_ROWS)
        fill = pltpu.make_async_copy(ybuf.at[0], y_hbm.at[pl.ds(dst0, blk_tile_rows), :], sem_out.at[0])
        fill.start()
        fill.wait()


def _cast_kernel(w_ref, o_ref):
    o_ref[...] = w_ref[...].astype(o_ref.dtype)


def _to_bf16(w):
    blk = pl.BlockSpec((1,) + w.shape[1:], lambda e: (e, 0, 0))
    return pl.pallas_call(_cast_kernel, grid=(w.shape[0],), in_specs=[blk], out_specs=blk,
                          out_shape=jax.ShapeDtypeStruct(w.shape, BF16),
                          compiler_params=_params("arbitrary"), name="expert_weights_bf16")(w)


def _experts(h_tiles, row_tok, row_dst, row_gate, blk_lo, blk_hi, n_used, w_eg, w_eu, w_ed):
    n_blocks = blk_lo.shape[0]
    d, de = w_eg.shape[1], w_eg.shape[2]
    assert d == TILE_ROWS * LANES
    idx_shape = (n_blocks, 1, MOE_ROWS)
    idx_block = (1, 1, MOE_ROWS)
    smem = pltpu.SMEM
    blk_tile_rows = MOE_ROWS * TILE_ROWS
    lo = lambda i, bl, bh, nu: (bl[i], 0, 0)
    hi = lambda i, bl, bh, nu: (bh[i], 0, 0)
    grid_spec = pltpu.PrefetchScalarGridSpec(
        num_scalar_prefetch=3,
        grid=(n_blocks,),
        in_specs=[pl.BlockSpec(idx_block, lambda i, bl, bh, nu: (i, 0, 0), memory_space=smem),
                  pl.BlockSpec(idx_block, lambda i, bl, bh, nu: (jnp.minimum(i + 1, n_blocks - 1), 0, 0),
                               memory_space=smem),
                  pl.BlockSpec(idx_block, lambda i, bl, bh, nu: (jnp.maximum(i - 1, 0), 0, 0), memory_space=smem),
                  pl.BlockSpec(idx_block, lambda i, bl, bh, nu: (i, 0, 0), memory_space=smem),
                  pl.BlockSpec((1, MOE_ROWS, 8), lambda i, bl, bh, nu: (i, 0, 0)),
                  pl.BlockSpec(memory_space=pl.ANY),
                  pl.BlockSpec((1, d, de), lo), pl.BlockSpec((1, d, de), lo), pl.BlockSpec((1, de, d), lo),
                  pl.BlockSpec((1, d, de), hi), pl.BlockSpec((1, d, de), hi), pl.BlockSpec((1, de, d), hi)],
        out_specs=pl.BlockSpec(memory_space=pl.ANY),
        scratch_shapes=[pltpu.VMEM((2, blk_tile_rows, LANES), F32), pltpu.VMEM((2, blk_tile_rows, LANES), F32),
                        pltpu.VMEM((MOE_ROWS, de), BF16), pltpu.VMEM((MOE_ROWS, de), BF16),
                        pltpu.SemaphoreType.DMA((2,)), pltpu.SemaphoreType.DMA((2,))],
    )
    row_tok = (row_tok * TILE_ROWS).reshape(idx_shape)
    row_dst = (row_dst * TILE_ROWS).reshape(idx_shape)
    return pl.pallas_call(
        _expert_kernel,
        grid_spec=grid_spec,
        out_shape=jax.ShapeDtypeStruct((n_blocks * blk_tile_rows, LANES), F32),
        compiler_params=_params("arbitrary"),
        name="moe_experts",
    )(blk_lo, blk_hi, n_used, row_tok, row_tok, row_dst, row_dst, row_gate, h_tiles,
      w_eg, w_eu, w_ed, w_eg, w_eu, w_ed)


_PAIRS = [(a, b) for a in range(EXPERTS_PER_GROUP) for b in range(a + 1, EXPERTS_PER_GROUP)]


def _route_tables(eid, gate, n_tok):
    n_pairs = len(_PAIRS)
    n_buckets = N_EXPERT_GROUPS * n_pairs
    e1, e2 = eid[:, 0], eid[:, 1]
    first_is_lo = e1 <= e2
    lo = jnp.where(first_is_lo, e1, e2)
    hi = jnp.where(first_is_lo, e2, e1)
    gate_lo = jnp.where(first_is_lo, gate[:, 0], gate[:, 1])
    gate_hi = jnp.where(first_is_lo, gate[:, 1], gate[:, 0])
    a = lo % EXPERTS_PER_GROUP
    b = hi % EXPERTS_PER_GROUP
    pair = a * (2 * EXPERTS_PER_GROUP - 1 - a) // 2 + (b - a - 1)
    bucket = (lo // EXPERTS_PER_GROUP) * n_pairs + pair
    order = jnp.argsort(bucket).astype(jnp.int32)
    ids = jnp.arange(n_buckets, dtype=jnp.int32)
    counts = jnp.sum((bucket[:, None] == ids[None, :]).astype(jnp.int32), axis=0)
    start = jnp.cumsum(counts) - counts
    padded = (counts + MOE_ROWS - 1) // MOE_ROWS * MOE_ROWS
    end_pad = jnp.cumsum(padded)
    start_pad = end_pad - padded
    n_blocks = -(-(n_tok + n_buckets * (MOE_ROWS - 1)) // MOE_ROWS)
    n_blocks += n_blocks % 2
    blk_row0 = jnp.arange(n_blocks, dtype=jnp.int32) * MOE_ROWS
    blk_bucket = jnp.minimum(jnp.sum((end_pad[None, :] <= blk_row0[:, None]).astype(jnp.int32), axis=1),
                             n_buckets - 1).astype(jnp.int32)
    j = (blk_row0 - start_pad[blk_bucket])[:, None] + jnp.arange(MOE_ROWS, dtype=jnp.int32)[None, :]
    valid = j < counts[blk_bucket][:, None]
    src = jnp.clip(start[blk_bucket][:, None] + j, 0, n_tok - 1)
    tok = order[src.reshape(-1)].reshape(src.shape)
    row_tok = jnp.where(valid, tok, 0).astype(jnp.int32)
    spare = n_tok + jnp.cumsum((~valid).reshape(-1).astype(jnp.int32)).reshape(valid.shape) - 1
    row_dst = jnp.where(valid, tok, spare).astype(jnp.int32)
    flat_tok = row_tok.reshape(-1)
    zero = jnp.zeros(valid.shape, F32)
    row_gate = jnp.stack([jnp.where(valid, gate_lo[flat_tok].reshape(valid.shape), 0.0),
                          jnp.where(valid, gate_hi[flat_tok].reshape(valid.shape), 0.0)] + [zero] * 6, axis=-1)
    pair_a = jnp.array([p[0] for p in _PAIRS], jnp.int32)
    pair_b = jnp.array([p[1] for p in _PAIRS], jnp.int32)
    blk_grp = blk_bucket // n_pairs
    blk_lo = (blk_grp * EXPERTS_PER_GROUP + pair_a[blk_bucket % n_pairs]).astype(jnp.int32)
    blk_hi = (blk_grp * EXPERTS_PER_GROUP + pair_b[blk_bucket % n_pairs]).astype(jnp.int32)
    n_used = (end_pad[-1:] // MOE_ROWS).astype(jnp.int32)
    return row_tok, row_dst, row_gate, blk_lo, blk_hi, n_used


def _combine_kernel(x1_ref, y_ref, g2_ref, nw_ref, o_ref):
    tm = x1_ref.shape[1]
    x2 = x1_ref[0] + g2_ref[0] * _load_token_tiles(y_ref, (), tm)
    ms = jnp.mean(x2 * x2, axis=-1, keepdims=True)
    o_ref[0] = x2 * lax.rsqrt(ms + EPS) * nw_ref[...]


def _combine(x1, y_tiles, g2, final_norm):
    b, l, d = x1.shape
    tm = min(512, l)
    nt = l // tm
    row = lambda bi, i: (bi, i, 0)
    return pl.pallas_call(
        _combine_kernel,
        grid=(b, nt),
        in_specs=[pl.BlockSpec((1, tm, d), row),
                  pl.BlockSpec((tm * TILE_ROWS, LANES), lambda bi, i: (bi * nt + i, 0)),
                  pl.BlockSpec((1, 1, d), lambda bi, i: (bi, 0, 0)),
                  pl.BlockSpec((1, d), lambda bi, i: (0, 0))],
        out_specs=pl.BlockSpec((1, tm, d), row),
        out_shape=jax.ShapeDtypeStruct((b, l, d), F32),
        compiler_params=_params("parallel", "arbitrary"),
        name="moe_combine",
    )(x1, y_tiles, g2, final_norm.reshape(1, d))


def _group_major(v):
    return v.reshape(2, N_GROUPS, HEADS_PER_GROUP).transpose(1, 0, 2).reshape(N_GROUPS, 1, 2 * HEADS_PER_GROUP)


def kernel(x, c, ctx, c_ctx, w_mod, b_mod, norm1, w_in, conv_w, conv_b, dt_bias, a_log, d_skip, ssd_norm,
           w_four, w_out, norm2, w_rg, b_rg, w_re, b_re, w_eg, w_eu, w_ed, final_norm):
    bsz, seq, d = x.shape
    n_tok = bsz * seq
    d_ssd = N_GROUPS * HEADS_PER_GROUP * HEADDIM
    conv_dim = d_ssd + 2 * N_GROUPS * D_STATE
    n_heads = N_GROUPS * HEADS_PER_GROUP
    layer = 0

    c_rows = jnp.zeros((16, d), F32).at[:bsz].set(c).at[bsz].set(c_ctx)
    mod = _modulation(c_rows, w_mod[layer], b_mod[layer])
    sh1, sc1, g1, sh2, sc2, g2 = [m[:bsz, None, :] for m in jnp.split(mod, 6, axis=-1)]
    sh1c, sc1c = [jnp.broadcast_to(m[bsz][None, None, :], (bsz, 1, d)) for m in jnp.split(mod, 6, axis=-1)[:2]]

    w = w_in[layer]
    wz = w[:, :d_ssd].astype(BF16)
    wx = w[:, d_ssd:d_ssd + conv_dim].astype(BF16)
    wdt = w[:, d_ssd + conv_dim:d_ssd + conv_dim + 2 * n_heads]
    wdt = wdt.reshape(d, 2, N_GROUPS, HEADS_PER_GROUP).transpose(0, 2, 1, 3).reshape(d, 2 * n_heads)
    wdt = jnp.pad(wdt, ((0, 0), (0, 128 - 2 * n_heads))).astype(BF16)
    wf = w[:, d_ssd + conv_dim + 2 * n_heads:].astype(BF16)
    n1 = norm1[layer].reshape(1, d)

    dtb = _group_major(dt_bias[layer])
    alog = _group_major(a_log[layer])
    dsk = jnp.repeat(d_skip[layer], HEADDIM).reshape(1, d_ssd)
    cw = conv_w[layer]
    cb = conv_b[layer].reshape(1, conv_dim)

    xbc_c, dt_c = _in_projection(ctx, sh1c, sc1c, n1, wx, wdt, cw, cb)
    h_zero = jnp.zeros((bsz, N_GROUPS, HEADS_PER_GROUP, D_STATE, HEADDIM), F32)
    z_dummy = jnp.zeros((bsz, 8, d_ssd), BF16)
    _, hf_c, hb_c = _ssd_mixer(xbc_c, dt_c, z_dummy, dtb, alog, dsk, h_zero, h_zero, emit_y=False)

    xbc_l, dt_l, z_l, f_l = _in_projection(x, sh1, sc1, n1, wx, wdt, cw, cb, wz, wf)
    y_l, _, _ = _ssd_mixer(xbc_l, dt_l, z_l, dtb, alog, dsk, hf_c, hb_c, emit_y=True)
    four = _fourier_mixer(f_l, w_four[layer])

    wo = w_out[layer]
    wr = jnp.concatenate([w_rg[layer], w_re[layer].transpose(1, 0, 2).reshape(d, N_EXPERTS)], axis=1)
    wr = jnp.pad(wr, ((0, 0), (0, ROUTE_LANES - wr.shape[1])))
    br = jnp.pad(jnp.concatenate([b_rg[layer], b_re[layer].reshape(-1)]),
                 (0, ROUTE_LANES - N_EXPERT_GROUPS - N_EXPERTS)).reshape(1, ROUTE_LANES)
    x1, h, eid, gate = _outproj_router(
        y_l, four, x, g1, sh2, sc2, ssd_norm[layer].reshape(1, d_ssd), norm2[layer].reshape(1, d),
        wo[:d_ssd].astype(BF16), wo[d_ssd:].astype(BF16), wr, br)

    row_tok, row_dst, row_gate, blk_lo, blk_hi, n_used = _route_tables(
        eid.reshape(n_tok, 8), gate.reshape(n_tok, 8), n_tok)
    y_tiles = _experts(h.reshape(n_tok * TILE_ROWS, LANES), row_tok, row_dst, row_gate, blk_lo, blk_hi, n_used,
                       _to_bf16(w_eg[layer]), _to_bf16(w_eu[layer]), _to_bf16(w_ed[layer]))
    return _combine(x1, y_tiles, g2, final_norm)
```

```python
import functools
import math

import jax
import jax.numpy as jnp
from jax import lax
from jax.experimental import pallas as pl
from jax.experimental.pallas import tpu as pltpu

F32 = jnp.float32
BF16 = jnp.bfloat16
HIGHEST = lax.Precision.HIGHEST

EPS = 1e-6
LOG2E = 1.4426950408889634
CHUNK = 128
N_GROUPS = 4
HEADS_PER_GROUP = 4
HEADDIM = 64
D_STATE = 128
CONV_W = 5
CONV_HALO = 16
N_FOURIER_GROUPS = 4
DFT_SPLIT = 64
DFT_EXTRA = 16
FOURIER_CH = 128
N_EXPERT_GROUPS = 4
EXPERTS_PER_GROUP = 8
N_EXPERTS = N_EXPERT_GROUPS * EXPERTS_PER_GROUP
ROUTE_LANES = 128
MOE_ROWS = 256
VMEM_LIMIT_BYTES = 56 * 1024 * 1024


def _params(*sem):
    return pltpu.CompilerParams(dimension_semantics=sem, vmem_limit_bytes=VMEM_LIMIT_BYTES)


def _silu(v):
    h = 0.5 * v
    return h + h * jnp.tanh(h)


LANES = 128
TILE_ROWS = 8


def _store_token_tiles(ref, lead, val):
    rows = val.shape[0]
    for j in range(TILE_ROWS):
        ref[lead + (pl.ds(j, rows, stride=TILE_ROWS), slice(None))] = val[:, LANES * j:LANES * (j + 1)]


def _load_token_tiles(ref, lead, rows):
    return jnp.concatenate(
        [ref[lead + (pl.ds(j, rows, stride=TILE_ROWS), slice(None))] for j in range(TILE_ROWS)], axis=1)


def _dot_split3(a, b):
    a_hi = a.astype(BF16)
    a_lo = (a - a_hi.astype(F32)).astype(BF16)
    b_hi = b.astype(BF16)
    b_lo = (b - b_hi.astype(F32)).astype(BF16)
    dot = functools.partial(jnp.dot, preferred_element_type=F32)
    return dot(a_hi, b_hi) + (dot(a_lo, b_hi) + dot(a_hi, b_lo))


def _mod_kernel(c_ref, w_ref, b_ref, o_ref):
    s = _silu(c_ref[...])
    o_ref[...] = jnp.dot(s, w_ref[...], preferred_element_type=F32, precision=HIGHEST) + b_ref[...]


def _modulation(c_rows, w_mod, b_mod):
    rows, d = c_rows.shape
    n = w_mod.shape[1]
    tn = 512
    return pl.pallas_call(
        _mod_kernel,
        grid=(n // tn,),
        in_specs=[pl.BlockSpec((rows, d), lambda j: (0, 0)),
                  pl.BlockSpec((d, tn), lambda j: (0, j)),
                  pl.BlockSpec((1, tn), lambda j: (0, j))],
        out_specs=pl.BlockSpec((rows, tn), lambda j: (0, j)),
        out_shape=jax.ShapeDtypeStruct((rows, n), F32),
        compiler_params=_params("arbitrary"),
        name="modulation",
    )(c_rows, w_mod, b_mod.reshape(1, n))


def _inproj_kernel(xp_ref, x_ref, xn_ref, sh_ref, sc_ref, nw_ref, wx_ref, wdt_ref, cw_ref, cb_ref, *rest,
                   with_zf):
    if with_zf:
        wz_ref, wf_ref, xbc_ref, dt_ref, z_ref, f_ref, pre_s = rest
    else:
        xbc_ref, dt_ref, pre_s = rest
    tm = x_ref.shape[1]
    i = pl.program_id(1)
    xa = jnp.concatenate([xp_ref[0], x_ref[0], xn_ref[0]], axis=0)
    ms = jnp.mean(xa * xa, axis=-1, keepdims=True)
    xnorm = xa * lax.rsqrt(ms + EPS) * nw_ref[...]
    xm_all = (xnorm * (1.0 + sc_ref[0]) + sh_ref[0]).astype(BF16)
    xm = xm_all[CONV_HALO:CONV_HALO + tm]

    pre_s[...] = jnp.dot(xm_all, wx_ref[...], preferred_element_type=F32)
    top, bot = slice(0, CONV_HALO), slice(CONV_HALO + tm, 2 * CONV_HALO + tm)
    pre_s[top, :] = pre_s[top, :] * (i > 0).astype(F32)
    pre_s[bot, :] = pre_s[bot, :] * (i < pl.num_programs(1) - 1).astype(F32)

    rblk, cblk = 64, 256
    for c0 in range(0, wx_ref.shape[1], cblk):
        cols = slice(c0, c0 + cblk)
        w = cw_ref[:, cols]
        bias = jnp.broadcast_to(cb_ref[:, cols], (rblk, cblk))
        for r0 in range(0, tm, rblk):
            lo = r0 + CONV_HALO - 8
            win = pre_s[lo:lo + rblk + 16, cols]
            acc = bias
            for k in range(CONV_W):
                off = 8 - CONV_W // 2 + k
                acc = acc + w[k:k + 1, :] * win[off:off + rblk, :]
            xbc_ref[0, r0:r0 + rblk, cols] = _silu(acc).astype(BF16)

    dt = jnp.dot(xm, wdt_ref[...], preferred_element_type=F32)
    for g in range(N_GROUPS):
        dt_ref[0, g] = dt[:, 8 * g:8 * g + 8]
    if with_zf:
        z_ref[0] = jnp.dot(xm, wz_ref[...], preferred_element_type=F32).astype(BF16)
        f_ref[0] = jnp.dot(xm, wf_ref[...], preferred_element_type=F32).astype(BF16)


def _in_projection(x, shift, scale, norm_w, wx, wdt, conv_w, conv_b, wz=None, wf=None):
    b, l, d = x.shape
    tm = min(512, l)
    with_zf = wz is not None
    hb = tm // CONV_HALO
    n_hb = l // CONV_HALO
    row = lambda bi, i: (bi, i, 0)
    vec = lambda bi, i: (bi, 0, 0)
    const = lambda bi, i: (0, 0)
    in_specs = [pl.BlockSpec((1, CONV_HALO, d), lambda bi, i: (bi, jnp.maximum(i * hb - 1, 0), 0)),
                pl.BlockSpec((1, tm, d), row),
                pl.BlockSpec((1, CONV_HALO, d), lambda bi, i: (bi, jnp.minimum((i + 1) * hb, n_hb - 1), 0)),
                pl.BlockSpec((1, 1, d), vec),
                pl.BlockSpec((1, 1, d), vec),
                pl.BlockSpec((1, d), const),
                pl.BlockSpec(wx.shape, const),
                pl.BlockSpec(wdt.shape, const),
                pl.BlockSpec(conv_w.shape, const),
                pl.BlockSpec(conv_b.shape, const)]
    args = [x, x, x, shift, scale, norm_w, wx, wdt, conv_w, conv_b]
    out_specs = [pl.BlockSpec((1, tm, wx.shape[1]), row),
                 pl.BlockSpec((1, N_GROUPS, tm, 8), lambda bi, i: (bi, 0, i, 0))]
    out_shape = [jax.ShapeDtypeStruct((b, l, wx.shape[1]), BF16),
                 jax.ShapeDtypeStruct((b, N_GROUPS, l, 8), F32)]
    if with_zf:
        in_specs += [pl.BlockSpec(wz.shape, const), pl.BlockSpec(wf.shape, const)]
        args += [wz, wf]
        out_specs += [pl.BlockSpec((1, tm, wz.shape[1]), row), pl.BlockSpec((1, tm, wf.shape[1]), row)]
        out_shape += [jax.ShapeDtypeStruct((b, l, wz.shape[1]), BF16),
                      jax.ShapeDtypeStruct((b, l, wf.shape[1]), BF16)]
    return pl.pallas_call(
        functools.partial(_inproj_kernel, with_zf=with_zf),
        grid=(b, l // tm),
        in_specs=in_specs,
        out_specs=out_specs,
        out_shape=out_shape,
        scratch_shapes=[pltpu.VMEM((tm + 2 * CONV_HALO, wx.shape[1]), F32)],
        compiler_params=_params("parallel", "arbitrary"),
        name="in_projection_zf" if with_zf else "in_projection",
    )(*args)


def _ssd_kernel(xs_ref, xb_ref, xc_ref, dt_ref, z_ref,
                dtb_ref, alog_ref, dsk_ref, h0f_ref, h0b_ref,
                y_ref, hf_ref, hb_ref,
                dt_s, cs_s, cst_s, bt_s, cb_s, yf_s, hst_s, *, seq, emit_y):
    nc = seq // CHUNK

    dtr = dt_ref[0, 0] + dtb_ref[0]
    dtv = jnp.maximum(dtr, 0.0) + jnp.log(1.0 + jnp.exp(-jnp.abs(dtr)))
    dt_s[...] = dtv
    a_log2 = -jnp.exp(alog_ref[0]) * LOG2E

    ri = lax.broadcasted_iota(jnp.int32, (CHUNK, CHUNK), 0)
    ci = lax.broadcasted_iota(jnp.int32, (CHUNK, CHUNK), 1)
    tri_fwd = ci <= ri
    tri_bwd = ci >= ri
    is_fwd_lane8 = lax.broadcasted_iota(jnp.int32, (CHUNK, 8), 1) < HEADS_PER_GROUP
    lane_pad = jnp.zeros((CHUNK, CHUNK - 8), F32)

    tri16 = tri_fwd.astype(BF16)

    def tables(c, carry):
        r0 = pl.multiple_of(c * CHUNK, CHUNK)
        rows = pl.ds(r0, CHUNK)
        da = dt_s[rows, :] * a_log2
        p0 = da.astype(BF16)
        r1 = da - p0.astype(F32)
        p1 = r1.astype(BF16)
        p2 = (r1 - p1.astype(F32)).astype(BF16)
        packed = jnp.concatenate([p0, p1, p2, jnp.zeros((CHUNK, CHUNK - 24), BF16)], axis=1)
        acc = jnp.dot(tri16, packed, preferred_element_type=F32)
        cs_f = acc[:, 0:8] + acc[:, 8:16] + acc[:, 16:24]
        cs_b = cs_f[CHUNK - 1:CHUNK, :] - cs_f + da
        cs = jnp.where(is_fwd_lane8, cs_f, cs_b)
        cs_s[rows, :] = cs
        cs_t = jnp.concatenate([cs, lane_pad], axis=1).T[:8, :]
        dt_t = jnp.concatenate([dt_s[rows, :], lane_pad], axis=1).T[:8, :]
        cst_s[c] = cs_t - jnp.log2(dt_t)
        bt = xb_ref[0, rows, :].astype(F32).T
        bt_s[c] = bt
        cb_s[c] = jnp.dot(xc_ref[0, rows, :], bt.astype(BF16), preferred_element_type=F32)
        return carry

    lax.fori_loop(0, nc, tables, 0, unroll=2)

    for j in range(HEADS_PER_GROUP):
        hst_s[0, j] = h0f_ref[0, 0, j]
        hst_s[1, j] = h0b_ref[0, 0, j]

    def chunk_dir(d, c, final):
        tri = tri_fwd if d == 0 else tri_bwd
        tot_row = CHUNK - 1 if d == 0 else 0
        r0 = pl.multiple_of(c * CHUNK, CHUNK)
        rows = pl.ds(r0, CHUNK)
        xs = xs_ref[0, rows, :]
        cc = xc_ref[0, rows, :].astype(F32)
        cs = cs_s[rows, :]
        cs_t = cst_s[c]
        bt = bt_s[c]
        cb = cb_s[c]
        ys = []
        for j in range(HEADS_PER_GROUP):
            lane = HEADS_PER_GROUP * d + j
            a_col1 = cs[:, lane:lane + 1]
            a_col = jnp.broadcast_to(a_col1, (CHUNK, CHUNK))
            a_tot = a_col1[tot_row:tot_row + 1, :]
            a_row = cs_t[lane:lane + 1, :]
            g = (cb * jnp.where(tri, jnp.exp2(a_col - a_row), 0.0)).astype(BF16)
            cea = (cc * jnp.exp2(a_col)).astype(BF16)
            xh = xs[:, HEADDIM * j:HEADDIM * (j + 1)]
            h = hst_s[d, j]
            y_h = (jnp.dot(g, xh, preferred_element_type=F32)
                   + jnp.dot(cea, h.astype(BF16), preferred_element_type=F32))
            w_row = jnp.exp2(a_tot - a_row)
            s_new = jnp.dot((bt * w_row).astype(BF16), xh, preferred_element_type=F32)
            hst_s[d, j] = h * jnp.exp2(a_tot) + s_new
            ys.append(y_h)
        if not emit_y:
            return
        y_c = jnp.concatenate(ys, axis=1)
        if d == 0:
            y_c = y_c + dsk_ref[...] * xs.astype(F32)
        if final:
            zc = z_ref[0, rows, :].astype(F32)
            y_ref[0, rows, :] = ((yf_s[rows, :] + y_c) * _silu(zc)).astype(BF16)
        else:
            yf_s[rows, :] = y_c

    def first_half(t, carry):
        chunk_dir(0, t, False)
        chunk_dir(1, nc - 1 - t, False)
        return carry

    def second_half(t, carry):
        chunk_dir(0, t, True)
        chunk_dir(1, nc - 1 - t, True)
        return carry

    lax.fori_loop(0, nc // 2, first_half, 0)
    lax.fori_loop(nc // 2, nc, second_half, 0)
    for j in range(HEADS_PER_GROUP):
        hf_ref[0, 0, j] = hst_s[0, j]
        hb_ref[0, 0, j] = hst_s[1, j]
    if not emit_y:
        y_ref[...] = jnp.zeros(y_ref.shape, y_ref.dtype)


def _ssd_mixer(xbc, dt, z, dtb, alog, dskip, h0f, h0b, emit_y):
    b, l, _ = xbc.shape
    gw = HEADS_PER_GROUP * HEADDIM
    nxb = (N_GROUPS * gw) // D_STATE
    y_rows = l if emit_y else 8
    nc = l // CHUNK
    st_shape = (b, N_GROUPS, HEADS_PER_GROUP, D_STATE, HEADDIM)
    st_spec = pl.BlockSpec((1, 1, HEADS_PER_GROUP, D_STATE, HEADDIM), lambda bi, g: (bi, g, 0, 0, 0))
    in_specs = [
        pl.BlockSpec((1, l, gw), lambda bi, g: (bi, 0, g)),
        pl.BlockSpec((1, l, D_STATE), lambda bi, g: (bi, 0, nxb + g)),
        pl.BlockSpec((1, l, D_STATE), lambda bi, g: (bi, 0, nxb + N_GROUPS + g)),
        pl.BlockSpec((1, 1, l, 8), lambda bi, g: (bi, g, 0, 0)),
        pl.BlockSpec((1, y_rows, gw), lambda bi, g: (bi, 0, g)),
        pl.BlockSpec((1, 1, 8), lambda bi, g: (g, 0, 0)),
        pl.BlockSpec((1, 1, 8), lambda bi, g: (g, 0, 0)),
        pl.BlockSpec((1, gw), lambda bi, g: (0, g)),
        st_spec, st_spec,
    ]
    out_specs = [pl.BlockSpec((1, y_rows, gw), lambda bi, g: (bi, 0, g)), st_spec, st_spec]
    out_shape = [jax.ShapeDtypeStruct((b, y_rows, N_GROUPS * gw), BF16),
                 jax.ShapeDtypeStruct(st_shape, F32), jax.ShapeDtypeStruct(st_shape, F32)]
    scratch = [pltpu.VMEM((l, 8), F32), pltpu.VMEM((l, 8), F32),
               pltpu.VMEM((nc, 8, CHUNK), F32),
               pltpu.VMEM((nc, D_STATE, CHUNK), F32), pltpu.VMEM((nc, CHUNK, CHUNK), F32),
               pltpu.VMEM((y_rows, gw), F32),
               pltpu.VMEM((2, HEADS_PER_GROUP, D_STATE, HEADDIM), F32)]
    return pl.pallas_call(
        functools.partial(_ssd_kernel, seq=l, emit_y=emit_y),
        grid=(b, N_GROUPS),
        in_specs=in_specs, out_specs=out_specs, out_shape=out_shape,
        scratch_shapes=scratch,
        compiler_params=_params("parallel", "arbitrary"),
        name="ssd_mixer" if emit_y else "ssd_mixer_ctx",
    )(xbc, xbc, xbc, dt, z, dtb, alog, dskip, h0f, h0b)


def _fourier_kernel(f_ref, cc_ref, sc_ref, w_ref, ca_ref, sa_ref, cb_ref, sb_ref, rev_ref, o_ref,
                    u_s, v_s, cl_s, sl_s, *, seq, rows):
    i = pl.program_id(1)
    n_blk = o_ref.shape[1]

    @pl.when(i == 0)
    def _():
        scale = 1.0 / math.sqrt(seq * FOURIER_CH)
        for g in range(N_FOURIER_GROUPS):
            w = w_ref[g]
            a = jnp.dot(cc_ref[...], w, preferred_element_type=F32, precision=HIGHEST) * scale
            bm = jnp.dot(sc_ref[...], w, preferred_element_type=F32, precision=HIGHEST) * scale
            cols = slice(FOURIER_CH * g, FOURIER_CH * (g + 1))
            fg = f_ref[0, :, cols]
            u_s[:, cols] = jnp.dot(fg, a.astype(BF16), preferred_element_type=F32).astype(BF16)
            v_s[:, cols] = jnp.dot(fg, bm.astype(BF16), preferred_element_type=F32).astype(BF16)

    k2_0 = i * (rows // DFT_SPLIT)
    cbeta = cb_ref[...]
    sbeta = sb_ref[...]
    for j in range(rows // DFT_SPLIT + 1):
        n_sub = DFT_SPLIT if j < rows // DFT_SPLIT else DFT_EXTRA
        calpha = ca_ref[pl.ds(k2_0 + j, 1), :]
        salpha = sa_ref[pl.ds(k2_0 + j, 1), :]
        sub = slice(DFT_SPLIT * j, DFT_SPLIT * j + n_sub)
        cl_s[sub, :] = (calpha * cbeta[:n_sub] - salpha * sbeta[:n_sub]).astype(BF16)
        sl_s[sub, :] = (salpha * cbeta[:n_sub] + calpha * sbeta[:n_sub]).astype(BF16)
    p = jnp.dot(cl_s[...], u_s[...], preferred_element_type=F32)
    q = jnp.dot(sl_s[...], v_s[...], preferred_element_type=F32)
    o_ref[0, i] = (p - q)[:rows].astype(BF16)
    o_ref[0, n_blk - 1 - i] = jnp.dot(rev_ref[...], (p + q).astype(BF16),
                                      preferred_element_type=F32).astype(BF16)


def _dft_tables(n, dtype):
    k = lax.broadcasted_iota(jnp.int32, (n, n), 0)
    l = lax.broadcasted_iota(jnp.int32, (n, n), 1)
    ang = ((k * l) % n).astype(F32) * (2.0 * math.pi / n)
    return jnp.cos(ang).astype(dtype), jnp.sin(ang).astype(dtype)


def _dft_factor_tables(n):
    n2 = n // DFT_SPLIT
    k2 = lax.broadcasted_iota(jnp.int32, (n2, n), 0)
    k1 = lax.broadcasted_iota(jnp.int32, (DFT_SPLIT, n), 0)
    alpha = ((k2 * lax.broadcasted_iota(jnp.int32, (n2, n), 1)) % n2).astype(F32) * (2.0 * math.pi / n2)
    beta = ((k1 * lax.broadcasted_iota(jnp.int32, (DFT_SPLIT, n), 1)) % n).astype(F32) * (2.0 * math.pi / n)
    return jnp.cos(alpha), jnp.sin(alpha), jnp.cos(beta), jnp.sin(beta)


def _fourier_mixer(f, w_four):
    b, l, df = f.shape
    tr = 512
    n_blk = l // tr
    assert l % (2 * tr) == 0
    cc, sc = _dft_tables(FOURIER_CH, F32)
    ca, sa, cb, sb = _dft_factor_tables(l)
    rev = (lax.broadcasted_iota(jnp.int32, (tr, tr + DFT_EXTRA), 0)
           + lax.broadcasted_iota(jnp.int32, (tr, tr + DFT_EXTRA), 1) == tr).astype(BF16)
    const = lambda bi, i: (0, 0)
    out = pl.pallas_call(
        functools.partial(_fourier_kernel, seq=l, rows=tr),
        grid=(b, n_blk // 2),
        in_specs=[pl.BlockSpec((1, l, df), lambda bi, i: (bi, 0, 0)),
                  pl.BlockSpec((FOURIER_CH, FOURIER_CH), const),
                  pl.BlockSpec((FOURIER_CH, FOURIER_CH), const),
                  pl.BlockSpec(w_four.shape, lambda bi, i: (0, 0, 0)),
                  pl.BlockSpec(ca.shape, const), pl.BlockSpec(sa.shape, const),
                  pl.BlockSpec(cb.shape, const), pl.BlockSpec(sb.shape, const),
                  pl.BlockSpec(rev.shape, const)],
        out_specs=pl.BlockSpec((1, n_blk, tr, df), lambda bi, i: (bi, 0, 0, 0)),
        out_shape=jax.ShapeDtypeStruct((b, n_blk, tr, df), BF16),
        scratch_shapes=[pltpu.VMEM((l, df), BF16), pltpu.VMEM((l, df), BF16),
                        pltpu.VMEM((tr + DFT_EXTRA, l), BF16), pltpu.VMEM((tr + DFT_EXTRA, l), BF16)],
        compiler_params=_params("parallel", "arbitrary"),
        name="fourier_mixer",
    )(f, cc, sc, w_four, ca, sa, cb, sb, rev)
    return out.reshape(b, l, df)


def _outproj_router_kernel(y_ref, four_ref, x_ref, g1_ref, sh_ref, sc_ref, nssd_ref, n2_ref,
                           wos_ref, wof_ref, wr_ref, br_ref,
                           x1_ref, h_ref, eid_ref, gate_ref):
    y = y_ref[0].astype(F32)
    ms = jnp.mean(y * y, axis=-1, keepdims=True)
    yn = (y * lax.rsqrt(ms + EPS) * nssd_ref[...]).astype(BF16)
    mix = (jnp.dot(yn, wos_ref[...], preferred_element_type=F32)
           + jnp.dot(four_ref[0], wof_ref[...], preferred_element_type=F32))
    x1 = x_ref[0] + g1_ref[0] * mix
    x1_ref[0] = x1
    ms2 = jnp.mean(x1 * x1, axis=-1, keepdims=True)
    h = (x1 * lax.rsqrt(ms2 + EPS) * n2_ref[...]) * (1.0 + sc_ref[0]) + sh_ref[0]
    _store_token_tiles(h_ref, (0,), h)

    lg = _dot_split3(h, wr_ref[...]) + br_ref[...]
    tm = lg.shape[0]
    lane = lax.broadcasted_iota(jnp.int32, (tm, ROUTE_LANES), 1)
    lane_f = lane.astype(F32)
    neg = jnp.float32(-1e30)
    big = jnp.float32(1e9)
    is_grp = lane < N_EXPERT_GROUPS
    gl = jnp.where(is_grp, lg, neg)
    gmax = jnp.max(gl, axis=-1, keepdims=True)
    gsum = jnp.sum(jnp.where(is_grp, jnp.exp(gl - gmax), 0.0), axis=-1, keepdims=True)
    grp = jnp.min(jnp.where(gl == gmax, lane_f, big), axis=-1, keepdims=True)
    p_grp = 1.0 / gsum
    lo = N_EXPERT_GROUPS + EXPERTS_PER_GROUP * grp
    in_grp = jnp.logical_and(lane_f >= lo, lane_f < lo + EXPERTS_PER_GROUP)
    el = jnp.where(in_grp, lg, neg)
    m1 = jnp.max(el, axis=-1, keepdims=True)
    i1 = jnp.min(jnp.where(el == m1, lane_f, big), axis=-1, keepdims=True)
    el2 = jnp.where(lane_f == i1, neg, el)
    m2 = jnp.max(el2, axis=-1, keepdims=True)
    i2 = jnp.min(jnp.where(el2 == m2, lane_f, big), axis=-1, keepdims=True)
    e2 = jnp.exp(m2 - m1)
    den = 1.0 + e2
    gate1 = p_grp / den
    gate2 = p_grp * e2 / den
    lane8 = lax.broadcasted_iota(jnp.int32, (tm, 8), 1)
    eid = jnp.where(lane8 == 0, i1 - N_EXPERT_GROUPS, jnp.where(lane8 == 1, i2 - N_EXPERT_GROUPS, 0.0))
    eid_ref[0] = eid.astype(jnp.int32)
    gate_ref[0] = jnp.where(lane8 == 0, gate1, jnp.where(lane8 == 1, gate2, 0.0))


def _outproj_router(y, four, x, g1, sh2, sc2, nssd, n2, wos, wof, wr, br):
    b, l, d = x.shape
    tm = min(512, l)
    row = lambda bi, i: (bi, i, 0)
    vec = lambda bi, i: (bi, 0, 0)
    const = lambda bi, i: (0, 0)
    return pl.pallas_call(
        _outproj_router_kernel,
        grid=(b, l // tm),
        in_specs=[pl.BlockSpec((1, tm, y.shape[2]), row),
                  pl.BlockSpec((1, tm, four.shape[2]), row),
                  pl.BlockSpec((1, tm, d), row),
                  pl.BlockSpec((1, 1, d), vec), pl.BlockSpec((1, 1, d), vec), pl.BlockSpec((1, 1, d), vec),
                  pl.BlockSpec((1, y.shape[2]), const), pl.BlockSpec((1, d), const),
                  pl.BlockSpec(wos.shape, const), pl.BlockSpec(wof.shape, const),
                  pl.BlockSpec(wr.shape, const), pl.BlockSpec(br.shape, const)],
        out_specs=[pl.BlockSpec((1, tm, d), row), pl.BlockSpec((1, tm * TILE_ROWS, LANES), row),
                   pl.BlockSpec((1, tm, 8), row), pl.BlockSpec((1, tm, 8), row)],
        out_shape=[jax.ShapeDtypeStruct((b, l, d), F32), jax.ShapeDtypeStruct((b, l * TILE_ROWS, LANES), F32),
                   jax.ShapeDtypeStruct((b, l, 8), jnp.int32), jax.ShapeDtypeStruct((b, l, 8), F32)],
        compiler_params=_params("parallel", "arbitrary"),
        name="outproj_router",
    )(y, four, x, g1, sh2, sc2, nssd, n2, wos, wof, wr, br)


def _expert_kernel(bexp_ref, nused_ref, tokc_ref, tokn_ref, dstp_ref, dstc_ref, h_hbm, wg_ref, wu_ref, wd_ref,
                   y_hbm, hbuf, ybuf, act_s, wg_s, wu_s, wd_s, sem_in, sem_out):
    i = pl.program_id(0)
    n_used = nused_ref[0]
    active = i < n_used
    is_last = i == n_used - 1
    slot = i % 2
    other = 1 - slot
    de = wg_s.shape[1]
    d = wd_s.shape[1]
    n_phase = 8
    rows_per_phase = MOE_ROWS // n_phase
    blk_tile_rows = MOE_ROWS * TILE_ROWS

    def tile(ref, lead, row0):
        return ref.at[lead + (pl.ds(row0, TILE_ROWS), slice(None))]

    def start_gather(tok_ref, s, r):
        src = tile(h_hbm, (), pl.multiple_of(tok_ref[0, 0, r], TILE_ROWS))
        pltpu.make_async_copy(src, tile(hbuf, (s,), r * TILE_ROWS), sem_in.at[s]).start(priority=r % 2)

    def start_scatter(dst_ref, s, r):
        dst = tile(y_hbm, (), pl.multiple_of(dst_ref[0, 0, r], TILE_ROWS))
        pltpu.make_async_copy(tile(ybuf, (s,), r * TILE_ROWS), dst, sem_out.at[s]).start(priority=r % 2)

    def wait_gather(s):
        pltpu.make_async_copy(h_hbm.at[pl.ds(0, blk_tile_rows), :], hbuf.at[s], sem_in.at[s]).wait()

    def wait_scatter(s):
        pltpu.make_async_copy(ybuf.at[s], y_hbm.at[pl.ds(0, blk_tile_rows), :], sem_out.at[s]).wait()

    @pl.when(i == 0)
    def _():
        for r in range(MOE_ROWS):
            start_gather(tokc_ref, 0, r)

    prev = bexp_ref[jnp.maximum(i - 1, 0)]

    @pl.when(jnp.logical_and(active, jnp.logical_or(i == 0, bexp_ref[i] != prev)))
    def _():
        wg_s[...] = wg_ref[0].astype(BF16)
        wu_s[...] = wu_ref[0].astype(BF16)
        wd_s[...] = wd_ref[0].astype(BF16)

    @pl.when(active)
    def _():
        wait_gather(slot)

    @pl.when(jnp.logical_and(active, i >= 2))
    def _():
        wait_scatter(slot)

    for nxt in (0, 1):
        @pl.when(jnp.logical_and(active, other == nxt))
        def _(nxt=nxt):
            for r in range(MOE_ROWS):
                start_gather(tokn_ref, nxt, r)

    def step(with_scatter, cur):
        nxt = 1 - cur
        xb = _load_token_tiles(hbuf, (cur,), MOE_ROWS).astype(BF16)
        ys = []
        for p in range(n_phase):
            if with_scatter:
                for r in range(p * rows_per_phase, (p + 1) * rows_per_phase):
                    start_scatter(dstp_ref, nxt, r)
            if p < n_phase // 2:
                w = de // (n_phase // 2)
                cols = slice(p * w, (p + 1) * w)
                gact = jnp.dot(xb, wg_s[:, cols], preferred_element_type=F32)
                up = jnp.dot(xb, wu_s[:, cols], preferred_element_type=F32)
                act_s[:, cols] = (_silu(gact) * up).astype(BF16)
            else:
                w = d // (n_phase // 2)
                cols = slice((p - n_phase // 2) * w, (p - n_phase // 2 + 1) * w)
                ys.append(jnp.dot(act_s[...], wd_s[:, cols], preferred_element_type=F32))
        _store_token_tiles(ybuf, (cur,), jnp.concatenate(ys, axis=1))

    @pl.when(i == 0)
    def _():
        step(False, 0)

    @pl.when(jnp.logical_and(active, jnp.logical_and(i > 0, slot == 0)))
    def _():
        step(True, 0)

    @pl.when(jnp.logical_and(active, slot == 1))
    def _():
        step(True, 1)

    @pl.when(is_last)
    def _():
        for r in range(MOE_ROWS):
            start_scatter(dstc_ref, slot, r)
        wait_gather(other)
        wait_scatter(other)
        wait_scatter(slot)

    @pl.when(jnp.logical_not(active))
    def _():
        ybuf[0] = jnp.zeros((blk_tile_rows, LANES), F32)
        dst0 = pl.multiple_of(dstc_ref[0, 0, 0], TILE_ROWS)
        fill = pltpu.make_async_copy(ybuf.at[0], y_hbm.at[pl.ds(dst0, blk_tile_rows), :], sem_out.at[0])
        fill.start()
        fill.wait()


def _experts(h_tiles, row_tok, row_dst, blk_exp, n_used, w_eg, w_eu, w_ed):
    n_blocks = blk_exp.shape[0]
    d, de = w_eg.shape[1], w_eg.shape[2]
    assert d == TILE_ROWS * LANES
    idx_shape = (n_blocks, 1, MOE_ROWS)
    idx_block = (1, 1, MOE_ROWS)
    smem = pltpu.SMEM
    blk_tile_rows = MOE_ROWS * TILE_ROWS
    grid_spec = pltpu.PrefetchScalarGridSpec(
        num_scalar_prefetch=2,
        grid=(n_blocks,),
        in_specs=[pl.BlockSpec(idx_block, lambda i, be, nu: (i, 0, 0), memory_space=smem),
                  pl.BlockSpec(idx_block, lambda i, be, nu: (jnp.minimum(i + 1, n_blocks - 1), 0, 0),
                               memory_space=smem),
                  pl.BlockSpec(idx_block, lambda i, be, nu: (jnp.maximum(i - 1, 0), 0, 0), memory_space=smem),
                  pl.BlockSpec(idx_block, lambda i, be, nu: (i, 0, 0), memory_space=smem),
                  pl.BlockSpec(memory_space=pl.ANY),
                  pl.BlockSpec((1, d, de), lambda i, be, nu: (be[i], 0, 0)),
                  pl.BlockSpec((1, d, de), lambda i, be, nu: (be[i], 0, 0)),
                  pl.BlockSpec((1, de, d), lambda i, be, nu: (be[i], 0, 0))],
        out_specs=pl.BlockSpec(memory_space=pl.ANY),
        scratch_shapes=[pltpu.VMEM((2, blk_tile_rows, LANES), F32), pltpu.VMEM((2, blk_tile_rows, LANES), F32),
                        pltpu.VMEM((MOE_ROWS, de), BF16),
                        pltpu.VMEM((d, de), BF16), pltpu.VMEM((d, de), BF16), pltpu.VMEM((de, d), BF16),
                        pltpu.SemaphoreType.DMA((2,)), pltpu.SemaphoreType.DMA((2,))],
    )
    row_tok = (row_tok * TILE_ROWS).reshape(idx_shape)
    row_dst = (row_dst * TILE_ROWS).reshape(idx_shape)
    return pl.pallas_call(
        _expert_kernel,
        grid_spec=grid_spec,
        out_shape=jax.ShapeDtypeStruct((n_blocks * blk_tile_rows, LANES), F32),
        compiler_params=_params("arbitrary"),
        name="moe_experts",
    )(blk_exp, n_used, row_tok, row_tok, row_dst, row_dst, h_tiles, w_eg, w_eu, w_ed)


def _route_tables(eid, n_tok):
    n_assign = 2 * n_tok
    e_flat = jnp.concatenate([eid[:, 0], eid[:, 1]])
    order = jnp.argsort(e_flat).astype(jnp.int32)
    experts = jnp.arange(N_EXPERTS, dtype=jnp.int32)
    counts = jnp.sum((e_flat[:, None] == experts[None, :]).astype(jnp.int32), axis=0)
    start = jnp.cumsum(counts) - counts
    padded = (counts + MOE_ROWS - 1) // MOE_ROWS * MOE_ROWS
    end_pad = jnp.cumsum(padded)
    start_pad = end_pad - padded
    n_blocks = -(-(n_assign + N_EXPERTS * (MOE_ROWS - 1)) // MOE_ROWS)
    blk_row0 = jnp.arange(n_blocks, dtype=jnp.int32) * MOE_ROWS
    blk_exp = jnp.minimum(jnp.sum((end_pad[None, :] <= blk_row0[:, None]).astype(jnp.int32), axis=1),
                          N_EXPERTS - 1).astype(jnp.int32)
    j = (blk_row0 - start_pad[blk_exp])[:, None] + jnp.arange(MOE_ROWS, dtype=jnp.int32)[None, :]
    valid = j < counts[blk_exp][:, None]
    src = jnp.clip(start[blk_exp][:, None] + j, 0, n_assign - 1)
    assign = order[src.reshape(-1)].reshape(src.shape)
    row_tok = jnp.where(valid, jnp.where(assign >= n_tok, assign - n_tok, assign), 0)
    spare = n_assign + jnp.cumsum((~valid).reshape(-1).astype(jnp.int32)).reshape(valid.shape) - 1
    row_dst = jnp.where(valid, assign, spare)
    n_used = (end_pad[-1:] // MOE_ROWS).astype(jnp.int32)
    return row_tok.astype(jnp.int32), row_dst.astype(jnp.int32), blk_exp, n_used


def _combine_kernel(x1_ref, y0_ref, y1_ref, gate_ref, g2_ref, nw_ref, o_ref):
    gate = gate_ref[0]
    tm = gate.shape[0]
    moe = (gate[:, 0:1] * _load_token_tiles(y0_ref, (), tm) + gate[:, 1:2] * _load_token_tiles(y1_ref, (), tm))
    x2 = x1_ref[0] + g2_ref[0] * moe
    ms = jnp.mean(x2 * x2, axis=-1, keepdims=True)
    o_ref[0] = x2 * lax.rsqrt(ms + EPS) * nw_ref[...]


def _combine(x1, y_assign, gate, g2, final_norm):
    b, l, d = x1.shape
    tm = min(512, l)
    nt = l // tm
    row = lambda bi, i: (bi, i, 0)
    return pl.pallas_call(
        _combine_kernel,
        grid=(b, nt),
        in_specs=[pl.BlockSpec((1, tm, d), row),
                  pl.BlockSpec((tm * TILE_ROWS, LANES), lambda bi, i: (bi * nt + i, 0)),
                  pl.BlockSpec((tm * TILE_ROWS, LANES), lambda bi, i: (b * nt + bi * nt + i, 0)),
                  pl.BlockSpec((1, tm, 8), row),
                  pl.BlockSpec((1, 1, d), lambda bi, i: (bi, 0, 0)),
                  pl.BlockSpec((1, d), lambda bi, i: (0, 0))],
        out_specs=pl.BlockSpec((1, tm, d), row),
        out_shape=jax.ShapeDtypeStruct((b, l, d), F32),
        compiler_params=_params("parallel", "arbitrary"),
        name="moe_combine",
    )(x1, y_assign, y_assign, gate, g2, final_norm.reshape(1, d))


def _group_major(v):
    return v.reshape(2, N_GROUPS, HEADS_PER_GROUP).transpose(1, 0, 2).reshape(N_GROUPS, 1, 2 * HEADS_PER_GROUP)


def kernel(x, c, ctx, c_ctx, w_mod, b_mod, norm1, w_in, conv_w, conv_b, dt_bias, a_log, d_skip, ssd_norm,
           w_four, w_out, norm2, w_rg, b_rg, w_re, b_re, w_eg, w_eu, w_ed, final_norm):
    bsz, seq, d = x.shape
    n_tok = bsz * seq
    d_ssd = N_GROUPS * HEADS_PER_GROUP * HEADDIM
    conv_dim = d_ssd + 2 * N_GROUPS * D_STATE
    n_heads = N_GROUPS * HEADS_PER_GROUP
    layer = 0

    c_rows = jnp.zeros((16, d), F32).at[:bsz].set(c).at[bsz].set(c_ctx)
    mod = _modulation(c_rows, w_mod[layer], b_mod[layer])
    sh1, sc1, g1, sh2, sc2, g2 = [m[:bsz, None, :] for m in jnp.split(mod, 6, axis=-1)]
    sh1c, sc1c = [jnp.broadcast_to(m[bsz][None, None, :], (bsz, 1, d)) for m in jnp.split(mod, 6, axis=-1)[:2]]

    w = w_in[layer]
    wz = w[:, :d_ssd].astype(BF16)
    wx = w[:, d_ssd:d_ssd + conv_dim].astype(BF16)
    wdt = w[:, d_ssd + conv_dim:d_ssd + conv_dim + 2 * n_heads]
    wdt = wdt.reshape(d, 2, N_GROUPS, HEADS_PER_GROUP).transpose(0, 2, 1, 3).reshape(d, 2 * n_heads)
    wdt = jnp.pad(wdt, ((0, 0), (0, 128 - 2 * n_heads))).astype(BF16)
    wf = w[:, d_ssd + conv_dim + 2 * n_heads:].astype(BF16)
    n1 = norm1[layer].reshape(1, d)

    dtb = _group_major(dt_bias[layer])
    alog = _group_major(a_log[layer])
    dsk = jnp.repeat(d_skip[layer], HEADDIM).reshape(1, d_ssd)
    cw = conv_w[layer]
    cb = conv_b[layer].reshape(1, conv_dim)

    xbc_c, dt_c = _in_projection(ctx, sh1c, sc1c, n1, wx, wdt, cw, cb)
    h_zero = jnp.zeros((bsz, N_GROUPS, HEADS_PER_GROUP, D_STATE, HEADDIM), F32)
    z_dummy = jnp.zeros((bsz, 8, d_ssd), BF16)
    _, hf_c, hb_c = _ssd_mixer(xbc_c, dt_c, z_dummy, dtb, alog, dsk, h_zero, h_zero, emit_y=False)

    xbc_l, dt_l, z_l, f_l = _in_projection(x, sh1, sc1, n1, wx, wdt, cw, cb, wz, wf)
    y_l, _, _ = _ssd_mixer(xbc_l, dt_l, z_l, dtb, alog, dsk, hf_c, hb_c, emit_y=True)
    four = _fourier_mixer(f_l, w_four[layer])

    wo = w_out[layer]
    wr = jnp.concatenate([w_rg[layer], w_re[layer].transpose(1, 0, 2).reshape(d, N_EXPERTS)], axis=1)
    wr = jnp.pad(wr, ((0, 0), (0, ROUTE_LANES - wr.shape[1])))
    br = jnp.pad(jnp.concatenate([b_rg[layer], b_re[layer].reshape(-1)]),
                 (0, ROUTE_LANES - N_EXPERT_GROUPS - N_EXPERTS)).reshape(1, ROUTE_LANES)
    x1, h, eid, gate = _outproj_router(
        y_l, four, x, g1, sh2, sc2, ssd_norm[layer].reshape(1, d_ssd), norm2[layer].reshape(1, d),
        wo[:d_ssd].astype(BF16), wo[d_ssd:].astype(BF16), wr, br)

    row_tok, row_dst, blk_exp, n_used = _route_tables(eid.reshape(n_tok, 8), n_tok)
    y_assign = _experts(h.reshape(n_tok * TILE_ROWS, LANES), row_tok, row_dst, blk_exp, n_used,
                        w_eg[layer], w_eu[layer], w_ed[layer])
    return _combine(x1, y_assign, gate, g2, final_norm)
```

```python
import functools
import math

import jax
import jax.numpy as jnp
from jax import lax
from jax.experimental import pallas as pl
from jax.experimental.pallas import tpu as pltpu

F32 = jnp.float32
BF16 = jnp.bfloat16
HIGHEST = lax.Precision.HIGHEST

EPS = 1e-6
LOG2E = 1.4426950408889634
CHUNK = 128
N_GROUPS = 4
HEADS_PER_GROUP = 4
HEADDIM = 64
D_STATE = 128
CONV_W = 5
CONV_HALO = 16
N_FOURIER_GROUPS = 4
DFT_SPLIT = 64
DFT_EXTRA = 16
FOURIER_CH = 128
N_EXPERT_GROUPS = 4
EXPERTS_PER_GROUP = 8
N_EXPERTS = N_EXPERT_GROUPS * EXPERTS_PER_GROUP
ROUTE_LANES = 128
MOE_ROWS = 256
VMEM_LIMIT_BYTES = 56 * 1024 * 1024


def _params(*sem):
    return pltpu.CompilerParams(dimension_semantics=sem, vmem_limit_bytes=VMEM_LIMIT_BYTES)


def _silu(v):
    h = 0.5 * v
    return h + h * jnp.tanh(h)


LANES = 128
TILE_ROWS = 8


def _store_token_tiles(ref, lead, val):
    rows = val.shape[0]
    for j in range(TILE_ROWS):
        ref[lead + (pl.ds(j, rows, stride=TILE_ROWS), slice(None))] = val[:, LANES * j:LANES * (j + 1)]


def _load_token_tiles(ref, lead, rows):
    return jnp.concatenate(
        [ref[lead + (pl.ds(j, rows, stride=TILE_ROWS), slice(None))] for j in range(TILE_ROWS)], axis=1)


def _dot_split3(a, b):
    a_hi = a.astype(BF16)
    a_lo = (a - a_hi.astype(F32)).astype(BF16)
    b_hi = b.astype(BF16)
    b_lo = (b - b_hi.astype(F32)).astype(BF16)
    dot = functools.partial(jnp.dot, preferred_element_type=F32)
    return dot(a_hi, b_hi) + (dot(a_lo, b_hi) + dot(a_hi, b_lo))


def _mod_kernel(c_ref, w_ref, b_ref, o_ref):
    s = _silu(c_ref[...])
    o_ref[...] = jnp.dot(s, w_ref[...], preferred_element_type=F32, precision=HIGHEST) + b_ref[...]


def _modulation(c_rows, w_mod, b_mod):
    rows, d = c_rows.shape
    n = w_mod.shape[1]
    tn = 512
    return pl.pallas_call(
        _mod_kernel,
        grid=(n // tn,),
        in_specs=[pl.BlockSpec((rows, d), lambda j: (0, 0)),
                  pl.BlockSpec((d, tn), lambda j: (0, j)),
                  pl.BlockSpec((1, tn), lambda j: (0, j))],
        out_specs=pl.BlockSpec((rows, tn), lambda j: (0, j)),
        out_shape=jax.ShapeDtypeStruct((rows, n), F32),
        compiler_params=_params("arbitrary"),
        name="modulation",
    )(c_rows, w_mod, b_mod.reshape(1, n))


def _inproj_kernel(xp_ref, x_ref, xn_ref, sh_ref, sc_ref, nw_ref, wx_ref, wdt_ref, cw_ref, cb_ref, *rest,
                   with_zf):
    if with_zf:
        wz_ref, wf_ref, xbc_ref, dt_ref, z_ref, f_ref, pre_s = rest
    else:
        xbc_ref, dt_ref, pre_s = rest
    tm = x_ref.shape[1]
    i = pl.program_id(1)
    xa = jnp.concatenate([xp_ref[0], x_ref[0], xn_ref[0]], axis=0)
    ms = jnp.mean(xa * xa, axis=-1, keepdims=True)
    xnorm = xa * lax.rsqrt(ms + EPS) * nw_ref[...]
    xm_all = (xnorm * (1.0 + sc_ref[0]) + sh_ref[0]).astype(BF16)
    xm = xm_all[CONV_HALO:CONV_HALO + tm]

    pre_s[...] = jnp.dot(xm_all, wx_ref[...], preferred_element_type=F32)
    top, bot = slice(0, CONV_HALO), slice(CONV_HALO + tm, 2 * CONV_HALO + tm)
    pre_s[top, :] = pre_s[top, :] * (i > 0).astype(F32)
    pre_s[bot, :] = pre_s[bot, :] * (i < pl.num_programs(1) - 1).astype(F32)

    rblk, cblk = 64, 256
    for c0 in range(0, wx_ref.shape[1], cblk):
        cols = slice(c0, c0 + cblk)
        w = cw_ref[:, cols]
        bias = jnp.broadcast_to(cb_ref[:, cols], (rblk, cblk))
        for r0 in range(0, tm, rblk):
            lo = r0 + CONV_HALO - 8
            win = pre_s[lo:lo + rblk + 16, cols]
            acc = bias
            for k in range(CONV_W):
                off = 8 - CONV_W // 2 + k
                acc = acc + w[k:k + 1, :] * win[off:off + rblk, :]
            xbc_ref[0, r0:r0 + rblk, cols] = _silu(acc).astype(BF16)

    dt = jnp.dot(xm, wdt_ref[...], preferred_element_type=F32)
    for g in range(N_GROUPS):
        dt_ref[0, g] = dt[:, 8 * g:8 * g + 8]
    if with_zf:
        z_ref[0] = jnp.dot(xm, wz_ref[...], preferred_element_type=F32).astype(BF16)
        f_ref[0] = jnp.dot(xm, wf_ref[...], preferred_element_type=F32).astype(BF16)


def _in_projection(x, shift, scale, norm_w, wx, wdt, conv_w, conv_b, wz=None, wf=None):
    b, l, d = x.shape
    tm = min(512, l)
    with_zf = wz is not None
    hb = tm // CONV_HALO
    n_hb = l // CONV_HALO
    row = lambda bi, i: (bi, i, 0)
    vec = lambda bi, i: (bi, 0, 0)
    const = lambda bi, i: (0, 0)
    in_specs = [pl.BlockSpec((1, CONV_HALO, d), lambda bi, i: (bi, jnp.maximum(i * hb - 1, 0), 0)),
                pl.BlockSpec((1, tm, d), row),
                pl.BlockSpec((1, CONV_HALO, d), lambda bi, i: (bi, jnp.minimum((i + 1) * hb, n_hb - 1), 0)),
                pl.BlockSpec((1, 1, d), vec),
                pl.BlockSpec((1, 1, d), vec),
                pl.BlockSpec((1, d), const),
                pl.BlockSpec(wx.shape, const),
                pl.BlockSpec(wdt.shape, const),
                pl.BlockSpec(conv_w.shape, const),
                pl.BlockSpec(conv_b.shape, const)]
    args = [x, x, x, shift, scale, norm_w, wx, wdt, conv_w, conv_b]
    out_specs = [pl.BlockSpec((1, tm, wx.shape[1]), row),
                 pl.BlockSpec((1, N_GROUPS, tm, 8), lambda bi, i: (bi, 0, i, 0))]
    out_shape = [jax.ShapeDtypeStruct((b, l, wx.shape[1]), BF16),
                 jax.ShapeDtypeStruct((b, N_GROUPS, l, 8), F32)]
    if with_zf:
        in_specs += [pl.BlockSpec(wz.shape, const), pl.BlockSpec(wf.shape, const)]
        args += [wz, wf]
        out_specs += [pl.BlockSpec((1, tm, wz.shape[1]), row), pl.BlockSpec((1, tm, wf.shape[1]), row)]
        out_shape += [jax.ShapeDtypeStruct((b, l, wz.shape[1]), BF16),
                      jax.ShapeDtypeStruct((b, l, wf.shape[1]), BF16)]
    return pl.pallas_call(
        functools.partial(_inproj_kernel, with_zf=with_zf),
        grid=(b, l // tm),
        in_specs=in_specs,
        out_specs=out_specs,
        out_shape=out_shape,
        scratch_shapes=[pltpu.VMEM((tm + 2 * CONV_HALO, wx.shape[1]), F32)],
        compiler_params=_params("parallel", "arbitrary"),
        name="in_projection_zf" if with_zf else "in_projection",
    )(*args)


def _ssd_kernel(xs_ref, xb_ref, xc_ref, dt_ref, z_ref,
                dtb_ref, alog_ref, dsk_ref, h0f_ref, h0b_ref,
                y_ref, hf_ref, hb_ref,
                dt_s, cs_s, cst_s, bt_s, cb_s, yf_s, hst_s, *, seq, emit_y):
    nc = seq // CHUNK

    dtr = dt_ref[0, 0] + dtb_ref[0]
    dtv = jnp.maximum(dtr, 0.0) + jnp.log(1.0 + jnp.exp(-jnp.abs(dtr)))
    dt_s[...] = dtv
    a_log2 = -jnp.exp(alog_ref[0]) * LOG2E

    ri = lax.broadcasted_iota(jnp.int32, (CHUNK, CHUNK), 0)
    ci = lax.broadcasted_iota(jnp.int32, (CHUNK, CHUNK), 1)
    tri_fwd = ci <= ri
    tri_bwd = ci >= ri
    is_fwd_lane8 = lax.broadcasted_iota(jnp.int32, (CHUNK, 8), 1) < HEADS_PER_GROUP
    lane_pad = jnp.zeros((CHUNK, CHUNK - 8), F32)

    tri16 = tri_fwd.astype(BF16)

    def tables(c, carry):
        r0 = pl.multiple_of(c * CHUNK, CHUNK)
        rows = pl.ds(r0, CHUNK)
        da = dt_s[rows, :] * a_log2
        p0 = da.astype(BF16)
        r1 = da - p0.astype(F32)
        p1 = r1.astype(BF16)
        p2 = (r1 - p1.astype(F32)).astype(BF16)
        packed = jnp.concatenate([p0, p1, p2, jnp.zeros((CHUNK, CHUNK - 24), BF16)], axis=1)
        acc = jnp.dot(tri16, packed, preferred_element_type=F32)
        cs_f = acc[:, 0:8] + acc[:, 8:16] + acc[:, 16:24]
        cs_b = cs_f[CHUNK - 1:CHUNK, :] - cs_f + da
        cs = jnp.where(is_fwd_lane8, cs_f, cs_b)
        cs_s[rows, :] = cs
        cs_t = jnp.concatenate([cs, lane_pad], axis=1).T[:8, :]
        dt_t = jnp.concatenate([dt_s[rows, :], lane_pad], axis=1).T[:8, :]
        cst_s[c] = cs_t - jnp.log2(dt_t)
        bt = xb_ref[0, rows, :].astype(F32).T
        bt_s[c] = bt
        cb_s[c] = jnp.dot(xc_ref[0, rows, :], bt.astype(BF16), preferred_element_type=F32)
        return carry

    lax.fori_loop(0, nc, tables, 0, unroll=4)

    for j in range(HEADS_PER_GROUP):
        hst_s[0, j] = h0f_ref[0, 0, j]
        hst_s[1, j] = h0b_ref[0, 0, j]

    def chunk_dir(d, c, final):
        tri = tri_fwd if d == 0 else tri_bwd
        tot_row = CHUNK - 1 if d == 0 else 0
        r0 = pl.multiple_of(c * CHUNK, CHUNK)
        rows = pl.ds(r0, CHUNK)
        xs = xs_ref[0, rows, :]
        cc = xc_ref[0, rows, :].astype(F32)
        cs = cs_s[rows, :]
        cs_t = cst_s[c]
        bt = bt_s[c]
        cb = cb_s[c]
        ys = []
        for j in range(HEADS_PER_GROUP):
            lane = HEADS_PER_GROUP * d + j
            a_col1 = cs[:, lane:lane + 1]
            a_col = jnp.broadcast_to(a_col1, (CHUNK, CHUNK))
            a_tot = a_col1[tot_row:tot_row + 1, :]
            a_row = cs_t[lane:lane + 1, :]
            g = (cb * jnp.where(tri, jnp.exp2(a_col - a_row), 0.0)).astype(BF16)
            cea = (cc * jnp.exp2(a_col)).astype(BF16)
            xh = xs[:, HEADDIM * j:HEADDIM * (j + 1)]
            h = hst_s[d, j]
            y_h = (jnp.dot(g, xh, preferred_element_type=F32)
                   + jnp.dot(cea, h.astype(BF16), preferred_element_type=F32))
            w_row = jnp.exp2(a_tot - a_row)
            s_new = jnp.dot((bt * w_row).astype(BF16), xh, preferred_element_type=F32)
            hst_s[d, j] = h * jnp.exp2(a_tot) + s_new
            ys.append(y_h)
        if not emit_y:
            return
        y_c = jnp.concatenate(ys, axis=1)
        if d == 0:
            y_c = y_c + dsk_ref[...] * xs.astype(F32)
        if final:
            zc = z_ref[0, rows, :].astype(F32)
            y_ref[0, rows, :] = ((yf_s[rows, :] + y_c) * _silu(zc)).astype(BF16)
        else:
            yf_s[rows, :] = y_c

    def first_half(t, carry):
        chunk_dir(0, t, False)
        chunk_dir(1, nc - 1 - t, False)
        return carry

    def second_half(t, carry):
        chunk_dir(0, t, True)
        chunk_dir(1, nc - 1 - t, True)
        return carry

    lax.fori_loop(0, nc // 2, first_half, 0, unroll=2)
    lax.fori_loop(nc // 2, nc, second_half, 0, unroll=2)
    for j in range(HEADS_PER_GROUP):
        hf_ref[0, 0, j] = hst_s[0, j]
        hb_ref[0, 0, j] = hst_s[1, j]
    if not emit_y:
        y_ref[...] = jnp.zeros(y_ref.shape, y_ref.dtype)


def _ssd_mixer(xbc, dt, z, dtb, alog, dskip, h0f, h0b, emit_y):
    b, l, _ = xbc.shape
    gw = HEADS_PER_GROUP * HEADDIM
    nxb = (N_GROUPS * gw) // D_STATE
    y_rows = l if emit_y else 8
    nc = l // CHUNK
    st_shape = (b, N_GROUPS, HEADS_PER_GROUP, D_STATE, HEADDIM)
    st_spec = pl.BlockSpec((1, 1, HEADS_PER_GROUP, D_STATE, HEADDIM), lambda bi, g: (bi, g, 0, 0, 0))
    in_specs = [
        pl.BlockSpec((1, l, gw), lambda bi, g: (bi, 0, g)),
        pl.BlockSpec((1, l, D_STATE), lambda bi, g: (bi, 0, nxb + g)),
        pl.BlockSpec((1, l, D_STATE), lambda bi, g: (bi, 0, nxb + N_GROUPS + g)),
        pl.BlockSpec((1, 1, l, 8), lambda bi, g: (bi, g, 0, 0)),
        pl.BlockSpec((1, y_rows, gw), lambda bi, g: (bi, 0, g)),
        pl.BlockSpec((1, 1, 8), lambda bi, g: (g, 0, 0)),
        pl.BlockSpec((1, 1, 8), lambda bi, g: (g, 0, 0)),
        pl.BlockSpec((1, gw), lambda bi, g: (0, g)),
        st_spec, st_spec,
    ]
    out_specs = [pl.BlockSpec((1, y_rows, gw), lambda bi, g: (bi, 0, g)), st_spec, st_spec]
    out_shape = [jax.ShapeDtypeStruct((b, y_rows, N_GROUPS * gw), BF16),
                 jax.ShapeDtypeStruct(st_shape, F32), jax.ShapeDtypeStruct(st_shape, F32)]
    scratch = [pltpu.VMEM((l, 8), F32), pltpu.VMEM((l, 8), F32),
               pltpu.VMEM((nc, 8, CHUNK), F32),
               pltpu.VMEM((nc, D_STATE, CHUNK), F32), pltpu.VMEM((nc, CHUNK, CHUNK), F32),
               pltpu.VMEM((y_rows, gw), F32),
               pltpu.VMEM((2, HEADS_PER_GROUP, D_STATE, HEADDIM), F32)]
    return pl.pallas_call(
        functools.partial(_ssd_kernel, seq=l, emit_y=emit_y),
        grid=(b, N_GROUPS),
        in_specs=in_specs, out_specs=out_specs, out_shape=out_shape,
        scratch_shapes=scratch,
        compiler_params=_params("parallel", "arbitrary"),
        name="ssd_mixer" if emit_y else "ssd_mixer_ctx",
    )(xbc, xbc, xbc, dt, z, dtb, alog, dskip, h0f, h0b)


def _fourier_kernel(f_ref, cc_ref, sc_ref, w_ref, ca_ref, sa_ref, cb_ref, sb_ref, rev_ref, o_ref,
                    u_s, v_s, cl_s, sl_s, *, seq, rows):
    i = pl.program_id(1)
    n_blk = o_ref.shape[1]

    @pl.when(i == 0)
    def _():
        scale = 1.0 / math.sqrt(seq * FOURIER_CH)
        for g in range(N_FOURIER_GROUPS):
            w = w_ref[g]
            a = jnp.dot(cc_ref[...], w, preferred_element_type=F32, precision=HIGHEST) * scale
            bm = jnp.dot(sc_ref[...], w, preferred_element_type=F32, precision=HIGHEST) * scale
            cols = slice(FOURIER_CH * g, FOURIER_CH * (g + 1))
            fg = f_ref[0, :, cols]
            u_s[:, cols] = jnp.dot(fg, a.astype(BF16), preferred_element_type=F32).astype(BF16)
            v_s[:, cols] = jnp.dot(fg, bm.astype(BF16), preferred_element_type=F32).astype(BF16)

    k2_0 = i * (rows // DFT_SPLIT)
    cbeta = cb_ref[...]
    sbeta = sb_ref[...]
    for j in range(rows // DFT_SPLIT + 1):
        n_sub = DFT_SPLIT if j < rows // DFT_SPLIT else DFT_EXTRA
        calpha = ca_ref[pl.ds(k2_0 + j, 1), :]
        salpha = sa_ref[pl.ds(k2_0 + j, 1), :]
        sub = slice(DFT_SPLIT * j, DFT_SPLIT * j + n_sub)
        cl_s[sub, :] = (calpha * cbeta[:n_sub] - salpha * sbeta[:n_sub]).astype(BF16)
        sl_s[sub, :] = (salpha * cbeta[:n_sub] + calpha * sbeta[:n_sub]).astype(BF16)
    p = jnp.dot(cl_s[...], u_s[...], preferred_element_type=F32)
    q = jnp.dot(sl_s[...], v_s[...], preferred_element_type=F32)
    o_ref[0, i] = (p - q)[:rows].astype(BF16)
    o_ref[0, n_blk - 1 - i] = jnp.dot(rev_ref[...], (p + q).astype(BF16),
                                      preferred_element_type=F32).astype(BF16)


def _dft_tables(n, dtype):
    k = lax.broadcasted_iota(jnp.int32, (n, n), 0)
    l = lax.broadcasted_iota(jnp.int32, (n, n), 1)
    ang = ((k * l) % n).astype(F32) * (2.0 * math.pi / n)
    return jnp.cos(ang).astype(dtype), jnp.sin(ang).astype(dtype)


def _dft_factor_tables(n):
    n2 = n // DFT_SPLIT
    k2 = lax.broadcasted_iota(jnp.int32, (n2, n), 0)
    k1 = lax.broadcasted_iota(jnp.int32, (DFT_SPLIT, n), 0)
    alpha = ((k2 * lax.broadcasted_iota(jnp.int32, (n2, n), 1)) % n2).astype(F32) * (2.0 * math.pi / n2)
    beta = ((k1 * lax.broadcasted_iota(jnp.int32, (DFT_SPLIT, n), 1)) % n).astype(F32) * (2.0 * math.pi / n)
    return jnp.cos(alpha), jnp.sin(alpha), jnp.cos(beta), jnp.sin(beta)


def _fourier_mixer(f, w_four):
    b, l, df = f.shape
    tr = 512
    n_blk = l // tr
    assert l % (2 * tr) == 0
    cc, sc = _dft_tables(FOURIER_CH, F32)
    ca, sa, cb, sb = _dft_factor_tables(l)
    rev = (lax.broadcasted_iota(jnp.int32, (tr, tr + DFT_EXTRA), 0)
           + lax.broadcasted_iota(jnp.int32, (tr, tr + DFT_EXTRA), 1) == tr).astype(BF16)
    const = lambda bi, i: (0, 0)
    out = pl.pallas_call(
        functools.partial(_fourier_kernel, seq=l, rows=tr),
        grid=(b, n_blk // 2),
        in_specs=[pl.BlockSpec((1, l, df), lambda bi, i: (bi, 0, 0)),
                  pl.BlockSpec((FOURIER_CH, FOURIER_CH), const),
                  pl.BlockSpec((FOURIER_CH, FOURIER_CH), const),
                  pl.BlockSpec(w_four.shape, lambda bi, i: (0, 0, 0)),
                  pl.BlockSpec(ca.shape, const), pl.BlockSpec(sa.shape, const),
                  pl.BlockSpec(cb.shape, const), pl.BlockSpec(sb.shape, const),
                  pl.BlockSpec(rev.shape, const)],
        out_specs=pl.BlockSpec((1, n_blk, tr, df), lambda bi, i: (bi, 0, 0, 0)),
        out_shape=jax.ShapeDtypeStruct((b, n_blk, tr, df), BF16),
        scratch_shapes=[pltpu.VMEM((l, df), BF16), pltpu.VMEM((l, df), BF16),
                        pltpu.VMEM((tr + DFT_EXTRA, l), BF16), pltpu.VMEM((tr + DFT_EXTRA, l), BF16)],
        compiler_params=_params("parallel", "arbitrary"),
        name="fourier_mixer",
    )(f, cc, sc, w_four, ca, sa, cb, sb, rev)
    return out.reshape(b, l, df)


def _outproj_router_kernel(y_ref, four_ref, x_ref, g1_ref, sh_ref, sc_ref, nssd_ref, n2_ref,
                           wos_ref, wof_ref, wr_ref, br_ref,
                           x1_ref, h_ref, eid_ref, gate_ref):
    y = y_ref[0].astype(F32)
    ms = jnp.mean(y * y, axis=-1, keepdims=True)
    yn = (y * lax.rsqrt(ms + EPS) * nssd_ref[...]).astype(BF16)
    mix = (jnp.dot(yn, wos_ref[...], preferred_element_type=F32)
           + jnp.dot(four_ref[0], wof_ref[...], preferred_element_type=F32))
    x1 = x_ref[0] + g1_ref[0] * mix
    x1_ref[0] = x1
    ms2 = jnp.mean(x1 * x1, axis=-1, keepdims=True)
    h = (x1 * lax.rsqrt(ms2 + EPS) * n2_ref[...]) * (1.0 + sc_ref[0]) + sh_ref[0]
    _store_token_tiles(h_ref, (0,), h)

    lg = _dot_split3(h, wr_ref[...]) + br_ref[...]
    tm = lg.shape[0]
    lane = lax.broadcasted_iota(jnp.int32, (tm, ROUTE_LANES), 1)
    lane_f = lane.astype(F32)
    neg = jnp.float32(-1e30)
    big = jnp.float32(1e9)
    is_grp = lane < N_EXPERT_GROUPS
    gl = jnp.where(is_grp, lg, neg)
    gmax = jnp.max(gl, axis=-1, keepdims=True)
    gsum = jnp.sum(jnp.where(is_grp, jnp.exp(gl - gmax), 0.0), axis=-1, keepdims=True)
    grp = jnp.min(jnp.where(gl == gmax, lane_f, big), axis=-1, keepdims=True)
    p_grp = 1.0 / gsum
    lo = N_EXPERT_GROUPS + EXPERTS_PER_GROUP * grp
    in_grp = jnp.logical_and(lane_f >= lo, lane_f < lo + EXPERTS_PER_GROUP)
    el = jnp.where(in_grp, lg, neg)
    m1 = jnp.max(el, axis=-1, keepdims=True)
    i1 = jnp.min(jnp.where(el == m1, lane_f, big), axis=-1, keepdims=True)
    el2 = jnp.where(lane_f == i1, neg, el)
    m2 = jnp.max(el2, axis=-1, keepdims=True)
    i2 = jnp.min(jnp.where(el2 == m2, lane_f, big), axis=-1, keepdims=True)
    e2 = jnp.exp(m2 - m1)
    den = 1.0 + e2
    gate1 = p_grp / den
    gate2 = p_grp * e2 / den
    lane8 = lax.broadcasted_iota(jnp.int32, (tm, 8), 1)
    eid = jnp.where(lane8 == 0, i1 - N_EXPERT_GROUPS, jnp.where(lane8 == 1, i2 - N_EXPERT_GROUPS, 0.0))
    eid_ref[0] = eid.astype(jnp.int32)
    gate_ref[0] = jnp.where(lane8 == 0, gate1, jnp.where(lane8 == 1, gate2, 0.0))


def _outproj_router(y, four, x, g1, sh2, sc2, nssd, n2, wos, wof, wr, br):
    b, l, d = x.shape
    tm = min(512, l)
    row = lambda bi, i: (bi, i, 0)
    vec = lambda bi, i: (bi, 0, 0)
    const = lambda bi, i: (0, 0)
    return pl.pallas_call(
        _outproj_router_kernel,
        grid=(b, l // tm),
        in_specs=[pl.BlockSpec((1, tm, y.shape[2]), row),
                  pl.BlockSpec((1, tm, four.shape[2]), row),
                  pl.BlockSpec((1, tm, d), row),
                  pl.BlockSpec((1, 1, d), vec), pl.BlockSpec((1, 1, d), vec), pl.BlockSpec((1, 1, d), vec),
                  pl.BlockSpec((1, y.shape[2]), const), pl.BlockSpec((1, d), const),
                  pl.BlockSpec(wos.shape, const), pl.BlockSpec(wof.shape, const),
                  pl.BlockSpec(wr.shape, const), pl.BlockSpec(br.shape, const)],
        out_specs=[pl.BlockSpec((1, tm, d), row), pl.BlockSpec((1, tm * TILE_ROWS, LANES), row),
                   pl.BlockSpec((1, tm, 8), row), pl.BlockSpec((1, tm, 8), row)],
        out_shape=[jax.ShapeDtypeStruct((b, l, d), F32), jax.ShapeDtypeStruct((b, l * TILE_ROWS, LANES), F32),
                   jax.ShapeDtypeStruct((b, l, 8), jnp.int32), jax.ShapeDtypeStruct((b, l, 8), F32)],
        compiler_params=_params("parallel", "arbitrary"),
        name="outproj_router",
    )(y, four, x, g1, sh2, sc2, nssd, n2, wos, wof, wr, br)


def _expert_kernel(bexp_ref, nused_ref, tokc_ref, tokn_ref, dstp_ref, dstc_ref, h_hbm, wg_ref, wu_ref, wd_ref,
                   y_hbm, hbuf, ybuf, act_s, wg_s, wu_s, wd_s, sem_in, sem_out):
    i = pl.program_id(0)
    n_used = nused_ref[0]
    active = i < n_used
    is_last = i == n_used - 1
    slot = i % 2
    other = 1 - slot
    de = wg_s.shape[1]
    d = wd_s.shape[1]
    n_phase = 8
    rows_per_phase = MOE_ROWS // n_phase
    blk_tile_rows = MOE_ROWS * TILE_ROWS

    def tile(ref, lead, row0):
        return ref.at[lead + (pl.ds(row0, TILE_ROWS), slice(None))]

    def start_gather(tok_ref, s, r):
        src = tile(h_hbm, (), pl.multiple_of(tok_ref[0, 0, r], TILE_ROWS))
        pltpu.make_async_copy(src, tile(hbuf, (s,), r * TILE_ROWS), sem_in.at[s]).start(priority=r % 2)

    def start_scatter(dst_ref, s, r):
        dst = tile(y_hbm, (), pl.multiple_of(dst_ref[0, 0, r], TILE_ROWS))
        pltpu.make_async_copy(tile(ybuf, (s,), r * TILE_ROWS), dst, sem_out.at[s]).start(priority=r % 2)

    def wait_gather(s):
        pltpu.make_async_copy(h_hbm.at[pl.ds(0, blk_tile_rows), :], hbuf.at[s], sem_in.at[s]).wait()

    def wait_scatter(s):
        pltpu.make_async_copy(ybuf.at[s], y_hbm.at[pl.ds(0, blk_tile_rows), :], sem_out.at[s]).wait()

    @pl.when(i == 0)
    def _():
        for r in range(MOE_ROWS):
            start_gather(tokc_ref, 0, r)

    prev = bexp_ref[jnp.maximum(i - 1, 0)]

    @pl.when(jnp.logical_and(active, jnp.logical_or(i == 0, bexp_ref[i] != prev)))
    def _():
        wg_s[...] = wg_ref[0].astype(BF16)
        wu_s[...] = wu_ref[0].astype(BF16)
        wd_s[...] = wd_ref[0].astype(BF16)

    @pl.when(active)
    def _():
        wait_gather(slot)

    @pl.when(jnp.logical_and(active, i >= 2))
    def _():
        wait_scatter(slot)

    def step(with_scatter, cur):
        nxt = 1 - cur
        xb = _load_token_tiles(hbuf, (cur,), MOE_ROWS).astype(BF16)
        ys = []
        for p in range(n_phase):
            for r in range(p * rows_per_phase, (p + 1) * rows_per_phase):
                start_gather(tokn_ref, nxt, r)
                if with_scatter:
                    start_scatter(dstp_ref, nxt, r)
            if p < n_phase // 2:
                w = de // (n_phase // 2)
                cols = slice(p * w, (p + 1) * w)
                gact = jnp.dot(xb, wg_s[:, cols], preferred_element_type=F32)
                up = jnp.dot(xb, wu_s[:, cols], preferred_element_type=F32)
                act_s[:, cols] = (_silu(gact) * up).astype(BF16)
            else:
                w = d // (n_phase // 2)
                cols = slice((p - n_phase // 2) * w, (p - n_phase // 2 + 1) * w)
                ys.append(jnp.dot(act_s[...], wd_s[:, cols], preferred_element_type=F32))
        _store_token_tiles(ybuf, (cur,), jnp.concatenate(ys, axis=1))

    @pl.when(i == 0)
    def _():
        step(False, 0)

    @pl.when(jnp.logical_and(active, jnp.logical_and(i > 0, slot == 0)))
    def _():
        step(True, 0)

    @pl.when(jnp.logical_and(active, slot == 1))
    def _():
        step(True, 1)

    @pl.when(is_last)
    def _():
        for r in range(MOE_ROWS):
            start_scatter(dstc_ref, slot, r)
        wait_gather(other)
        wait_scatter(other)
        wait_scatter(slot)

    @pl.when(jnp.logical_not(active))
    def _():
        ybuf[0] = jnp.zeros((blk_tile_rows, LANES), F32)
        dst0 = pl.multiple_of(dstc_ref[0, 0, 0], TILE_ROWS)
        fill = pltpu.make_async_copy(ybuf.at[0], y_hbm.at[pl.ds(dst0, blk_tile_rows), :], sem_out.at[0])
        fill.start()
        fill.wait()


def _experts(h_tiles, row_tok, row_dst, blk_exp, n_used, w_eg, w_eu, w_ed):
    n_blocks = blk_exp.shape[0]
    d, de = w_eg.shape[1], w_eg.shape[2]
    assert d == TILE_ROWS * LANES
    idx_shape = (n_blocks, 1, MOE_ROWS)
    idx_block = (1, 1, MOE_ROWS)
    smem = pltpu.SMEM
    blk_tile_rows = MOE_ROWS * TILE_ROWS
    grid_spec = pltpu.PrefetchScalarGridSpec(
        num_scalar_prefetch=2,
        grid=(n_blocks,),
        in_specs=[pl.BlockSpec(idx_block, lambda i, be, nu: (i, 0, 0), memory_space=smem),
                  pl.BlockSpec(idx_block, lambda i, be, nu: (jnp.minimum(i + 1, n_blocks - 1), 0, 0),
                               memory_space=smem),
                  pl.BlockSpec(idx_block, lambda i, be, nu: (jnp.maximum(i - 1, 0), 0, 0), memory_space=smem),
                  pl.BlockSpec(idx_block, lambda i, be, nu: (i, 0, 0), memory_space=smem),
                  pl.BlockSpec(memory_space=pl.ANY),
                  pl.BlockSpec((1, d, de), lambda i, be, nu: (be[i], 0, 0)),
                  pl.BlockSpec((1, d, de), lambda i, be, nu: (be[i], 0, 0)),
                  pl.BlockSpec((1, de, d), lambda i, be, nu: (be[i], 0, 0))],
        out_specs=pl.BlockSpec(memory_space=pl.ANY),
        scratch_shapes=[pltpu.VMEM((2, blk_tile_rows, LANES), F32), pltpu.VMEM((2, blk_tile_rows, LANES), F32),
                        pltpu.VMEM((MOE_ROWS, de), BF16),
                        pltpu.VMEM((d, de), BF16), pltpu.VMEM((d, de), BF16), pltpu.VMEM((de, d), BF16),
                        pltpu.SemaphoreType.DMA((2,)), pltpu.SemaphoreType.DMA((2,))],
    )
    row_tok = (row_tok * TILE_ROWS).reshape(idx_shape)
    row_dst = (row_dst * TILE_ROWS).reshape(idx_shape)
    return pl.pallas_call(
        _expert_kernel,
        grid_spec=grid_spec,
        out_shape=jax.ShapeDtypeStruct((n_blocks * blk_tile_rows, LANES), F32),
        compiler_params=_params("arbitrary"),
        name="moe_experts",
    )(blk_exp, n_used, row_tok, row_tok, row_dst, row_dst, h_tiles, w_eg, w_eu, w_ed)


def _route_tables(eid, n_tok):
    n_assign = 2 * n_tok
    e_flat = jnp.concatenate([eid[:, 0], eid[:, 1]])
    order = jnp.argsort(e_flat).astype(jnp.int32)
    experts = jnp.arange(N_EXPERTS, dtype=jnp.int32)
    counts = jnp.sum((e_flat[:, None] == experts[None, :]).astype(jnp.int32), axis=0)
    start = jnp.cumsum(counts) - counts
    padded = (counts + MOE_ROWS - 1) // MOE_ROWS * MOE_ROWS
    end_pad = jnp.cumsum(padded)
    start_pad = end_pad - padded
    n_blocks = -(-(n_assign + N_EXPERTS * (MOE_ROWS - 1)) // MOE_ROWS)
    blk_row0 = jnp.arange(n_blocks, dtype=jnp.int32) * MOE_ROWS
    blk_exp = jnp.minimum(jnp.sum((end_pad[None, :] <= blk_row0[:, None]).astype(jnp.int32), axis=1),
                          N_EXPERTS - 1).astype(jnp.int32)
    j = (blk_row0 - start_pad[blk_exp])[:, None] + jnp.arange(MOE_ROWS, dtype=jnp.int32)[None, :]
    valid = j < counts[blk_exp][:, None]
    src = jnp.clip(start[blk_exp][:, None] + j, 0, n_assign - 1)
    assign = order[src.reshape(-1)].reshape(src.shape)
    row_tok = jnp.where(valid, jnp.where(assign >= n_tok, assign - n_tok, assign), 0)
    spare = n_assign + jnp.cumsum((~valid).reshape(-1).astype(jnp.int32)).reshape(valid.shape) - 1
    row_dst = jnp.where(valid, assign, spare)
    n_used = (end_pad[-1:] // MOE_ROWS).astype(jnp.int32)
    return row_tok.astype(jnp.int32), row_dst.astype(jnp.int32), blk_exp, n_used


def _combine_kernel(x1_ref, y0_ref, y1_ref, gate_ref, g2_ref, nw_ref, o_ref):
    gate = gate_ref[0]
    tm = gate.shape[0]
    moe = (gate[:, 0:1] * _load_token_tiles(y0_ref, (), tm) + gate[:, 1:2] * _load_token_tiles(y1_ref, (), tm))
    x2 = x1_ref[0] + g2_ref[0] * moe
    ms = jnp.mean(x2 * x2, axis=-1, keepdims=True)
    o_ref[0] = x2 * lax.rsqrt(ms + EPS) * nw_ref[...]


def _combine(x1, y_assign, gate, g2, final_norm):
    b, l, d = x1.shape
    tm = min(512, l)
    nt = l // tm
    row = lambda bi, i: (bi, i, 0)
    return pl.pallas_call(
        _combine_kernel,
        grid=(b, nt),
        in_specs=[pl.BlockSpec((1, tm, d), row),
                  pl.BlockSpec((tm * TILE_ROWS, LANES), lambda bi, i: (bi * nt + i, 0)),
                  pl.BlockSpec((tm * TILE_ROWS, LANES), lambda bi, i: (b * nt + bi * nt + i, 0)),
                  pl.BlockSpec((1, tm, 8), row),
                  pl.BlockSpec((1, 1, d), lambda bi, i: (bi, 0, 0)),
                  pl.BlockSpec((1, d), lambda bi, i: (0, 0))],
        out_specs=pl.BlockSpec((1, tm, d), row),
        out_shape=jax.ShapeDtypeStruct((b, l, d), F32),
        compiler_params=_params("parallel", "arbitrary"),
        name="moe_combine",
    )(x1, y_assign, y_assign, gate, g2, final_norm.reshape(1, d))


def _group_major(v):
    return v.reshape(2, N_GROUPS, HEADS_PER_GROUP).transpose(1, 0, 2).reshape(N_GROUPS, 1, 2 * HEADS_PER_GROUP)


def kernel(x, c, ctx, c_ctx, w_mod, b_mod, norm1, w_in, conv_w, conv_b, dt_bias, a_log, d_skip, ssd_norm,
           w_four, w_out, norm2, w_rg, b_rg, w_re, b_re, w_eg, w_eu, w_ed, final_norm):
    bsz, seq, d = x.shape
    n_tok = bsz * seq
    d_ssd = N_GROUPS * HEADS_PER_GROUP * HEADDIM
    conv_dim = d_ssd + 2 * N_GROUPS * D_STATE
    n_heads = N_GROUPS * HEADS_PER_GROUP
    layer = 0

    c_rows = jnp.zeros((16, d), F32).at[:bsz].set(c).at[bsz].set(c_ctx)
    mod = _modulation(c_rows, w_mod[layer], b_mod[layer])
    sh1, sc1, g1, sh2, sc2, g2 = [m[:bsz, None, :] for m in jnp.split(mod, 6, axis=-1)]
    sh1c, sc1c = [jnp.broadcast_to(m[bsz][None, None, :], (bsz, 1, d)) for m in jnp.split(mod, 6, axis=-1)[:2]]

    w = w_in[layer]
    wz = w[:, :d_ssd].astype(BF16)
    wx = w[:, d_ssd:d_ssd + conv_dim].astype(BF16)
    wdt = w[:, d_ssd + conv_dim:d_ssd + conv_dim + 2 * n_heads]
    wdt = wdt.reshape(d, 2, N_GROUPS, HEADS_PER_GROUP).transpose(0, 2, 1, 3).reshape(d, 2 * n_heads)
    wdt = jnp.pad(wdt, ((0, 0), (0, 128 - 2 * n_heads))).astype(BF16)
    wf = w[:, d_ssd + conv_dim + 2 * n_heads:].astype(BF16)
    n1 = norm1[layer].reshape(1, d)

    dtb = _group_major(dt_bias[layer])
    alog = _group_major(a_log[layer])
    dsk = jnp.repeat(d_skip[layer], HEADDIM).reshape(1, d_ssd)
    cw = conv_w[layer]
    cb = conv_b[layer].reshape(1, conv_dim)

    xbc_c, dt_c = _in_projection(ctx, sh1c, sc1c, n1, wx, wdt, cw, cb)
    h_zero = jnp.zeros((bsz, N_GROUPS, HEADS_PER_GROUP, D_STATE, HEADDIM), F32)
    z_dummy = jnp.zeros((bsz, 8, d_ssd), BF16)
    _, hf_c, hb_c = _ssd_mixer(xbc_c, dt_c, z_dummy, dtb, alog, dsk, h_zero, h_zero, emit_y=False)

    xbc_l, dt_l, z_l, f_l = _in_projection(x, sh1, sc1, n1, wx, wdt, cw, cb, wz, wf)
    y_l, _, _ = _ssd_mixer(xbc_l, dt_l, z_l, dtb, alog, dsk, hf_c, hb_c, emit_y=True)
    four = _fourier_mixer(f_l, w_four[layer])

    wo = w_out[layer]
    wr = jnp.concatenate([w_rg[layer], w_re[layer].transpose(1, 0, 2).reshape(d, N_EXPERTS)], axis=1)
    wr = jnp.pad(wr, ((0, 0), (0, ROUTE_LANES - wr.shape[1])))
    br = jnp.pad(jnp.concatenate([b_rg[layer], b_re[layer].reshape(-1)]),
                 (0, ROUTE_LANES - N_EXPERT_GROUPS - N_EXPERTS)).reshape(1, ROUTE_LANES)
    x1, h, eid, gate = _outproj_router(
        y_l, four, x, g1, sh2, sc2, ssd_norm[layer].reshape(1, d_ssd), norm2[layer].reshape(1, d),
        wo[:d_ssd].astype(BF16), wo[d_ssd:].astype(BF16), wr, br)

    row_tok, row_dst, blk_exp, n_used = _route_tables(eid.reshape(n_tok, 8), n_tok)
    y_assign = _experts(h.reshape(n_tok * TILE_ROWS, LANES), row_tok, row_dst, blk_exp, n_used,
                        w_eg[layer], w_eu[layer], w_ed[layer])
    return _combine(x1, y_assign, gate, g2, final_norm)
```

```python
import functools
import math

import jax
import jax.numpy as jnp
from jax import lax
from jax.experimental import pallas as pl
from jax.experimental.pallas import tpu as pltpu

F32 = jnp.float32
BF16 = jnp.bfloat16

EPS = 1e-6
LOG2E = 1.4426950408889634
CHUNK = 128
N_GROUPS = 4
HEADS_PER_GROUP = 4
HEADDIM = 64
D_STATE = 128
CONV_W = 5
CONV_HALO = 16
N_FOURIER_GROUPS = 4
DFT_SPLIT = 64
DFT_EXTRA = 16
FOURIER_CH = 128
N_EXPERT_GROUPS = 4
EXPERTS_PER_GROUP = 8
N_EXPERTS = N_EXPERT_GROUPS * EXPERTS_PER_GROUP
ROUTE_LANES = 128
MOE_ROWS = 256
VMEM_LIMIT_BYTES = 56 * 1024 * 1024


def _params(*sem):
    return pltpu.CompilerParams(dimension_semantics=sem, vmem_limit_bytes=VMEM_LIMIT_BYTES)


def _silu(v):
    h = 0.5 * v
    return h + h * jnp.tanh(h)


LANES = 128
TILE_ROWS = 8


def _store_token_tiles(ref, lead, val):
    rows = val.shape[0]
    for j in range(TILE_ROWS):
        ref[lead + (pl.ds(j, rows, stride=TILE_ROWS), slice(None))] = val[:, LANES * j:LANES * (j + 1)]


def _load_token_tiles(ref, lead, rows):
    return jnp.concatenate(
        [ref[lead + (pl.ds(j, rows, stride=TILE_ROWS), slice(None))] for j in range(TILE_ROWS)], axis=1)


def _dot_split3(a, b):
    a_hi = a.astype(BF16)
    a_lo = (a - a_hi.astype(F32)).astype(BF16)
    b_hi = b.astype(BF16)
    b_lo = (b - b_hi.astype(F32)).astype(BF16)
    dot = functools.partial(jnp.dot, preferred_element_type=F32)
    return dot(a_hi, b_hi) + (dot(a_lo, b_hi) + dot(a_hi, b_lo))


def _mod_kernel(c_ref, w_ref, b_ref, o_ref):
    s = _silu(c_ref[...])
    o_ref[...] = _dot_split3(s, w_ref[...]) + b_ref[...]


def _modulation(c_rows, w_mod, b_mod):
    rows, d = c_rows.shape
    n = w_mod.shape[1]
    tn = 512
    return pl.pallas_call(
        _mod_kernel,
        grid=(n // tn,),
        in_specs=[pl.BlockSpec((rows, d), lambda j: (0, 0)),
                  pl.BlockSpec((d, tn), lambda j: (0, j)),
                  pl.BlockSpec((1, tn), lambda j: (0, j))],
        out_specs=pl.BlockSpec((rows, tn), lambda j: (0, j)),
        out_shape=jax.ShapeDtypeStruct((rows, n), F32),
        compiler_params=_params("arbitrary"),
        name="modulation",
    )(c_rows, w_mod, b_mod.reshape(1, n))


def _inproj_kernel(xp_ref, x_ref, xn_ref, sh_ref, sc_ref, nw_ref, wx_ref, wdt_ref, cw_ref, cb_ref, *rest,
                   with_zf):
    if with_zf:
        wz_ref, wf_ref, xbc_ref, dt_ref, z_ref, f_ref, pre_s = rest
    else:
        xbc_ref, dt_ref, pre_s = rest
    tm = x_ref.shape[1]
    i = pl.program_id(1)
    xa = jnp.concatenate([xp_ref[0], x_ref[0], xn_ref[0]], axis=0)
    ms = jnp.mean(xa * xa, axis=-1, keepdims=True)
    xnorm = xa * lax.rsqrt(ms + EPS) * nw_ref[...]
    xm_all = (xnorm * (1.0 + sc_ref[0]) + sh_ref[0]).astype(BF16)
    xm = xm_all[CONV_HALO:CONV_HALO + tm]

    pre_s[...] = jnp.dot(xm_all, wx_ref[...], preferred_element_type=F32)
    top, bot = slice(0, CONV_HALO), slice(CONV_HALO + tm, 2 * CONV_HALO + tm)
    pre_s[top, :] = pre_s[top, :] * (i > 0).astype(F32)
    pre_s[bot, :] = pre_s[bot, :] * (i < pl.num_programs(1) - 1).astype(F32)

    rblk, cblk = 64, 256
    for c0 in range(0, wx_ref.shape[1], cblk):
        cols = slice(c0, c0 + cblk)
        w = cw_ref[:, cols]
        bias = jnp.broadcast_to(cb_ref[:, cols], (rblk, cblk))
        for r0 in range(0, tm, rblk):
            lo = r0 + CONV_HALO - 8
            win = pre_s[lo:lo + rblk + 16, cols]
            acc = bias
            for k in range(CONV_W):
                off = 8 - CONV_W // 2 + k
                acc = acc + w[k:k + 1, :] * win[off:off + rblk, :]
            xbc_ref[0, r0:r0 + rblk, cols] = _silu(acc).astype(BF16)

    dt = jnp.dot(xm, wdt_ref[...], preferred_element_type=F32)
    for g in range(N_GROUPS):
        dt_ref[0, g] = dt[:, 8 * g:8 * g + 8]
    if with_zf:
        z_ref[0] = jnp.dot(xm, wz_ref[...], preferred_element_type=F32).astype(BF16)
        f_ref[0] = jnp.dot(xm, wf_ref[...], preferred_element_type=F32).astype(BF16)


def _in_projection(x, shift, scale, norm_w, wx, wdt, conv_w, conv_b, wz=None, wf=None):
    b, l, d = x.shape
    tm = min(512, l)
    with_zf = wz is not None
    hb = tm // CONV_HALO
    n_hb = l // CONV_HALO
    row = lambda bi, i: (bi, i, 0)
    vec = lambda bi, i: (bi, 0, 0)
    const = lambda bi, i: (0, 0)
    in_specs = [pl.BlockSpec((1, CONV_HALO, d), lambda bi, i: (bi, jnp.maximum(i * hb - 1, 0), 0)),
                pl.BlockSpec((1, tm, d), row),
                pl.BlockSpec((1, CONV_HALO, d), lambda bi, i: (bi, jnp.minimum((i + 1) * hb, n_hb - 1), 0)),
                pl.BlockSpec((1, 1, d), vec),
                pl.BlockSpec((1, 1, d), vec),
                pl.BlockSpec((1, d), const),
                pl.BlockSpec(wx.shape, const),
                pl.BlockSpec(wdt.shape, const),
                pl.BlockSpec(conv_w.shape, const),
                pl.BlockSpec(conv_b.shape, const)]
    args = [x, x, x, shift, scale, norm_w, wx, wdt, conv_w, conv_b]
    out_specs = [pl.BlockSpec((1, tm, wx.shape[1]), row),
                 pl.BlockSpec((1, N_GROUPS, tm, 8), lambda bi, i: (bi, 0, i, 0))]
    out_shape = [jax.ShapeDtypeStruct((b, l, wx.shape[1]), BF16),
                 jax.ShapeDtypeStruct((b, N_GROUPS, l, 8), F32)]
    if with_zf:
        in_specs += [pl.BlockSpec(wz.shape, const), pl.BlockSpec(wf.shape, const)]
        args += [wz, wf]
        out_specs += [pl.BlockSpec((1, tm, wz.shape[1]), row), pl.BlockSpec((1, tm, wf.shape[1]), row)]
        out_shape += [jax.ShapeDtypeStruct((b, l, wz.shape[1]), BF16),
                      jax.ShapeDtypeStruct((b, l, wf.shape[1]), BF16)]
    return pl.pallas_call(
        functools.partial(_inproj_kernel, with_zf=with_zf),
        grid=(b, l // tm),
        in_specs=in_specs,
        out_specs=out_specs,
        out_shape=out_shape,
        scratch_shapes=[pltpu.VMEM((tm + 2 * CONV_HALO, wx.shape[1]), F32)],
        compiler_params=_params("parallel", "arbitrary"),
        name="in_projection_zf" if with_zf else "in_projection",
    )(*args)


def _ssd_kernel(xs_ref, xb_ref, xc_ref, dt_ref, z_ref,
                dtb_ref, alog_ref, dsk_ref, h0f_ref, h0b_ref,
                y_ref, hf_ref, hb_ref,
                dt_s, cs_s, cst_s, bt_s, cb_s, yf_s, hst_s, *, seq, emit_y):
    nc = seq // CHUNK

    dtr = dt_ref[0, 0] + dtb_ref[0]
    dtv = jnp.maximum(dtr, 0.0) + jnp.log(1.0 + jnp.exp(-jnp.abs(dtr)))
    dt_s[...] = dtv
    a_log2 = -jnp.exp(alog_ref[0]) * LOG2E

    ri = lax.broadcasted_iota(jnp.int32, (CHUNK, CHUNK), 0)
    ci = lax.broadcasted_iota(jnp.int32, (CHUNK, CHUNK), 1)
    tri_fwd = ci <= ri
    tri_bwd = ci >= ri
    is_fwd_lane8 = lax.broadcasted_iota(jnp.int32, (CHUNK, 8), 1) < HEADS_PER_GROUP
    lane_pad = jnp.zeros((CHUNK, CHUNK - 8), F32)

    tri16 = tri_fwd.astype(BF16)

    def tables(c, carry):
        r0 = pl.multiple_of(c * CHUNK, CHUNK)
        rows = pl.ds(r0, CHUNK)
        da = dt_s[rows, :] * a_log2
        p0 = da.astype(BF16)
        r1 = da - p0.astype(F32)
        p1 = r1.astype(BF16)
        p2 = (r1 - p1.astype(F32)).astype(BF16)
        packed = jnp.concatenate([p0, p1, p2, jnp.zeros((CHUNK, CHUNK - 24), BF16)], axis=1)
        acc = jnp.dot(tri16, packed, preferred_element_type=F32)
        cs_f = acc[:, 0:8] + acc[:, 8:16] + acc[:, 16:24]
        cs_b = cs_f[CHUNK - 1:CHUNK, :] - cs_f + da
        cs = jnp.where(is_fwd_lane8, cs_f, cs_b)
        cs_s[rows, :] = cs
        cs_t = jnp.concatenate([cs, lane_pad], axis=1).T[:8, :]
        dt_t = jnp.concatenate([dt_s[rows, :], lane_pad], axis=1).T[:8, :]
        cst_s[c] = cs_t - jnp.log2(dt_t)
        bt = xb_ref[0, rows, :].astype(F32).T
        bt_s[c] = bt
        cb_s[c] = jnp.dot(xc_ref[0, rows, :], bt.astype(BF16), preferred_element_type=F32)
        return carry

    lax.fori_loop(0, nc, tables, 0, unroll=4)

    for j in range(HEADS_PER_GROUP):
        hst_s[0, j] = h0f_ref[0, 0, j]
        hst_s[1, j] = h0b_ref[0, 0, j]

    def chunk_dir(d, c, final):
        tri = tri_fwd if d == 0 else tri_bwd
        tot_row = CHUNK - 1 if d == 0 else 0
        r0 = pl.multiple_of(c * CHUNK, CHUNK)
        rows = pl.ds(r0, CHUNK)
        xs = xs_ref[0, rows, :]
        cc = xc_ref[0, rows, :].astype(F32)
        cs = cs_s[rows, :]
        cs_t = cst_s[c]
        bt = bt_s[c]
        cb = cb_s[c]
        ys = []
        for j in range(HEADS_PER_GROUP):
            lane = HEADS_PER_GROUP * d + j
            a_col1 = cs[:, lane:lane + 1]
            a_col = jnp.broadcast_to(a_col1, (CHUNK, CHUNK))
            a_tot = a_col1[tot_row:tot_row + 1, :]
            a_row = cs_t[lane:lane + 1, :]
            g = (cb * jnp.where(tri, jnp.exp2(a_col - a_row), 0.0)).astype(BF16)
            cea = (cc * jnp.exp2(a_col)).astype(BF16)
            xh = xs[:, HEADDIM * j:HEADDIM * (j + 1)]
            h = hst_s[d, j]
            y_h = (jnp.dot(g, xh, preferred_element_type=F32)
                   + jnp.dot(cea, h.astype(BF16), preferred_element_type=F32))
            w_row = jnp.exp2(a_tot - a_row)
            s_new = jnp.dot((bt * w_row).astype(BF16), xh, preferred_element_type=F32)
            hst_s[d, j] = h * jnp.exp2(a_tot) + s_new
            ys.append(y_h)
        if not emit_y:
            return
        y_c = jnp.concatenate(ys, axis=1)
        if d == 0:
            y_c = y_c + dsk_ref[...] * xs.astype(F32)
        if final:
            zc = z_ref[0, rows, :].astype(F32)
            y_ref[0, rows, :] = ((yf_s[rows, :] + y_c) * _silu(zc)).astype(BF16)
        else:
            yf_s[rows, :] = y_c

    def first_half(t, carry):
        chunk_dir(0, t, False)
        chunk_dir(1, nc - 1 - t, False)
        return carry

    def second_half(t, carry):
        chunk_dir(0, t, True)
        chunk_dir(1, nc - 1 - t, True)
        return carry

    lax.fori_loop(0, nc // 2, first_half, 0, unroll=2)
    lax.fori_loop(nc // 2, nc, second_half, 0, unroll=2)
    for j in range(HEADS_PER_GROUP):
        hf_ref[0, 0, j] = hst_s[0, j]
        hb_ref[0, 0, j] = hst_s[1, j]
    if not emit_y:
        y_ref[...] = jnp.zeros(y_ref.shape, y_ref.dtype)


def _ssd_mixer(xbc, dt, z, dtb, alog, dskip, h0f, h0b, emit_y):
    b, l, _ = xbc.shape
    gw = HEADS_PER_GROUP * HEADDIM
    nxb = (N_GROUPS * gw) // D_STATE
    y_rows = l if emit_y else 8
    nc = l // CHUNK
    st_shape = (b, N_GROUPS, HEADS_PER_GROUP, D_STATE, HEADDIM)
    st_spec = pl.BlockSpec((1, 1, HEADS_PER_GROUP, D_STATE, HEADDIM), lambda bi, g: (bi, g, 0, 0, 0))
    in_specs = [
        pl.BlockSpec((1, l, gw), lambda bi, g: (bi, 0, g)),
        pl.BlockSpec((1, l, D_STATE), lambda bi, g: (bi, 0, nxb + g)),
        pl.BlockSpec((1, l, D_STATE), lambda bi, g: (bi, 0, nxb + N_GROUPS + g)),
        pl.BlockSpec((1, 1, l, 8), lambda bi, g: (bi, g, 0, 0)),
        pl.BlockSpec((1, y_rows, gw), lambda bi, g: (bi, 0, g)),
        pl.BlockSpec((1, 1, 8), lambda bi, g: (g, 0, 0)),
        pl.BlockSpec((1, 1, 8), lambda bi, g: (g, 0, 0)),
        pl.BlockSpec((1, gw), lambda bi, g: (0, g)),
        st_spec, st_spec,
    ]
    out_specs = [pl.BlockSpec((1, y_rows, gw), lambda bi, g: (bi, 0, g)), st_spec, st_spec]
    out_shape = [jax.ShapeDtypeStruct((b, y_rows, N_GROUPS * gw), BF16),
                 jax.ShapeDtypeStruct(st_shape, F32), jax.ShapeDtypeStruct(st_shape, F32)]
    scratch = [pltpu.VMEM((l, 8), F32), pltpu.VMEM((l, 8), F32),
               pltpu.VMEM((nc, 8, CHUNK), F32),
               pltpu.VMEM((nc, D_STATE, CHUNK), F32), pltpu.VMEM((nc, CHUNK, CHUNK), F32),
               pltpu.VMEM((y_rows, gw), F32),
               pltpu.VMEM((2, HEADS_PER_GROUP, D_STATE, HEADDIM), F32)]
    return pl.pallas_call(
        functools.partial(_ssd_kernel, seq=l, emit_y=emit_y),
        grid=(b, N_GROUPS),
        in_specs=in_specs, out_specs=out_specs, out_shape=out_shape,
        scratch_shapes=scratch,
        compiler_params=_params("parallel", "arbitrary"),
        name="ssd_mixer" if emit_y else "ssd_mixer_ctx",
    )(xbc, xbc, xbc, dt, z, dtb, alog, dskip, h0f, h0b)


def _fourier_kernel(f_ref, cc_ref, sc_ref, w_ref, ca_ref, sa_ref, cb_ref, sb_ref, rev_ref, o_ref,
                    u_s, v_s, cl_s, sl_s, *, seq, rows):
    i = pl.program_id(1)
    n_blk = o_ref.shape[1]

    @pl.when(i == 0)
    def _():
        scale = 1.0 / math.sqrt(seq * FOURIER_CH)
        for g in range(N_FOURIER_GROUPS):
            w = w_ref[g]
            a = _dot_split3(cc_ref[...], w) * scale
            bm = _dot_split3(sc_ref[...], w) * scale
            cols = slice(FOURIER_CH * g, FOURIER_CH * (g + 1))
            fg = f_ref[0, :, cols]
            u_s[:, cols] = jnp.dot(fg, a.astype(BF16), preferred_element_type=F32).astype(BF16)
            v_s[:, cols] = jnp.dot(fg, bm.astype(BF16), preferred_element_type=F32).astype(BF16)

    k2_0 = i * (rows // DFT_SPLIT)
    cbeta = cb_ref[...]
    sbeta = sb_ref[...]
    for j in range(rows // DFT_SPLIT + 1):
        n_sub = DFT_SPLIT if j < rows // DFT_SPLIT else DFT_EXTRA
        calpha = ca_ref[pl.ds(k2_0 + j, 1), :]
        salpha = sa_ref[pl.ds(k2_0 + j, 1), :]
        sub = slice(DFT_SPLIT * j, DFT_SPLIT * j + n_sub)
        cl_s[sub, :] = (calpha * cbeta[:n_sub] - salpha * sbeta[:n_sub]).astype(BF16)
        sl_s[sub, :] = (salpha * cbeta[:n_sub] + calpha * sbeta[:n_sub]).astype(BF16)
    p = jnp.dot(cl_s[...], u_s[...], preferred_element_type=F32)
    q = jnp.dot(sl_s[...], v_s[...], preferred_element_type=F32)
    o_ref[0, i] = (p - q)[:rows].astype(BF16)
    o_ref[0, n_blk - 1 - i] = jnp.dot(rev_ref[...], (p + q).astype(BF16),
                                      preferred_element_type=F32).astype(BF16)


def _dft_tables(n, dtype):
    k = lax.broadcasted_iota(jnp.int32, (n, n), 0)
    l = lax.broadcasted_iota(jnp.int32, (n, n), 1)
    ang = ((k * l) % n).astype(F32) * (2.0 * math.pi / n)
    return jnp.cos(ang).astype(dtype), jnp.sin(ang).astype(dtype)


def _dft_factor_tables(n):
    n2 = n // DFT_SPLIT
    k2 = lax.broadcasted_iota(jnp.int32, (n2, n), 0)
    k1 = lax.broadcasted_iota(jnp.int32, (DFT_SPLIT, n), 0)
    alpha = ((k2 * lax.broadcasted_iota(jnp.int32, (n2, n), 1)) % n2).astype(F32) * (2.0 * math.pi / n2)
    beta = ((k1 * lax.broadcasted_iota(jnp.int32, (DFT_SPLIT, n), 1)) % n).astype(F32) * (2.0 * math.pi / n)
    return jnp.cos(alpha), jnp.sin(alpha), jnp.cos(beta), jnp.sin(beta)


def _fourier_mixer(f, w_four):
    b, l, df = f.shape
    tr = 512
    n_blk = l // tr
    assert l % (2 * tr) == 0
    cc, sc = _dft_tables(FOURIER_CH, F32)
    ca, sa, cb, sb = _dft_factor_tables(l)
    rev = (lax.broadcasted_iota(jnp.int32, (tr, tr + DFT_EXTRA), 0)
           + lax.broadcasted_iota(jnp.int32, (tr, tr + DFT_EXTRA), 1) == tr).astype(BF16)
    const = lambda bi, i: (0, 0)
    out = pl.pallas_call(
        functools.partial(_fourier_kernel, seq=l, rows=tr),
        grid=(b, n_blk // 2),
        in_specs=[pl.BlockSpec((1, l, df), lambda bi, i: (bi, 0, 0)),
                  pl.BlockSpec((FOURIER_CH, FOURIER_CH), const),
                  pl.BlockSpec((FOURIER_CH, FOURIER_CH), const),
                  pl.BlockSpec(w_four.shape, lambda bi, i: (0, 0, 0)),
                  pl.BlockSpec(ca.shape, const), pl.BlockSpec(sa.shape, const),
                  pl.BlockSpec(cb.shape, const), pl.BlockSpec(sb.shape, const),
                  pl.BlockSpec(rev.shape, const)],
        out_specs=pl.BlockSpec((1, n_blk, tr, df), lambda bi, i: (bi, 0, 0, 0)),
        out_shape=jax.ShapeDtypeStruct((b, n_blk, tr, df), BF16),
        scratch_shapes=[pltpu.VMEM((l, df), BF16), pltpu.VMEM((l, df), BF16),
                        pltpu.VMEM((tr + DFT_EXTRA, l), BF16), pltpu.VMEM((tr + DFT_EXTRA, l), BF16)],
        compiler_params=_params("parallel", "arbitrary"),
        name="fourier_mixer",
    )(f, cc, sc, w_four, ca, sa, cb, sb, rev)
    return out.reshape(b, l, df)


def _outproj_router_kernel(y_ref, four_ref, x_ref, g1_ref, sh_ref, sc_ref, nssd_ref, n2_ref,
                           wos_ref, wof_ref, wr_ref, br_ref,
                           x1_ref, h_ref, eid_ref, gate_ref):
    y = y_ref[0].astype(F32)
    ms = jnp.mean(y * y, axis=-1, keepdims=True)
    yn = (y * lax.rsqrt(ms + EPS) * nssd_ref[...]).astype(BF16)
    mix = (jnp.dot(yn, wos_ref[...], preferred_element_type=F32)
           + jnp.dot(four_ref[0], wof_ref[...], preferred_element_type=F32))
    x1 = x_ref[0] + g1_ref[0] * mix
    x1_ref[0] = x1
    ms2 = jnp.mean(x1 * x1, axis=-1, keepdims=True)
    h = (x1 * lax.rsqrt(ms2 + EPS) * n2_ref[...]) * (1.0 + sc_ref[0]) + sh_ref[0]
    _store_token_tiles(h_ref, (0,), h)

    lg = _dot_split3(h, wr_ref[...]) + br_ref[...]
    tm = lg.shape[0]
    lane = lax.broadcasted_iota(jnp.int32, (tm, ROUTE_LANES), 1)
    lane_f = lane.astype(F32)
    neg = jnp.float32(-1e30)
    big = jnp.float32(1e9)
    is_grp = lane < N_EXPERT_GROUPS
    gl = jnp.where(is_grp, lg, neg)
    gmax = jnp.max(gl, axis=-1, keepdims=True)
    gsum = jnp.sum(jnp.where(is_grp, jnp.exp(gl - gmax), 0.0), axis=-1, keepdims=True)
    grp = jnp.min(jnp.where(gl == gmax, lane_f, big), axis=-1, keepdims=True)
    p_grp = 1.0 / gsum
    lo = N_EXPERT_GROUPS + EXPERTS_PER_GROUP * grp
    in_grp = jnp.logical_and(lane_f >= lo, lane_f < lo + EXPERTS_PER_GROUP)
    el = jnp.where(in_grp, lg, neg)
    m1 = jnp.max(el, axis=-1, keepdims=True)
    i1 = jnp.min(jnp.where(el == m1, lane_f, big), axis=-1, keepdims=True)
    el2 = jnp.where(lane_f == i1, neg, el)
    m2 = jnp.max(el2, axis=-1, keepdims=True)
    i2 = jnp.min(jnp.where(el2 == m2, lane_f, big), axis=-1, keepdims=True)
    e2 = jnp.exp(m2 - m1)
    den = 1.0 + e2
    gate1 = p_grp / den
    gate2 = p_grp * e2 / den
    lane8 = lax.broadcasted_iota(jnp.int32, (tm, 8), 1)
    eid = jnp.where(lane8 == 0, i1 - N_EXPERT_GROUPS, jnp.where(lane8 == 1, i2 - N_EXPERT_GROUPS, 0.0))
    eid_ref[0] = eid.astype(jnp.int32)
    gate_ref[0] = jnp.where(lane8 == 0, gate1, jnp.where(lane8 == 1, gate2, 0.0))


def _outproj_router(y, four, x, g1, sh2, sc2, nssd, n2, wos, wof, wr, br):
    b, l, d = x.shape
    tm = min(512, l)
    row = lambda bi, i: (bi, i, 0)
    vec = lambda bi, i: (bi, 0, 0)
    const = lambda bi, i: (0, 0)
    return pl.pallas_call(
        _outproj_router_kernel,
        grid=(b, l // tm),
        in_specs=[pl.BlockSpec((1, tm, y.shape[2]), row),
                  pl.BlockSpec((1, tm, four.shape[2]), row),
                  pl.BlockSpec((1, tm, d), row),
                  pl.BlockSpec((1, 1, d), vec), pl.BlockSpec((1, 1, d), vec), pl.BlockSpec((1, 1, d), vec),
                  pl.BlockSpec((1, y.shape[2]), const), pl.BlockSpec((1, d), const),
                  pl.BlockSpec(wos.shape, const), pl.BlockSpec(wof.shape, const),
                  pl.BlockSpec(wr.shape, const), pl.BlockSpec(br.shape, const)],
        out_specs=[pl.BlockSpec((1, tm, d), row), pl.BlockSpec((1, tm * TILE_ROWS, LANES), row),
                   pl.BlockSpec((1, tm, 8), row), pl.BlockSpec((1, tm, 8), row)],
        out_shape=[jax.ShapeDtypeStruct((b, l, d), F32), jax.ShapeDtypeStruct((b, l * TILE_ROWS, LANES), F32),
                   jax.ShapeDtypeStruct((b, l, 8), jnp.int32), jax.ShapeDtypeStruct((b, l, 8), F32)],
        compiler_params=_params("parallel", "arbitrary"),
        name="outproj_router",
    )(y, four, x, g1, sh2, sc2, nssd, n2, wos, wof, wr, br)


def _expert_kernel(bexp_ref, nused_ref, tokc_ref, tokn_ref, dstp_ref, dstc_ref, h_hbm, wg_ref, wu_ref, wd_ref,
                   y_hbm, hbuf, ybuf, act_s, wg_s, wu_s, wd_s, sem_in, sem_out):
    i = pl.program_id(0)
    n_used = nused_ref[0]
    active = i < n_used
    is_last = i == n_used - 1
    slot = i % 2
    other = 1 - slot
    de = wg_s.shape[1]
    d = wd_s.shape[1]
    n_phase = 8
    rows_per_phase = MOE_ROWS // n_phase
    blk_tile_rows = MOE_ROWS * TILE_ROWS

    def tile(ref, lead, row0):
        return ref.at[lead + (pl.ds(row0, TILE_ROWS), slice(None))]

    def start_gather(tok_ref, s, r):
        src = tile(h_hbm, (), pl.multiple_of(tok_ref[0, 0, r], TILE_ROWS))
        pltpu.make_async_copy(src, tile(hbuf, (s,), r * TILE_ROWS), sem_in.at[s]).start(priority=r % 2)

    def start_scatter(dst_ref, s, r):
        dst = tile(y_hbm, (), pl.multiple_of(dst_ref[0, 0, r], TILE_ROWS))
        pltpu.make_async_copy(tile(ybuf, (s,), r * TILE_ROWS), dst, sem_out.at[s]).start(priority=r % 2)

    def wait_gather(s):
        pltpu.make_async_copy(h_hbm.at[pl.ds(0, blk_tile_rows), :], hbuf.at[s], sem_in.at[s]).wait()

    def wait_scatter(s):
        pltpu.make_async_copy(ybuf.at[s], y_hbm.at[pl.ds(0, blk_tile_rows), :], sem_out.at[s]).wait()

    @pl.when(i == 0)
    def _():
        for r in range(MOE_ROWS):
            start_gather(tokc_ref, 0, r)

    prev = bexp_ref[jnp.maximum(i - 1, 0)]

    @pl.when(jnp.logical_and(active, jnp.logical_or(i == 0, bexp_ref[i] != prev)))
    def _():
        wg_s[...] = wg_ref[0].astype(BF16)
        wu_s[...] = wu_ref[0].astype(BF16)
        wd_s[...] = wd_ref[0].astype(BF16)

    @pl.when(active)
    def _():
        wait_gather(slot)

    @pl.when(jnp.logical_and(active, i >= 2))
    def _():
        wait_scatter(slot)

    def step(with_scatter, cur):
        nxt = 1 - cur
        xb = _load_token_tiles(hbuf, (cur,), MOE_ROWS).astype(BF16)
        ys = []
        for p in range(n_phase):
            for r in range(p * rows_per_phase, (p + 1) * rows_per_phase):
                start_gather(tokn_ref, nxt, r)
                if with_scatter:
                    start_scatter(dstp_ref, nxt, r)
            if p < n_phase // 2:
                w = de // (n_phase // 2)
                cols = slice(p * w, (p + 1) * w)
                gact = jnp.dot(xb, wg_s[:, cols], preferred_element_type=F32)
                up = jnp.dot(xb, wu_s[:, cols], preferred_element_type=F32)
                act_s[:, cols] = (_silu(gact) * up).astype(BF16)
            else:
                w = d // (n_phase // 2)
                cols = slice((p - n_phase // 2) * w, (p - n_phase // 2 + 1) * w)
                ys.append(jnp.dot(act_s[...], wd_s[:, cols], preferred_element_type=F32))
        _store_token_tiles(ybuf, (cur,), jnp.concatenate(ys, axis=1))

    @pl.when(i == 0)
    def _():
        step(False, 0)

    @pl.when(jnp.logical_and(active, jnp.logical_and(i > 0, slot == 0)))
    def _():
        step(True, 0)

    @pl.when(jnp.logical_and(active, slot == 1))
    def _():
        step(True, 1)

    @pl.when(is_last)
    def _():
        for r in range(MOE_ROWS):
            start_scatter(dstc_ref, slot, r)
        wait_gather(other)
        wait_scatter(other)
        wait_scatter(slot)

    @pl.when(jnp.logical_not(active))
    def _():
        ybuf[0] = jnp.zeros((blk_tile_rows, LANES), F32)
        dst0 = pl.multiple_of(dstc_ref[0, 0, 0], TILE_ROWS)
        fill = pltpu.make_async_copy(ybuf.at[0], y_hbm.at[pl.ds(dst0, blk_tile_rows), :], sem_out.at[0])
        fill.start()
        fill.wait()


def _experts(h_tiles, row_tok, row_dst, blk_exp, n_used, w_eg, w_eu, w_ed):
    n_blocks = blk_exp.shape[0]
    d, de = w_eg.shape[1], w_eg.shape[2]
    assert d == TILE_ROWS * LANES
    idx_shape = (n_blocks, 1, MOE_ROWS)
    idx_block = (1, 1, MOE_ROWS)
    smem = pltpu.SMEM
    blk_tile_rows = MOE_ROWS * TILE_ROWS
    grid_spec = pltpu.PrefetchScalarGridSpec(
        num_scalar_prefetch=2,
        grid=(n_blocks,),
        in_specs=[pl.BlockSpec(idx_block, lambda i, be, nu: (i, 0, 0), memory_space=smem),
                  pl.BlockSpec(idx_block, lambda i, be, nu: (jnp.minimum(i + 1, n_blocks - 1), 0, 0),
                               memory_space=smem),
                  pl.BlockSpec(idx_block, lambda i, be, nu: (jnp.maximum(i - 1, 0), 0, 0), memory_space=smem),
                  pl.BlockSpec(idx_block, lambda i, be, nu: (i, 0, 0), memory_space=smem),
                  pl.BlockSpec(memory_space=pl.ANY),
                  pl.BlockSpec((1, d, de), lambda i, be, nu: (be[i], 0, 0)),
                  pl.BlockSpec((1, d, de), lambda i, be, nu: (be[i], 0, 0)),
                  pl.BlockSpec((1, de, d), lambda i, be, nu: (be[i], 0, 0))],
        out_specs=pl.BlockSpec(memory_space=pl.ANY),
        scratch_shapes=[pltpu.VMEM((2, blk_tile_rows, LANES), F32), pltpu.VMEM((2, blk_tile_rows, LANES), F32),
                        pltpu.VMEM((MOE_ROWS, de), BF16),
                        pltpu.VMEM((d, de), BF16), pltpu.VMEM((d, de), BF16), pltpu.VMEM((de, d), BF16),
                        pltpu.SemaphoreType.DMA((2,)), pltpu.SemaphoreType.DMA((2,))],
    )
    row_tok = (row_tok * TILE_ROWS).reshape(idx_shape)
    row_dst = (row_dst * TILE_ROWS).reshape(idx_shape)
    return pl.pallas_call(
        _expert_kernel,
        grid_spec=grid_spec,
        out_shape=jax.ShapeDtypeStruct((n_blocks * blk_tile_rows, LANES), F32),
        compiler_params=_params("arbitrary"),
        name="moe_experts",
    )(blk_exp, n_used, row_tok, row_tok, row_dst, row_dst, h_tiles, w_eg, w_eu, w_ed)


def _route_tables(eid, n_tok):
    n_assign = 2 * n_tok
    e_flat = jnp.concatenate([eid[:, 0], eid[:, 1]])
    order = jnp.argsort(e_flat).astype(jnp.int32)
    experts = jnp.arange(N_EXPERTS, dtype=jnp.int32)
    counts = jnp.sum((e_flat[:, None] == experts[None, :]).astype(jnp.int32), axis=0)
    start = jnp.cumsum(counts) - counts
    padded = (counts + MOE_ROWS - 1) // MOE_ROWS * MOE_ROWS
    end_pad = jnp.cumsum(padded)
    start_pad = end_pad - padded
    n_blocks = -(-(n_assign + N_EXPERTS * (MOE_ROWS - 1)) // MOE_ROWS)
    blk_row0 = jnp.arange(n_blocks, dtype=jnp.int32) * MOE_ROWS
    blk_exp = jnp.minimum(jnp.sum((end_pad[None, :] <= blk_row0[:, None]).astype(jnp.int32), axis=1),
                          N_EXPERTS - 1).astype(jnp.int32)
    j = (blk_row0 - start_pad[blk_exp])[:, None] + jnp.arange(MOE_ROWS, dtype=jnp.int32)[None, :]
    valid = j < counts[blk_exp][:, None]
    src = jnp.clip(start[blk_exp][:, None] + j, 0, n_assign - 1)
    assign = order[src.reshape(-1)].reshape(src.shape)
    row_tok = jnp.where(valid, jnp.where(assign >= n_tok, assign - n_tok, assign), 0)
    spare = n_assign + jnp.cumsum((~valid).reshape(-1).astype(jnp.int32)).reshape(valid.shape) - 1
    row_dst = jnp.where(valid, assign, spare)
    n_used = (end_pad[-1:] // MOE_ROWS).astype(jnp.int32)
    return row_tok.astype(jnp.int32), row_dst.astype(jnp.int32), blk_exp, n_used


def _combine_kernel(x1_ref, y0_ref, y1_ref, gate_ref, g2_ref, nw_ref, o_ref):
    gate = gate_ref[0]
    tm = gate.shape[0]
    moe = (gate[:, 0:1] * _load_token_tiles(y0_ref, (), tm) + gate[:, 1:2] * _load_token_tiles(y1_ref, (), tm))
    x2 = x1_ref[0] + g2_ref[0] * moe
    ms = jnp.mean(x2 * x2, axis=-1, keepdims=True)
    o_ref[0] = x2 * lax.rsqrt(ms + EPS) * nw_ref[...]


def _combine(x1, y_assign, gate, g2, final_norm):
    b, l, d = x1.shape
    tm = min(512, l)
    nt = l // tm
    row = lambda bi, i: (bi, i, 0)
    return pl.pallas_call(
        _combine_kernel,
        grid=(b, nt),
        in_specs=[pl.BlockSpec((1, tm, d), row),
                  pl.BlockSpec((tm * TILE_ROWS, LANES), lambda bi, i: (bi * nt + i, 0)),
                  pl.BlockSpec((tm * TILE_ROWS, LANES), lambda bi, i: (b * nt + bi * nt + i, 0)),
                  pl.BlockSpec((1, tm, 8), row),
                  pl.BlockSpec((1, 1, d), lambda bi, i: (bi, 0, 0)),
                  pl.BlockSpec((1, d), lambda bi, i: (0, 0))],
        out_specs=pl.BlockSpec((1, tm, d), row),
        out_shape=jax.ShapeDtypeStruct((b, l, d), F32),
        compiler_params=_params("parallel", "arbitrary"),
        name="moe_combine",
    )(x1, y_assign, y_assign, gate, g2, final_norm.reshape(1, d))


def _group_major(v):
    return v.reshape(2, N_GROUPS, HEADS_PER_GROUP).transpose(1, 0, 2).reshape(N_GROUPS, 1, 2 * HEADS_PER_GROUP)


def kernel(x, c, ctx, c_ctx, w_mod, b_mod, norm1, w_in, conv_w, conv_b, dt_bias, a_log, d_skip, ssd_norm,
           w_four, w_out, norm2, w_rg, b_rg, w_re, b_re, w_eg, w_eu, w_ed, final_norm):
    bsz, seq, d = x.shape
    n_tok = bsz * seq
    d_ssd = N_GROUPS * HEADS_PER_GROUP * HEADDIM
    conv_dim = d_ssd + 2 * N_GROUPS * D_STATE
    n_heads = N_GROUPS * HEADS_PER_GROUP
    layer = 0

    c_rows = jnp.zeros((16, d), F32).at[:bsz].set(c).at[bsz].set(c_ctx)
    mod = _modulation(c_rows, w_mod[layer], b_mod[layer])
    sh1, sc1, g1, sh2, sc2, g2 = [m[:bsz, None, :] for m in jnp.split(mod, 6, axis=-1)]
    sh1c, sc1c = [jnp.broadcast_to(m[bsz][None, None, :], (bsz, 1, d)) for m in jnp.split(mod, 6, axis=-1)[:2]]

    w = w_in[layer]
    wz = w[:, :d_ssd].astype(BF16)
    wx = w[:, d_ssd:d_ssd + conv_dim].astype(BF16)
    wdt = w[:, d_ssd + conv_dim:d_ssd + conv_dim + 2 * n_heads]
    wdt = wdt.reshape(d, 2, N_GROUPS, HEADS_PER_GROUP).transpose(0, 2, 1, 3).reshape(d, 2 * n_heads)
    wdt = jnp.pad(wdt, ((0, 0), (0, 128 - 2 * n_heads))).astype(BF16)
    wf = w[:, d_ssd + conv_dim + 2 * n_heads:].astype(BF16)
    n1 = norm1[layer].reshape(1, d)

    dtb = _group_major(dt_bias[layer])
    alog = _group_major(a_log[layer])
    dsk = jnp.repeat(d_skip[layer], HEADDIM).reshape(1, d_ssd)
    cw = conv_w[layer]
    cb = conv_b[layer].reshape(1, conv_dim)

    xbc_c, dt_c = _in_projection(ctx, sh1c, sc1c, n1, wx, wdt, cw, cb)
    h_zero = jnp.zeros((bsz, N_GROUPS, HEADS_PER_GROUP, D_STATE, HEADDIM), F32)
    z_dummy = jnp.zeros((bsz, 8, d_ssd), BF16)
    _, hf_c, hb_c = _ssd_mixer(xbc_c, dt_c, z_dummy, dtb, alog, dsk, h_zero, h_zero, emit_y=False)

    xbc_l, dt_l, z_l, f_l = _in_projection(x, sh1, sc1, n1, wx, wdt, cw, cb, wz, wf)
    y_l, _, _ = _ssd_mixer(xbc_l, dt_l, z_l, dtb, alog, dsk, hf_c, hb_c, emit_y=True)
    four = _fourier_mixer(f_l, w_four[layer])

    wo = w_out[layer]
    wr = jnp.concatenate([w_rg[layer], w_re[layer].transpose(1, 0, 2).reshape(d, N_EXPERTS)], axis=1)
    wr = jnp.pad(wr, ((0, 0), (0, ROUTE_LANES - wr.shape[1])))
    br = jnp.pad(jnp.concatenate([b_rg[layer], b_re[layer].reshape(-1)]),
                 (0, ROUTE_LANES - N_EXPERT_GROUPS - N_EXPERTS)).reshape(1, ROUTE_LANES)
    x1, h, eid, gate = _outproj_router(
        y_l, four, x, g1, sh2, sc2, ssd_norm[layer].reshape(1, d_ssd), norm2[layer].reshape(1, d),
        wo[:d_ssd].astype(BF16), wo[d_ssd:].astype(BF16), wr, br)

    row_tok, row_dst, blk_exp, n_used = _route_tables(eid.reshape(n_tok, 8), n_tok)
    y_assign = _experts(h.reshape(n_tok * TILE_ROWS, LANES), row_tok, row_dst, blk_exp, n_used,
                        w_eg[layer], w_eu[layer], w_ed[layer])
    return _combine(x1, y_assign, gate, g2, final_norm)
```
